```python
import math
import jax, jax.numpy as jnp
from jax import lax
import numpy as np

D_MODEL = 1024
BATCH = 8
SEQ = 2048
DEPTH = 2

HEAD_DIM = 64
N_HEADS_A = 4
N_HEADS_B = 6
N_HEADS_C = 6
WIDTH_A = N_HEADS_A * HEAD_DIM
WIDTH_B = N_HEADS_B * HEAD_DIM
WIDTH_C = N_HEADS_C * HEAD_DIM
MIX_WIDTH = WIDTH_A + WIDTH_B + WIDTH_C
DIFF_DIM = HEAD_DIM // 2
ROPE_THETA = 500000.0
ROPE_FRACTION = 4
Q_BLOCK = 128
DILATED_CONFIGS = ((128, 1), (512, 4), (2048, 16))
SW_BLOCK = 64
GRID_W = 64
NA_ROWS_MAX = 8
NA_COLS = 16
D_FF_DENSE = 2816
N_EXPERTS = 8
TOP_K = 2
D_FF_EXPERT = 3584
N_DENSE = (DEPTH + 1) // 2
N_MOE = DEPTH // 2
DEEPNORM_ALPHA = (2 * DEPTH) ** 0.25
DEEPNORM_BETA = (8 * DEPTH) ** -0.25
LN_EPS = 1e-5
NEG = -1e30

kernel_name = 'hybrid_diff_dilated_natten_moe_encoder'


def layer_norm(x, g, b):
    xf = x.astype(jnp.float32)
    mu = jnp.mean(xf, axis=-1, keepdims=True)
    var = jnp.mean(jnp.square(xf - mu), axis=-1, keepdims=True)
    y = (xf - mu) * lax.rsqrt(var + LN_EPS) * g.astype(jnp.float32) + b.astype(jnp.float32)
    return y.astype(x.dtype)


def rms_norm(x, g):
    xf = x.astype(jnp.float32)
    y = xf * lax.rsqrt(jnp.mean(xf * xf, axis=-1, keepdims=True) + LN_EPS) * g.astype(jnp.float32)
    return y.astype(x.dtype)


def rope_tables(seq, rot_dim):
    inv = ROPE_THETA ** (-jnp.arange(0, rot_dim, 2, dtype=jnp.float32) / rot_dim)
    ang = jnp.arange(seq, dtype=jnp.float32)[:, None] * inv[None, :]
    return jnp.cos(ang), jnp.sin(ang)


def apply_partial_rope(x, cos, sin):
    half = cos.shape[-1]
    c = cos.astype(x.dtype)
    s = sin.astype(x.dtype)
    x1, x2, xp = x[..., :half], x[..., half:2 * half], x[..., 2 * half:]
    return jnp.concatenate([x1 * c - x2 * s, x2 * c + x1 * s, xp], axis=-1)


def diff_attention(q, k, v, lam, lam_init, norm_g):
    B, H, _, S, dq = q.shape
    nb = S // Q_BLOCK
    qb = q.reshape(B, H, 2, nb, Q_BLOCK, dq).transpose(3, 0, 1, 2, 4, 5)
    scale = dq ** -0.5

    def block(q_blk):
        s = jnp.einsum('bhmqd,bhmkd->bhmqk', q_blk, k).astype(jnp.float32) * scale
        p = jax.nn.softmax(s, axis=-1)
        pd = p[:, :, 0] - lam * p[:, :, 1]
        return jnp.einsum('bhqk,bhkd->bhqd', pd.astype(v.dtype), v)

    o = lax.map(block, qb)
    o = o.transpose(1, 2, 0, 3, 4).reshape(B, H, S, v.shape[-1])
    return rms_norm(o, norm_g) * (1.0 - lam_init)


def dilated_branch(q, k, v, window, dilation):
    B, H, S, dh = q.shape
    half = window // (2 * dilation)
    assert half <= SW_BLOCK
    L = S // dilation
    nb = -(-L // SW_BLOCK)
    Lp = nb * SW_BLOCK

    def by_residue(t):
        return t.reshape(B, H, L, dilation, dh).transpose(0, 1, 3, 2, 4)

    pad0 = ((0, 0), (0, 0), (0, 0))
    qs = jnp.pad(by_residue(q), pad0 + ((0, Lp - L), (0, 0)))
    ks = jnp.pad(by_residue(k), pad0 + ((SW_BLOCK, Lp - L + SW_BLOCK), (0, 0)))
    vs = jnp.pad(by_residue(v), pad0 + ((SW_BLOCK, Lp - L + SW_BLOCK), (0, 0)))
    qb = qs.reshape(B, H, dilation, nb, SW_BLOCK, dh)

    def band(t):
        tb = t.reshape(B, H, dilation, nb + 2, SW_BLOCK, dh)
        return jnp.concatenate([tb[:, :, :, j:j + nb] for j in range(3)], axis=-2)

    kb, vb = band(ks), band(vs)
    qpos = np.arange(nb)[:, None] * SW_BLOCK + np.arange(SW_BLOCK)[None, :]
    kpos = (np.arange(nb)[:, None] - 1) * SW_BLOCK + np.arange(3 * SW_BLOCK)[None, :]
    delta = kpos[:, None, :] - qpos[:, :, None]
    mask = (np.abs(delta) <= half) & (kpos[:, None, :] >= 0) & (kpos[:, None, :] < L)
    s = jnp.einsum('bhrnqd,bhrnkd->bhrnqk', qb, kb).astype(jnp.float32) * dh ** -0.5
    s = jnp.where(mask, s, NEG)
    lse = jax.nn.logsumexp(s, axis=-1)
    p = jnp.exp(s - lse[..., None])
    o = jnp.einsum('bhrnqk,bhrnkd->bhrnqd', p.astype(v.dtype), vb)
    o = o.reshape(B, H, dilation, Lp, dh)[:, :, :, :L].transpose(0, 1, 3, 2, 4).reshape(B, H, S, dh)
    lse = lse.reshape(B, H, dilation, Lp)[:, :, :, :L].transpose(0, 1, 3, 2).reshape(B, H, S)
    return o, lse


def dilated_attention(q, k, v):
    outs, lses = [], []
    for window, dilation in DILATED_CONFIGS:
        o, lse = dilated_branch(q, k, v, window, dilation)
        outs.append(o)
        lses.append(lse)
    wts = jax.nn.softmax(jnp.stack(lses, axis=0), axis=0)
    return jnp.einsum('gbhs,gbhsd->bhsd', wts.astype(v.dtype), jnp.stack(outs, axis=0))


def neighbourhood_attention(q, k, v, rpb):
    B, H, S, dh = q.shape
    rows = S // GRID_W
    kh = min(NA_ROWS_MAX, rows)
    kw = NA_COLS
    r = np.arange(rows)
    row_start = np.clip(r - kh // 2, 0, rows - kh)
    key_rows = row_start[:, None] + np.arange(kh)[None, :]
    tok = (key_rows[:, :, None] * GRID_W + np.arange(GRID_W)[None, None, :]).reshape(rows, kh * GRID_W)
    kg = jnp.take(k, tok, axis=2)
    vg = jnp.take(v, tok, axis=2)
    qg = q.reshape(B, H, rows, GRID_W, dh)
    c = np.arange(GRID_W)
    col_start = np.clip(c - kw // 2, 0, GRID_W - kw)
    kcol = np.tile(np.arange(GRID_W), kh)
    valid = (kcol[None, :] >= col_start[:, None]) & (kcol[None, :] < col_start[:, None] + kw)
    ridx = np.repeat(key_rows, GRID_W, axis=1) - r[:, None] + NA_ROWS_MAX - 1
    cidx = np.clip(kcol[None, :] - c[:, None] + kw - 1, 0, 2 * kw - 2)
    bias = rpb[:, ridx[:, None, :], cidx[None, :, :]].astype(jnp.float32)
    bias = jnp.where(valid[None, None], bias, NEG)
    s = jnp.einsum('bhrqd,bhrkd->bhrqk', qg, kg).astype(jnp.float32) * dh ** -0.5 + bias[None]
    p = jax.nn.softmax(s, axis=-1)
    o = jnp.einsum('bhrqk,bhrkd->bhrqd', p.astype(v.dtype), vg)
    return o.reshape(B, H, S, dh)


def hybrid_mixer(x, w_in, w_out, beta_mix, diff_lambda, diff_norm_g, na_rpb, lam_init, rope_a, rope_b):
    B, S, _ = x.shape
    proj = x @ w_in
    pa, pb, pc = jnp.split(proj, [3 * WIDTH_A, 3 * WIDTH_A + 3 * WIDTH_B], axis=-1)

    def heads(t, n):
        return t.reshape(B, S, n, HEAD_DIM).transpose(0, 2, 1, 3)

    def diff_heads(t):
        return t.reshape(B, S, N_HEADS_A, 2, DIFF_DIM).transpose(0, 2, 3, 1, 4)

    qa, ka, va = jnp.split(pa, 3, axis=-1)
    qa = apply_partial_rope(diff_heads(qa), *rope_a)
    ka = apply_partial_rope(diff_heads(ka), *rope_a)
    lam_p = diff_lambda.astype(jnp.float32)
    lam = jnp.exp(jnp.sum(lam_p[0] * lam_p[1])) - jnp.exp(jnp.sum(lam_p[2] * lam_p[3])) + lam_init
    oa = diff_attention(qa, ka, heads(va, N_HEADS_A), lam, lam_init, diff_norm_g)

    qb, kb, vb = jnp.split(pb, 3, axis=-1)
    qb = apply_partial_rope(heads(qb, N_HEADS_B), *rope_b)
    kb = apply_partial_rope(heads(kb, N_HEADS_B), *rope_b)
    ob = dilated_attention(qb, kb, heads(vb, N_HEADS_B))

    qc, kc, vc = jnp.split(pc, 3, axis=-1)
    oc = neighbourhood_attention(heads(qc, N_HEADS_C), heads(kc, N_HEADS_C), heads(vc, N_HEADS_C), na_rpb)

    merged = jnp.concatenate([oa, ob, oc], axis=1).transpose(0, 2, 1, 3).reshape(B, S, MIX_WIDTH)
    return (merged * beta_mix) @ w_out


def swiglu(x, w_gate, w_up, w_down):
    return (jax.nn.silu(x @ w_gate) * (x @ w_up)) @ w_down


def moe_swiglu(x, w_router, w_gate, w_up, w_down):
    logits = (x @ w_router).astype(jnp.float32)
    top_v, top_i = lax.top_k(logits, TOP_K)
    gates = jax.nn.softmax(top_v, axis=-1)
    combine = jnp.sum(jax.nn.one_hot(top_i, N_EXPERTS, dtype=jnp.float32) * gates[..., None], axis=-2)
    out = jnp.zeros_like(x)
    for e in range(N_EXPERTS):
        out = out + combine[..., e:e + 1].astype(x.dtype) * swiglu(x, w_gate[e], w_up[e], w_down[e])
    return out


def setup_inputs(seed: int = 0) -> dict:
    key = jax.random.key(seed)
    ks = jax.random.split(key, 18)
    D = D_MODEL
    f32 = jnp.float32

    def nrm(k, shape, scale):
        return jax.random.normal(k, shape, f32) * scale

    col_scale = np.ones(3 * MIX_WIDTH, np.float32)
    off = 0
    for w in (WIDTH_A, WIDTH_B, WIDTH_C):
        col_scale[off + 2 * w: off + 3 * w] = DEEPNORM_BETA
        off += 3 * w
    return {
        'x': nrm(ks[0], (BATCH, SEQ, D), 1.0),
        'w_in': nrm(ks[1], (DEPTH, D, 3 * MIX_WIDTH), D ** -0.5) * jnp.asarray(col_scale),
        'w_out': nrm(ks[2], (DEPTH, MIX_WIDTH, D), MIX_WIDTH ** -0.5 * DEEPNORM_BETA),
        'beta_mix': 1.0 + nrm(ks[3], (DEPTH, MIX_WIDTH), 0.02),
        'diff_lambda': nrm(ks[4], (DEPTH, 4, DIFF_DIM), 0.1),
        'diff_norm_g': 1.0 + nrm(ks[5], (DEPTH, HEAD_DIM), 0.02),
        'na_rpb': nrm(ks[6], (DEPTH, N_HEADS_C, 2 * NA_ROWS_MAX - 1, 2 * NA_COLS - 1), 0.02),
        'ln1_g': 1.0 + nrm(ks[7], (DEPTH, D), 0.02),
        'ln1_b': nrm(ks[8], (DEPTH, D), 0.02),
        'ln2_g': 1.0 + nrm(ks[9], (DEPTH, D), 0.02),
        'ln2_b': nrm(ks[10], (DEPTH, D), 0.02),
        'ffn_w_gate': nrm(ks[11], (N_DENSE, D, D_FF_DENSE), D ** -0.5),
        'ffn_w_up': nrm(ks[12], (N_DENSE, D, D_FF_DENSE), D ** -0.5),
        'ffn_w_down': nrm(ks[13], (N_DENSE, D_FF_DENSE, D), D_FF_DENSE ** -0.5 * DEEPNORM_BETA),
        'moe_w_router': nrm(ks[14], (N_MOE, D, N_EXPERTS), D ** -0.5),
        'moe_w_gate': nrm(ks[15], (N_MOE, N_EXPERTS, D, D_FF_EXPERT), D ** -0.5),
        'moe_w_up': nrm(ks[16], (N_MOE, N_EXPERTS, D, D_FF_EXPERT), D ** -0.5),
        'moe_w_down': nrm(ks[17], (N_MOE, N_EXPERTS, D_FF_EXPERT, D), D_FF_EXPERT ** -0.5 * DEEPNORM_BETA),
    }


def reference(x, w_in, w_out, beta_mix, diff_lambda, diff_norm_g, na_rpb, ln1_g, ln1_b, ln2_g, ln2_b,
              ffn_w_gate, ffn_w_up, ffn_w_down, moe_w_router, moe_w_gate, moe_w_up, moe_w_down):
    S = x.shape[1]
    rope_a = rope_tables(S, DIFF_DIM // ROPE_FRACTION)
    rope_b = rope_tables(S, HEAD_DIM // ROPE_FRACTION)
    for l in range(DEPTH):
        lam_init = 0.8 - 0.6 * math.exp(-0.3 * l)
        mix = hybrid_mixer(x, w_in[l], w_out[l], beta_mix[l], diff_lambda[l], diff_norm_g[l], na_rpb[l],
                           lam_init, rope_a, rope_b)
        x = layer_norm(DEEPNORM_ALPHA * x + mix, ln1_g[l], ln1_b[l])
        if l % 2 == 0:
            j = l // 2
            f = swiglu(x, ffn_w_gate[j], ffn_w_up[j], ffn_w_down[j])
        else:
            j = l // 2
            f = moe_swiglu(x, moe_w_router[j], moe_w_gate[j], moe_w_up[j], moe_w_down[j])
        x = layer_norm(DEEPNORM_ALPHA * x + f, ln2_g[l], ln2_b[l])
    return x
```

```python
import functools
import math

import jax
import jax.numpy as jnp
import numpy as np
from jax import lax
from jax.experimental import pallas as pl
from jax.experimental.pallas import tpu as pltpu

F32 = jnp.float32
BF16 = jnp.bfloat16

D_MODEL = 1024
HEAD_DIM = 64
N_HEADS_A, N_HEADS_B, N_HEADS_C = 4, 6, 6
WIDTH_A, WIDTH_B, WIDTH_C = N_HEADS_A * HEAD_DIM, N_HEADS_B * HEAD_DIM, N_HEADS_C * HEAD_DIM
DIFF_DIM = HEAD_DIM // 2
ROPE_THETA = 500000.0
ROPE_FRACTION = 4
DILATIONS = (1, 4, 16)
BAND_HALF = 64
GRID_W = 64
NA_ROWS = 8
NA_COLS = 16
N_EXPERTS = 8
TOP_K = 2
LN_EPS = 1e-5
NEG = -1e30

LANES = 128
SUBLANES = 8
VMEM_LIMIT = 56 * 1024 * 1024

PROJ_W = 3 * (WIDTH_A + WIDTH_B + WIDTH_C)
ROPE_COLS = 3 * WIDTH_A + 2 * WIDTH_B
COL_CHUNK = 3 * WIDTH_A


def _cparams(sem):
    return pltpu.CompilerParams(dimension_semantics=sem, vmem_limit_bytes=VMEM_LIMIT)


def _layer_norm(z, g, b):
    mu = jnp.mean(z, axis=-1, keepdims=True)
    zc = z - mu
    var = jnp.mean(zc * zc, axis=-1, keepdims=True)
    return zc * lax.rsqrt(var + LN_EPS) * g + b


def _rope_tables(seq):
    rot_a = DIFF_DIM // ROPE_FRACTION
    rot_b = HEAD_DIM // ROPE_FRACTION
    pos = jnp.arange(seq, dtype=F32)[:, None]
    inv_a = ROPE_THETA ** (-jnp.arange(0, rot_a, 2, dtype=F32) / rot_a)
    inv_b = ROPE_THETA ** (-jnp.arange(0, rot_b, 2, dtype=F32) / rot_b)
    ang_a, ang_b = pos * inv_a[None, :], pos * inv_b[None, :]

    cols = np.arange(ROPE_COLS)
    in_a = cols < 3 * WIDTH_A
    sub_a = cols // WIDTH_A
    within_a = cols % DIFF_DIM
    rot_mask_a = in_a & (sub_a < 2) & (within_a < rot_a)
    cb = cols - 3 * WIDTH_A
    sub_b = cb // WIDTH_B
    within_b = cb % HEAD_DIM
    rot_mask_b = (~in_a) & (within_b < rot_b)

    freq = np.where(in_a, within_a % (rot_a // 2), within_b % (rot_b // 2))
    first_half = np.where(in_a, within_a < rot_a // 2, within_b < rot_b // 2)
    rot = rot_mask_a | rot_mask_b
    scale = np.ones(ROPE_COLS, np.float32)
    scale[in_a & (sub_a == 0)] = DIFF_DIM ** -0.5
    scale[(~in_a) & (sub_b == 0)] = HEAD_DIM ** -0.5

    freq_a = np.where(in_a, freq, 0)
    freq_b = np.where(in_a, 0, freq)
    cos = jnp.where(jnp.asarray(in_a)[None, :], jnp.cos(ang_a)[:, freq_a], jnp.cos(ang_b)[:, freq_b])
    sin = jnp.where(jnp.asarray(in_a)[None, :], jnp.sin(ang_a)[:, freq_a], jnp.sin(ang_b)[:, freq_b])
    rot_j = jnp.asarray(rot)[None, :]
    cm = jnp.where(rot_j, cos, 1.0) * jnp.asarray(scale)[None, :]
    sn = jnp.where(rot_j, sin, 0.0) * jnp.asarray(scale)[None, :]
    ma = np.where(rot & first_half, -1.0, 0.0).astype(np.float32)[None, :]
    mb = np.where(rot & ~first_half, 1.0, 0.0).astype(np.float32)[None, :]
    return cm, sn, jnp.asarray(ma), jnp.asarray(mb)


PROJ_TM = 512
W_CAST_ROWS = 128


def _cast_rows_to_bf16(src_ref, dst_ref, rows):
    def body(i, c):
        r = pl.multiple_of(i * W_CAST_ROWS, W_CAST_ROWS)
        dst_ref[pl.ds(r, W_CAST_ROWS), :] = src_ref[pl.ds(r, W_CAST_ROWS), :].astype(BF16)
        return c
    lax.fori_loop(0, rows // W_CAST_ROWS, body, 0)


def _proj_kernel(x_ref, w_ref, cm_ref, sn_ref, ma_ref, mb_ref, o_ref, wbf_ref):
    @pl.when((pl.program_id(0) == 0) & (pl.program_id(1) == 0))
    def _():
        _cast_rows_to_bf16(w_ref, wbf_ref, D_MODEL)

    xb = x_ref[0].astype(BF16)
    n_chunks = PROJ_W // COL_CHUNK
    for c in range(n_chunks):
        lo, hi = c * COL_CHUNK, (c + 1) * COL_CHUNK
        acc = jnp.dot(xb, wbf_ref[:, lo:hi], preferred_element_type=F32)
        if hi <= ROPE_COLS:
            half = (DIFF_DIM if c == 0 else HEAD_DIM) // ROPE_FRACTION // 2
            up = pltpu.roll(acc, COL_CHUNK - half, 1)
            dn = pltpu.roll(acc, half, 1)
            acc = acc * cm_ref[:, lo:hi] + (up * ma_ref[:, lo:hi] + dn * mb_ref[:, lo:hi]) * sn_ref[:, lo:hi]
        elif c == 2:
            lane = lax.broadcasted_iota(jnp.int32, (1, COL_CHUNK), 1)
            acc = acc * jnp.where(lane >= WIDTH_B, HEAD_DIM ** -0.5, 1.0)
        o_ref[0, :, lo:hi] = acc.astype(BF16)


def _project(x, w, tables):
    B, S, _ = x.shape
    cm, sn, ma, mb = tables
    tm = PROJ_TM
    return pl.pallas_call(
        _proj_kernel,
        grid=(S // tm, B),
        in_specs=[
            pl.BlockSpec((1, tm, D_MODEL), lambda p, b: (b, p, 0)),
            pl.BlockSpec((D_MODEL, PROJ_W), lambda p, b: (0, 0), pipeline_mode=pl.Buffered(1)),
            pl.BlockSpec((tm, ROPE_COLS), lambda p, b: (p, 0)),
            pl.BlockSpec((tm, ROPE_COLS), lambda p, b: (p, 0)),
            pl.BlockSpec((1, ROPE_COLS), lambda p, b: (0, 0)),
            pl.BlockSpec((1, ROPE_COLS), lambda p, b: (0, 0)),
        ],
        out_specs=pl.BlockSpec((1, tm, PROJ_W), lambda p, b: (b, p, 0)),
        out_shape=jax.ShapeDtypeStruct((B, S, PROJ_W), BF16),
        scratch_shapes=[pltpu.VMEM((D_MODEL, PROJ_W), BF16)],
        compiler_params=_cparams(("arbitrary", "arbitrary")),
        name="proj_in",
    )(x, w, cm, sn, ma, mb)


ATT_A_TQ = 256


def _nt_dot(a, b):
    return lax.dot_general(a, b, (((1,), (1,)), ((), ())), preferred_element_type=F32)


def _attn_a_kernel(lam_ref, g_ref, q_ref, k_ref, v_ref, o_ref, *, lam_init, seq):
    lp = lam_ref[...]
    lam = (jnp.exp(jnp.sum(lp[0:1] * lp[1:2], axis=-1, keepdims=True))
           - jnp.exp(jnp.sum(lp[2:3] * lp[3:4], axis=-1, keepdims=True)) + lam_init)
    lane = lax.broadcasted_iota(jnp.int32, (1, LANES), 1)
    first = lane < HEAD_DIM
    gain = g_ref[...] * (1.0 - lam_init)
    tq = ATT_A_TQ

    def qblock(i, carry):
        r0 = pl.multiple_of(i * tq, tq)
        q = q_ref[0, pl.ds(r0, tq), :]
        k = k_ref[0]
        v = v_ref[0]
        outs = []
        for hh in range(2):
            probs = []
            for m in range(2):
                lo = hh * HEAD_DIM + m * DIFF_DIM
                qm = jnp.where((lane >= lo) & (lane < lo + DIFF_DIM), q, jnp.zeros_like(q))
                s = _nt_dot(qm, k)
                p = jnp.exp(s - jnp.max(s, axis=-1, keepdims=True))
                probs.append((p, jnp.sum(p, axis=-1, keepdims=True)))
            (p0, l0), (p1, l1) = probs
            pd = p0 * (1.0 / l0) - p1 * (lam / l1)
            outs.append(jnp.dot(pd.astype(BF16), v, preferred_element_type=F32))
        o = jnp.where(first, outs[0], outs[1])
        sq = o * o
        ss0 = jnp.sum(jnp.where(first, sq, 0.0), axis=-1, keepdims=True)
        ss1 = jnp.sum(jnp.where(first, 0.0, sq), axis=-1, keepdims=True)
        ms = jnp.where(first, ss0, ss1) * (1.0 / HEAD_DIM)
        o_ref[0, pl.ds(r0, tq), :] = (o * lax.rsqrt(ms + LN_EPS) * gain).astype(BF16)
        return carry

    lax.fori_loop(0, seq // tq, qblock, 0)


def _attn_a(qkv, diff_lambda, norm_g, lam_init):
    B, S, _ = qkv.shape
    n_pairs = WIDTH_A // LANES
    g2 = jnp.tile(norm_g.astype(F32), LANES // HEAD_DIM)[None, :]
    blk = lambda off: pl.BlockSpec((1, S, LANES), lambda b, h, off=off: (b, 0, off + h))
    return pl.pallas_call(
        functools.partial(_attn_a_kernel, lam_init=lam_init, seq=S),
        grid=(B, n_pairs),
        in_specs=[
            pl.BlockSpec((4, DIFF_DIM), lambda b, h: (0, 0)),
            pl.BlockSpec((1, LANES), lambda b, h: (0, 0)),
            blk(0), blk(n_pairs), blk(2 * n_pairs),
        ],
        out_specs=pl.BlockSpec((1, S, LANES), lambda b, h: (b, 0, h)),
        out_shape=jax.ShapeDtypeStruct((B, S, WIDTH_A), BF16),
        compiler_params=_cparams(("arbitrary", "arbitrary")),
        name="attn_diff",
    )(diff_lambda.astype(F32), g2, qkv, qkv, qkv)


BAND_TQ = 128


def _stack_pair(q, lane):
    z = jnp.zeros_like(q)
    return jnp.concatenate([jnp.where(lane < HEAD_DIM, q, z), jnp.where(lane < HEAD_DIM, z, q)], axis=0)


def _attn_b_kernel(q_ref, k_ref, v_ref, o_ref, lse_ref, *, length):
    tq = BAND_TQ
    win = min(tq + 2 * BAND_HALF, length)
    lane = lax.broadcasted_iota(jnp.int32, (1, LANES), 1)
    for i in range(length // tq):
        q0 = i * tq
        k0 = min(max(q0 - BAND_HALF, 0), length - win)
        row = lax.broadcasted_iota(jnp.int32, (2 * tq, win), 0)
        qpos = q0 + jnp.where(row >= tq, row - tq, row)
        kpos = k0 + lax.broadcasted_iota(jnp.int32, (2 * tq, win), 1)
        band2 = jnp.abs(kpos - qpos) <= BAND_HALF
        lse_tile = jnp.zeros((tq, LANES), F32)
        for pr in range(WIDTH_B // LANES):
            cs = slice(pr * LANES, (pr + 1) * LANES)
            q2 = _stack_pair(q_ref[0, q0:q0 + tq, cs], lane)
            s = _nt_dot(q2, k_ref[0, k0:k0 + win, cs])
            s = jnp.where(band2, s, NEG)
            mx = jnp.max(s, axis=-1, keepdims=True)
            p = jnp.exp(s - mx)
            l = jnp.sum(p, axis=-1, keepdims=True)
            o2 = jnp.dot(p.astype(BF16), v_ref[0, k0:k0 + win, cs], preferred_element_type=F32) * (1.0 / l)
            o_ref[0, q0:q0 + tq, cs] = jnp.where(lane < HEAD_DIM, o2[:tq], o2[tq:]).astype(BF16)
            lse2 = mx + jnp.log(l)
            lse_tile = jnp.where(lane == 2 * pr, lse2[:tq], lse_tile)
            lse_tile = jnp.where(lane == 2 * pr + 1, lse2[tq:], lse_tile)
        lse_ref[0, q0:q0 + tq, :] = lse_tile


def _attn_b(qkv, dil):
    B, S, _ = qkv.shape
    L = S // dil
    view = qkv.reshape(B, L, dil * PROJ_W)
    per_res = PROJ_W // WIDTH_B
    q_blk = 3 * WIDTH_A // WIDTH_B
    blk = lambda off: pl.BlockSpec((1, L, WIDTH_B), lambda b, r, off=off: (b, 0, r * per_res + off))
    o, lse = pl.pallas_call(
        functools.partial(_attn_b_kernel, length=L),
        grid=(B, dil),
        in_specs=[blk(q_blk), blk(q_blk + 1), blk(q_blk + 2)],
        out_specs=[pl.BlockSpec((1, L, WIDTH_B), lambda b, r: (b, 0, r)),
                   pl.BlockSpec((1, L, LANES), lambda b, r: (b, 0, r))],
        out_shape=[jax.ShapeDtypeStruct((B, L, dil * WIDTH_B), BF16),
                   jax.ShapeDtypeStruct((B, L, dil * LANES), F32)],
        compiler_params=_cparams(("arbitrary", "arbitrary")),
        name=f"attn_band_d{dil}",
    )(view, view, view)
    return o.reshape(B * S, WIDTH_B), lse.reshape(B * S, LANES)


N_RPB_ROWS = 2 * NA_ROWS - 1
N_RPB_COLS = 2 * NA_COLS - 1
NA_KEYS = NA_ROWS * GRID_W
N_BIAS_VARIANTS = NA_ROWS


def _attn_c_kernel(u_ref, q_ref, k_ref, v_ref, o_ref, bias_ref, *, rows):
    lane = lax.broadcasted_iota(jnp.int32, (1, LANES), 1)
    n_pairs = WIDTH_C // LANES

    @pl.when(pl.program_id(0) == 0)
    def _():
        wide = N_RPB_ROWS * GRID_W
        c = lax.broadcasted_iota(jnp.int32, (GRID_W, wide), 0)
        kc = lax.broadcasted_iota(jnp.int32, (GRID_W, wide), 1) % GRID_W
        tidx = jnp.clip(kc - c + NA_COLS - 1, 0, N_RPB_COLS - 1)
        c5 = lax.broadcasted_iota(jnp.int32, (GRID_W, NA_KEYS), 0)
        kc5 = lax.broadcasted_iota(jnp.int32, (GRID_W, NA_KEYS), 1) % GRID_W
        cstart = jnp.clip(c5 - NA_COLS // 2, 0, GRID_W - NA_COLS)
        valid = (kc5 >= cstart) & (kc5 < cstart + NA_COLS)
        for h in range(N_HEADS_C):
            def tbody(t, acc, h=h):
                row = u_ref[h, pl.ds(t, 1), :]
                return jnp.where(tidx == t, row, acc)
            toep = lax.fori_loop(0, N_RPB_COLS, tbody, jnp.zeros((GRID_W, wide), F32))
            for var in range(N_BIAS_VARIANTS):
                tile = jnp.where(valid, toep[:, var * GRID_W:var * GRID_W + NA_KEYS], NEG)
                bias_ref[var, h // 2, (h % 2) * GRID_W:(h % 2 + 1) * GRID_W, :] = tile

    def rbody(r, carry):
        rs = jnp.clip(r - NA_ROWS // 2, 0, rows - NA_ROWS)
        var = rs - r + NA_ROWS - 1
        q0 = pl.multiple_of(r * GRID_W, GRID_W)
        k0 = pl.multiple_of(rs * GRID_W, GRID_W)
        for pr in range(n_pairs):
            cs = slice(pr * LANES, (pr + 1) * LANES)
            q2 = _stack_pair(q_ref[0, pl.ds(q0, GRID_W), cs], lane)
            s = _nt_dot(q2, k_ref[0, pl.ds(k0, NA_KEYS), cs]) + bias_ref[var, pr]
            p = jnp.exp(s - jnp.max(s, axis=-1, keepdims=True))
            l = jnp.sum(p, axis=-1, keepdims=True)
            o2 = jnp.dot(p.astype(BF16), v_ref[0, pl.ds(k0, NA_KEYS), cs], preferred_element_type=F32) * (1.0 / l)
            o_ref[0, pl.ds(q0, GRID_W), cs] = jnp.where(lane < HEAD_DIM, o2[:GRID_W], o2[GRID_W:]).astype(BF16)
        return carry

    lax.fori_loop(0, rows, rbody, 0)


def _attn_c(qkv, rpb):
    B, S, _ = qkv.shape
    rows = S // GRID_W
    assert rows >= NA_ROWS
    u = jnp.repeat(jnp.transpose(rpb.astype(F32), (0, 2, 1)), GRID_W, axis=2)
    q_blk = (3 * WIDTH_A + 3 * WIDTH_B) // WIDTH_C
    blk = lambda off: pl.BlockSpec((1, S, WIDTH_C), lambda b, off=off: (b, 0, off))
    return pl.pallas_call(
        functools.partial(_attn_c_kernel, rows=rows),
        grid=(B,),
        in_specs=[pl.BlockSpec((N_HEADS_C, N_RPB_COLS, N_RPB_ROWS * GRID_W), lambda b: (0, 0, 0)),
                  blk(q_blk), blk(q_blk + 1), blk(q_blk + 2)],
        out_specs=pl.BlockSpec((1, S, WIDTH_C), lambda b: (b, 0, 0)),
        out_shape=jax.ShapeDtypeStruct((B, S, WIDTH_C), BF16),
        scratch_shapes=[pltpu.VMEM((N_BIAS_VARIANTS, WIDTH_C // LANES, LANES, NA_KEYS), F32)],
        compiler_params=_cparams(("arbitrary",)),
        name="attn_nbr",
    )(u, qkv, qkv, qkv)


OUT_TM = 512


def _split_bf16(x):
    hi = x.astype(BF16)
    return hi, (x - hi.astype(F32)).astype(BF16)


def _outproj_kernel(x_ref, oa_ref, ob0_ref, ob1_ref, ob2_ref, l0_ref, l1_ref, l2_ref, oc_ref,
                    beta_ref, w_ref, g_ref, b_ref, e_ref, o_ref, wbf_ref, *, alpha, tile_out):
    @pl.when(pl.program_id(0) == 0)
    def _():
        _cast_rows_to_bf16(w_ref, wbf_ref, D_MODEL)

    l0, l1, l2 = l0_ref[...], l1_ref[...], l2_ref[...]
    lm = jnp.maximum(jnp.maximum(l0, l1), l2)
    e0, e1, e2 = jnp.exp(l0 - lm), jnp.exp(l1 - lm), jnp.exp(l2 - lm)
    inv = 1.0 / (e0 + e1 + e2)
    ob = jnp.zeros((x_ref.shape[0], WIDTH_B), F32)
    for e, o_r in ((e0, ob0_ref), (e1, ob1_ref), (e2, ob2_ref)):
        hi, lo = _split_bf16(e * inv)
        wexp = (jnp.dot(hi, e_ref[...], preferred_element_type=F32)
                + jnp.dot(lo, e_ref[...], preferred_element_type=F32))
        ob = ob + wexp * o_r[...].astype(F32)

    beta = beta_ref[...]
    ma = (oa_ref[...].astype(F32) * beta[:, :WIDTH_A]).astype(BF16)
    mb = (ob * beta[:, WIDTH_A:WIDTH_A + WIDTH_B]).astype(BF16)
    mc = (oc_ref[...].astype(F32) * beta[:, WIDTH_A + WIDTH_B:]).astype(BF16)
    acc = jnp.dot(ma, wbf_ref[0:WIDTH_A, :], preferred_element_type=F32)
    acc = acc + jnp.dot(mb, wbf_ref[WIDTH_A:WIDTH_A + WIDTH_B, :], preferred_element_type=F32)
    acc = acc + jnp.dot(mc, wbf_ref[WIDTH_A + WIDTH_B:, :], preferred_element_type=F32)
    y = _layer_norm(alpha * x_ref[...] + acc, g_ref[...], b_ref[...])
    if tile_out:
        _store_token_tiles(o_ref, y)
    else:
        o_ref[...] = y


def _store_token_tiles(ref, y):
    rows = y.shape[0]
    for j in range(D_MODEL // LANES):
        ref[pl.ds(j, rows, stride=SUBLANES), :] = y[:, j * LANES:(j + 1) * LANES]


def _load_token_tiles(ref, rows, row0=0):
    return jnp.concatenate(
        [ref[pl.ds(row0 * SUBLANES + j, rows, stride=SUBLANES), :] for j in range(D_MODEL // LANES)], axis=1)


def _outproj(x2, oa, obs, lses, oc, beta, w, g, b, alpha, tile_out):
    N = x2.shape[0]
    tm = OUT_TM
    expand = np.zeros((LANES, WIDTH_B), np.float32)
    for h in range(N_HEADS_B):
        expand[h, h * HEAD_DIM:(h + 1) * HEAD_DIM] = 1.0
    row = lambda w_: pl.BlockSpec((tm, w_), lambda i: (i, 0))
    full = lambda r, c, **kw: pl.BlockSpec((r, c), lambda i: (0, 0), **kw)
    if tile_out:
        out_spec = pl.BlockSpec((tm * SUBLANES, LANES), lambda i: (i, 0))
        out_shape = jax.ShapeDtypeStruct((N * SUBLANES, LANES), F32)
    else:
        out_spec = row(D_MODEL)
        out_shape = jax.ShapeDtypeStruct((N, D_MODEL), F32)
    return pl.pallas_call(
        functools.partial(_outproj_kernel, alpha=alpha, tile_out=tile_out),
        grid=(N // tm,),
        in_specs=[row(D_MODEL), row(WIDTH_A), row(WIDTH_B), row(WIDTH_B), row(WIDTH_B),
                  row(LANES), row(LANES), row(LANES), row(WIDTH_C),
                  full(1, D_MODEL), full(D_MODEL, D_MODEL, pipeline_mode=pl.Buffered(1)),
                  full(1, D_MODEL), full(1, D_MODEL), full(LANES, WIDTH_B)],
        out_specs=out_spec,
        out_shape=out_shape,
        scratch_shapes=[pltpu.VMEM((D_MODEL, D_MODEL), BF16)],
        compiler_params=_cparams(("arbitrary",)),
        name="merge_outproj_ln",
    )(x2, oa, obs[0], obs[1], obs[2], lses[0], lses[1], lses[2], oc,
      beta[None, :].astype(F32), w, g[None, :].astype(F32), b[None, :].astype(F32),
      jnp.asarray(expand, BF16))


FFN_TM = 1024


def _ffn_kernel(te_ref, nt_ref, x_ref, wg_ref, wu_ref, wd_ref, *rest, n_chunks, grouped, alpha):
    if grouped:
        o_ref, xb_ref, acc_ref = rest
    else:
        g_ref, b_ref, o_ref, xb_ref, acc_ref = rest
    i, j = pl.program_id(0), pl.program_id(1)
    tm = xb_ref.shape[0]

    @pl.when(i < nt_ref[0])
    def _():
        @pl.when(j == 0)
        def _():
            if grouped:
                xb_ref[...] = _load_token_tiles(x_ref, tm).astype(BF16)
            else:
                xb_ref[...] = x_ref[...].astype(BF16)
            acc_ref[...] = jnp.zeros_like(acc_ref)

        xb = xb_ref[...]
        gate = jnp.dot(xb, wg_ref[...].astype(BF16), preferred_element_type=F32)
        up = jnp.dot(xb, wu_ref[...].astype(BF16), preferred_element_type=F32)
        h = (gate * jax.nn.sigmoid(gate) * up).astype(BF16)
        acc_ref[...] += jnp.dot(h, wd_ref[...].astype(BF16), preferred_element_type=F32)

        @pl.when(j == n_chunks - 1)
        def _():
            if grouped:
                _store_token_tiles(o_ref, acc_ref[...])
            else:
                o_ref[...] = _layer_norm(alpha * x_ref[...] + acc_ref[...], g_ref[...], b_ref[...])

    @pl.when((i >= nt_ref[0]) & (j == n_chunks - 1))
    def _():
        o_ref[...] = jnp.zeros_like(o_ref)


def _ffn(x, w_gate, w_up, w_down, tile_expert, n_tiles, tf, *, grouped, ln=None, alpha=None):
    tm = FFN_TM
    F = w_gate.shape[-1]
    n_chunks = F // tf
    max_tiles = tile_expert.shape[0]

    def tile_ix(i, nt):
        return jnp.minimum(i, nt[0] - 1)

    def chunk_ix(i, j, nt):
        return jnp.where(i < nt[0], j, n_chunks - 1)

    blk = (tm * SUBLANES, LANES) if grouped else (tm, D_MODEL)
    x_spec = pl.BlockSpec(blk, lambda i, j, te, nt: (tile_ix(i, nt), 0))
    out_spec = pl.BlockSpec(blk, lambda i, j, te, nt: (i, 0))
    out_shape = jax.ShapeDtypeStruct(x.shape, F32)
    in_specs = [
        x_spec,
        pl.BlockSpec((None, D_MODEL, tf), lambda i, j, te, nt: (te[tile_ix(i, nt)], 0, chunk_ix(i, j, nt))),
        pl.BlockSpec((None, D_MODEL, tf), lambda i, j, te, nt: (te[tile_ix(i, nt)], 0, chunk_ix(i, j, nt))),
        pl.BlockSpec((None, tf, D_MODEL), lambda i, j, te, nt: (te[tile_ix(i, nt)], chunk_ix(i, j, nt), 0)),
    ]
    args = [x, w_gate, w_up, w_down]
    if not grouped:
        in_specs += [pl.BlockSpec((1, D_MODEL), lambda i, j, te, nt: (0, 0))] * 2
        args += [ln[0][None, :].astype(F32), ln[1][None, :].astype(F32)]
    return pl.pallas_call(
        functools.partial(_ffn_kernel, n_chunks=n_chunks, grouped=grouped, alpha=alpha),
        grid_spec=pltpu.PrefetchScalarGridSpec(
            num_scalar_prefetch=2,
            grid=(max_tiles, n_chunks),
            in_specs=in_specs,
            out_specs=out_spec,
            scratch_shapes=[pltpu.VMEM((tm, D_MODEL), BF16), pltpu.VMEM((tm, D_MODEL), F32)],
        ),
        out_shape=out_shape,
        compiler_params=_cparams(("arbitrary", "arbitrary")),
        name="swiglu_grouped" if grouped else "swiglu_dense",
    )(tile_expert, n_tiles, *args)


ROUTE_TB = 512
META_E1, META_E2, META_R1, META_R2, META_G1, META_G2 = range(6)


def _router_kernel(x_ref, wr_ref, meta_ref, cnt_ref, carry_ref):
    tb = ROUTE_TB

    @pl.when(pl.program_id(0) == 0)
    def _():
        carry_ref[...] = jnp.zeros_like(carry_ref)

    x = _load_token_tiles(x_ref, tb)
    xh, xl = _split_bf16(x)
    wh, wl = _split_bf16(wr_ref[...])
    logits = (jnp.dot(xh, wh, preferred_element_type=F32) + jnp.dot(xh, wl, preferred_element_type=F32)
              + jnp.dot(xl, wh, preferred_element_type=F32))
    lane = lax.broadcasted_iota(jnp.int32, (tb, LANES), 1).astype(F32)
    logits = jnp.where(lane < N_EXPERTS, logits, -jnp.inf)
    m1 = jnp.max(logits, axis=-1, keepdims=True)
    i1 = jnp.min(jnp.where(logits == m1, lane, float(LANES)), axis=-1, keepdims=True)
    rest = jnp.where(lane == i1, -jnp.inf, logits)
    m2 = jnp.max(rest, axis=-1, keepdims=True)
    i2 = jnp.min(jnp.where(rest == m2, lane, float(LANES)), axis=-1, keepdims=True)
    e21 = jnp.exp(m2 - m1)
    g1 = 1.0 / (1.0 + e21)
    g2 = e21 * g1

    member = ((lane == i1) | (lane == i2)).astype(F32)
    r_i = lax.broadcasted_iota(jnp.int32, (tb, tb), 0)
    c_i = lax.broadcasted_iota(jnp.int32, (tb, tb), 1)
    lower = (c_i < r_i).astype(BF16)
    before = jnp.dot(lower, member.astype(BF16), preferred_element_type=F32)
    rank = carry_ref[...] + before
    r1 = jnp.sum(jnp.where(lane == i1, rank, 0.0), axis=-1, keepdims=True)
    r2 = jnp.sum(jnp.where(lane == i2, rank, 0.0), axis=-1, keepdims=True)
    carry_ref[...] += jnp.sum(member, axis=0, keepdims=True)
    cnt_ref[...] = carry_ref[...]

    meta = jnp.zeros((tb, LANES), F32)
    for slot, val in ((META_E1, i1.astype(F32)), (META_E2, i2.astype(F32)), (META_R1, r1), (META_R2, r2),
                      (META_G1, g1), (META_G2, g2)):
        meta = jnp.where(lane == slot, val, meta)
    meta_ref[...] = meta


def _router(xt, w_router):
    n_tok = xt.shape[0] // SUBLANES
    tb = ROUTE_TB
    wr = jnp.zeros((D_MODEL, LANES), F32).at[:, :N_EXPERTS].set(w_router.astype(F32))
    return pl.pallas_call(
        _router_kernel,
        grid=(n_tok // tb,),
        in_specs=[pl.BlockSpec((tb * SUBLANES, LANES), lambda i: (i, 0)),
                  pl.BlockSpec((D_MODEL, LANES), lambda i: (0, 0))],
        out_specs=[pl.BlockSpec((tb, LANES), lambda i: (i, 0)),
                   pl.BlockSpec((1, LANES), lambda i: (0, 0))],
        out_shape=[jax.ShapeDtypeStruct((n_tok, LANES), F32), jax.ShapeDtypeStruct((1, LANES), F32)],
        scratch_shapes=[pltpu.VMEM((1, LANES), F32)],
        compiler_params=_cparams(("arbitrary",)),
        name="router_top2",
    )(xt, wr)


MOVE_TB = 1024


def _tile_rows(ref, idx):
    return ref.at[pl.ds(pl.multiple_of(idx * SUBLANES, SUBLANES), SUBLANES), :]


def _dispatch_kernel(dest_ref, x_hbm, xs_in, xs_hbm, sem):
    del xs_in
    tb = MOVE_TB
    base = pl.program_id(0) * tb

    def body(t, c):
        src = _tile_rows(x_hbm, base + t)
        for k in range(TOP_K):
            pltpu.make_async_copy(src, _tile_rows(xs_hbm, dest_ref[0, 0, TOP_K * t + k]), sem).start()
        return c

    lax.fori_loop(0, tb, body, 0)
    n_rows = TOP_K * tb * SUBLANES
    pltpu.make_async_copy(xs_hbm.at[pl.ds(0, n_rows), :], xs_hbm.at[pl.ds(0, n_rows), :], sem).wait()


def _dispatch(xt, dest, n_slots):
    n_tok = xt.shape[0] // SUBLANES
    tb = MOVE_TB
    dest3 = dest.reshape(n_tok // tb, 1, TOP_K * tb)
    zeros = jnp.zeros((n_slots * SUBLANES, LANES), F32)
    return pl.pallas_call(
        _dispatch_kernel,
        grid=(n_tok // tb,),
        in_specs=[pl.BlockSpec((1, 1, TOP_K * tb), lambda i: (i, 0, 0), memory_space=pltpu.SMEM),
                  pl.BlockSpec(memory_space=pl.ANY),
                  pl.BlockSpec(memory_space=pl.ANY)],
        out_specs=pl.BlockSpec(memory_space=pl.ANY),
        out_shape=jax.ShapeDtypeStruct(zeros.shape, F32),
        scratch_shapes=[pltpu.SemaphoreType.DMA(())],
        input_output_aliases={2: 0},
        compiler_params=_cparams(("arbitrary",)),
        name="moe_dispatch",
    )(dest3, xt, zeros)


COMB_TB = 256


def _combine_kernel(dest_ref, y_hbm, x_ref, meta_ref, g_ref, b_ref, o_ref, ybuf, sem, *, alpha):
    tb = COMB_TB

    def body(t, c):
        for k in range(TOP_K):
            pltpu.make_async_copy(_tile_rows(y_hbm, dest_ref[0, 0, TOP_K * t + k]),
                                  _tile_rows(ybuf, k * tb + t), sem).start()
        return c

    lax.fori_loop(0, tb, body, 0)
    pltpu.make_async_copy(ybuf, ybuf, sem).wait()

    meta = meta_ref[...]
    lane = lax.broadcasted_iota(jnp.int32, meta.shape, 1)
    g1 = jnp.sum(jnp.where(lane == META_G1, meta, 0.0), axis=-1, keepdims=True)
    g2 = jnp.sum(jnp.where(lane == META_G2, meta, 0.0), axis=-1, keepdims=True)
    x = _load_token_tiles(x_ref, tb)
    f = g1 * _load_token_tiles(ybuf, tb) + g2 * _load_token_tiles(ybuf, tb, row0=tb)
    o_ref[...] = _layer_norm(alpha * x + f, g_ref[...], b_ref[...])


def _combine(yt, xt, dest, meta, g, b, alpha):
    n_tok = xt.shape[0] // SUBLANES
    tb = COMB_TB
    dest3 = dest.reshape(n_tok // tb, 1, TOP_K * tb)
    return pl.pallas_call(
        functools.partial(_combine_kernel, alpha=alpha),
        grid=(n_tok // tb,),
        in_specs=[pl.BlockSpec((1, 1, TOP_K * tb), lambda i: (i, 0, 0), memory_space=pltpu.SMEM),
                  pl.BlockSpec(memory_space=pl.ANY),
                  pl.BlockSpec((tb * SUBLANES, LANES), lambda i: (i, 0)),
                  pl.BlockSpec((tb, LANES), lambda i: (i, 0)),
                  pl.BlockSpec((1, D_MODEL), lambda i: (0, 0)),
                  pl.BlockSpec((1, D_MODEL), lambda i: (0, 0))],
        out_specs=pl.BlockSpec((tb, D_MODEL), lambda i: (i, 0)),
        out_shape=jax.ShapeDtypeStruct((n_tok, D_MODEL), F32),
        scratch_shapes=[pltpu.VMEM((TOP_K * tb * SUBLANES, LANES), F32), pltpu.SemaphoreType.DMA(())],
        compiler_params=_cparams(("arbitrary",)),
        name="moe_combine_ln",
    )(dest3, yt, xt, meta, g[None, :].astype(F32), b[None, :].astype(F32))


def _moe_plan(meta, counts, n_tok):
    tm = FFN_TM
    max_tiles = n_tok * TOP_K // tm + N_EXPERTS
    cnt = counts[0, :N_EXPERTS].astype(jnp.int32)
    tiles = (cnt + tm - 1) // tm
    tile_end = jnp.cumsum(tiles)
    start = (tile_end - tiles) * tm
    e = meta[:, META_E1:META_E2 + 1].astype(jnp.int32)
    r = meta[:, META_R1:META_R2 + 1].astype(jnp.int32)
    dest = (start[e] + r).reshape(-1)
    tile_ids = jnp.arange(max_tiles, dtype=jnp.int32)
    tile_expert = jnp.minimum(jnp.sum((tile_ids[:, None] >= tile_end[None, :]).astype(jnp.int32), axis=1),
                              N_EXPERTS - 1)
    return dest, tile_expert, tile_end[-1:].astype(jnp.int32), max_tiles * tm


def kernel(x, w_in, w_out, beta_mix, diff_lambda, diff_norm_g, na_rpb, ln1_g, ln1_b, ln2_g, ln2_b,
           ffn_w_gate, ffn_w_up, ffn_w_down, moe_w_router, moe_w_gate, moe_w_up, moe_w_down):
    B, S, D = x.shape
    assert D == D_MODEL and S % (max(DILATIONS) * BAND_TQ) == 0 and S % PROJ_TM == 0
    depth = w_in.shape[0]
    alpha = (2 * depth) ** 0.25
    n_tok = B * S
    tables = _rope_tables(S)
    x = x.astype(F32)
    for l in range(depth):
        lam_init = 0.8 - 0.6 * math.exp(-0.3 * l)
        moe = l % 2 == 1
        qkv = _project(x, w_in[l], tables)
        oa = _attn_a(qkv, diff_lambda[l], diff_norm_g[l], lam_init).reshape(n_tok, WIDTH_A)
        band = [_attn_b(qkv, d) for d in DILATIONS]
        oc = _attn_c(qkv, na_rpb[l]).reshape(n_tok, WIDTH_C)
        x1 = _outproj(x.reshape(n_tok, D), oa, [o for o, _ in band], [s for _, s in band], oc,
                      beta_mix[l], w_out[l], ln1_g[l], ln1_b[l], alpha, tile_out=moe)
        j = l // 2
        if not moe:
            n_tiles = n_tok // FFN_TM
            x = _ffn(x1, ffn_w_gate[j][None], ffn_w_up[j][None], ffn_w_down[j][None],
                     jnp.zeros((n_tiles,), jnp.int32), jnp.full((1,), n_tiles, jnp.int32), 256,
                     grouped=False, ln=(ln2_g[l], ln2_b[l]), alpha=alpha)
        else:
            meta, counts = _router(x1, moe_w_router[j])
            dest, tile_expert, n_tiles, n_slots = _moe_plan(meta, counts, n_tok)
            xs = _dispatch(x1, dest, n_slots)
            ys = _ffn(xs, moe_w_gate[j], moe_w_up[j], moe_w_down[j], tile_expert, n_tiles, 512, grouped=True)
            x = _combine(ys, x1, dest, meta, ln2_g[l], ln2_b[l], alpha)
        x = x.reshape(B, S, D)
    return x
```

```python
import functools
import math

import jax
import jax.numpy as jnp
import numpy as np
from jax import lax
from jax.experimental import pallas as pl
from jax.experimental.pallas import tpu as pltpu

F32 = jnp.float32
BF16 = jnp.bfloat16

D_MODEL = 1024
HEAD_DIM = 64
N_HEADS_A, N_HEADS_B, N_HEADS_C = 4, 6, 6
WIDTH_A, WIDTH_B, WIDTH_C = N_HEADS_A * HEAD_DIM, N_HEADS_B * HEAD_DIM, N_HEADS_C * HEAD_DIM
DIFF_DIM = HEAD_DIM // 2
ROPE_THETA = 500000.0
ROPE_FRACTION = 4
DILATIONS = (1, 4, 16)
BAND_HALF = 64
GRID_W = 64
NA_ROWS = 8
NA_COLS = 16
N_EXPERTS = 8
TOP_K = 2
LN_EPS = 1e-5
NEG = -1e30
LOG2E = math.log2(math.e)

LANES = 128
SUBLANES = 8
VMEM_LIMIT = 56 * 1024 * 1024

PROJ_W = 3 * (WIDTH_A + WIDTH_B + WIDTH_C)
ROPE_COLS = 3 * WIDTH_A + 2 * WIDTH_B
COL_CHUNK = 3 * WIDTH_A
BAND_LO = 3 * WIDTH_A
BAND_COLS = 3 * WIDTH_B


def _cparams(sem):
    return pltpu.CompilerParams(dimension_semantics=sem, vmem_limit_bytes=VMEM_LIMIT)


def _layer_norm(z, g, b):
    mu = jnp.mean(z, axis=-1, keepdims=True)
    zc = z - mu
    var = jnp.mean(zc * zc, axis=-1, keepdims=True)
    return zc * lax.rsqrt(var + LN_EPS) * g + b


def _rope_tables(seq):
    rot_a = DIFF_DIM // ROPE_FRACTION
    rot_b = HEAD_DIM // ROPE_FRACTION
    pos = jnp.arange(seq, dtype=F32)[:, None]
    inv_a = ROPE_THETA ** (-jnp.arange(0, rot_a, 2, dtype=F32) / rot_a)
    inv_b = ROPE_THETA ** (-jnp.arange(0, rot_b, 2, dtype=F32) / rot_b)
    ang_a, ang_b = pos * inv_a[None, :], pos * inv_b[None, :]

    cols = np.arange(ROPE_COLS)
    in_a = cols < 3 * WIDTH_A
    sub_a = cols // WIDTH_A
    within_a = cols % DIFF_DIM
    rot_mask_a = in_a & (sub_a < 2) & (within_a < rot_a)
    cb = cols - 3 * WIDTH_A
    sub_b = cb // WIDTH_B
    within_b = cb % HEAD_DIM
    rot_mask_b = (~in_a) & (within_b < rot_b)

    freq = np.where(in_a, within_a % (rot_a // 2), within_b % (rot_b // 2))
    first_half = np.where(in_a, within_a < rot_a // 2, within_b < rot_b // 2)
    rot = rot_mask_a | rot_mask_b
    scale = np.ones(ROPE_COLS, np.float32)
    scale[in_a & (sub_a == 0)] = DIFF_DIM ** -0.5 * LOG2E
    scale[(~in_a) & (sub_b == 0)] = HEAD_DIM ** -0.5

    freq_a = np.where(in_a, freq, 0)
    freq_b = np.where(in_a, 0, freq)
    cos = jnp.where(jnp.asarray(in_a)[None, :], jnp.cos(ang_a)[:, freq_a], jnp.cos(ang_b)[:, freq_b])
    sin = jnp.where(jnp.asarray(in_a)[None, :], jnp.sin(ang_a)[:, freq_a], jnp.sin(ang_b)[:, freq_b])
    rot_j = jnp.asarray(rot)[None, :]
    cm = jnp.where(rot_j, cos, 1.0) * jnp.asarray(scale)[None, :]
    sn = jnp.where(rot_j, sin, 0.0) * jnp.asarray(scale)[None, :]
    ma = np.where(rot & first_half, -1.0, 0.0).astype(np.float32)[None, :]
    mb = np.where(rot & ~first_half, 1.0, 0.0).astype(np.float32)[None, :]
    return cm, sn, jnp.asarray(ma), jnp.asarray(mb)


PROJ_TM = 512
W_CAST_ROWS = 128


def _cast_rows_to_bf16(src_ref, dst_ref, rows):
    def body(i, c):
        r = pl.multiple_of(i * W_CAST_ROWS, W_CAST_ROWS)
        dst_ref[pl.ds(r, W_CAST_ROWS), :] = src_ref[pl.ds(r, W_CAST_ROWS), :].astype(BF16)
        return c
    lax.fori_loop(0, rows // W_CAST_ROWS, body, 0)


def _proj_kernel(x_ref, w_ref, cm_ref, sn_ref, ma_ref, mb_ref, o_ref, *rest):
    res_refs, (wbf_ref, band_ref) = rest[:-2], rest[-2:]

    @pl.when((pl.program_id(0) == 0) & (pl.program_id(1) == 0))
    def _():
        _cast_rows_to_bf16(w_ref, wbf_ref, D_MODEL)

    xb = x_ref[0].astype(BF16)
    tm = xb.shape[0]
    n_chunks = PROJ_W // COL_CHUNK
    for c in range(n_chunks):
        lo, hi = c * COL_CHUNK, (c + 1) * COL_CHUNK
        acc = jnp.dot(xb, wbf_ref[:, lo:hi], preferred_element_type=F32)
        if hi <= ROPE_COLS:
            half = (DIFF_DIM if c == 0 else HEAD_DIM) // ROPE_FRACTION // 2
            up = pltpu.roll(acc, COL_CHUNK - half, 1)
            dn = pltpu.roll(acc, half, 1)
            acc = acc * cm_ref[:, lo:hi] + (up * ma_ref[:, lo:hi] + dn * mb_ref[:, lo:hi]) * sn_ref[:, lo:hi]
        elif c == 2:
            lane = lax.broadcasted_iota(jnp.int32, (1, COL_CHUNK), 1)
            acc = acc * jnp.where(lane >= WIDTH_B, HEAD_DIM ** -0.5, 1.0)
        o_ref[0, :, lo:hi] = acc.astype(BF16)
        for col in range(max(lo, BAND_LO), min(hi, BAND_LO + BAND_COLS), LANES):
            band_ref[(col - BAND_LO) // LANES] = acc[:, col - lo:col - lo + LANES]

    for dil, r_ref in zip(DILATIONS[1:], res_refs):
        for r in range(dil):
            for page in range(BAND_COLS // LANES):
                r_ref[0, r, :, page * LANES:(page + 1) * LANES] = (
                    band_ref[page, pl.ds(r, tm // dil, stride=dil), :].astype(BF16))


def _project(x, w, tables):
    B, S, _ = x.shape
    cm, sn, ma, mb = tables
    tm = PROJ_TM
    res_specs = [pl.BlockSpec((1, d, tm // d, BAND_COLS), lambda p, b: (b, 0, p, 0)) for d in DILATIONS[1:]]
    res_shapes = [jax.ShapeDtypeStruct((B, d, S // d, BAND_COLS), BF16) for d in DILATIONS[1:]]
    return pl.pallas_call(
        _proj_kernel,
        grid=(S // tm, B),
        in_specs=[
            pl.BlockSpec((1, tm, D_MODEL), lambda p, b: (b, p, 0)),
            pl.BlockSpec((D_MODEL, PROJ_W), lambda p, b: (0, 0), pipeline_mode=pl.Buffered(1)),
            pl.BlockSpec((tm, ROPE_COLS), lambda p, b: (p, 0)),
            pl.BlockSpec((tm, ROPE_COLS), lambda p, b: (p, 0)),
            pl.BlockSpec((1, ROPE_COLS), lambda p, b: (0, 0)),
            pl.BlockSpec((1, ROPE_COLS), lambda p, b: (0, 0)),
        ],
        out_specs=[pl.BlockSpec((1, tm, PROJ_W), lambda p, b: (b, p, 0))] + res_specs,
        out_shape=[jax.ShapeDtypeStruct((B, S, PROJ_W), BF16)] + res_shapes,
        scratch_shapes=[pltpu.VMEM((D_MODEL, PROJ_W), BF16),
                        pltpu.VMEM((BAND_COLS // LANES, tm, LANES), F32)],
        compiler_params=_cparams(("arbitrary", "arbitrary")),
        name="proj_in",
    )(x, w, cm, sn, ma, mb)


ATT_A_TQ = 256
ATT_A_UNROLL = 4


def _nt_dot(a, b):
    return lax.dot_general(a, b, (((1,), (1,)), ((), ())), preferred_element_type=F32)


def _with_ones(v):
    return jnp.concatenate([v, jnp.ones_like(v)], axis=1)


def _softmax_times_v(e, v1):
    nd = jnp.dot(e, v1, preferred_element_type=F32)
    return nd[:, :LANES] * (1.0 / nd[:, LANES:])


def _attn_a_kernel(lam_ref, g_ref, q_ref, k_ref, v_ref, o_ref, *, lam_init, seq):
    lp = lam_ref[...]
    lam = (jnp.exp(jnp.sum(lp[0:1] * lp[1:2], axis=-1, keepdims=True))
           - jnp.exp(jnp.sum(lp[2:3] * lp[3:4], axis=-1, keepdims=True)) + lam_init)
    lane = lax.broadcasted_iota(jnp.int32, (1, LANES), 1)
    first = lane < HEAD_DIM
    gain = g_ref[...] * (1.0 - lam_init)
    tq = ATT_A_TQ

    def qblock(i, carry):
        r0 = pl.multiple_of(i * tq, tq)
        q = q_ref[0, pl.ds(r0, tq), :]
        k = k_ref[0]
        v1 = _with_ones(v_ref[0])
        outs = []
        for hh in range(2):
            maps = []
            for m in range(2):
                lo = hh * HEAD_DIM + m * DIFF_DIM
                qm = jnp.where((lane >= lo) & (lane < lo + DIFF_DIM), q, jnp.zeros_like(q))
                s = _nt_dot(qm, k)
                e = jnp.exp2(s - jnp.max(s, axis=-1, keepdims=True)).astype(BF16)
                maps.append(_softmax_times_v(e, v1))
            outs.append(maps[0] - lam * maps[1])
        o = jnp.where(first, outs[0], outs[1])
        sq = o * o
        ss0 = jnp.sum(jnp.where(first, sq, 0.0), axis=-1, keepdims=True)
        ss1 = jnp.sum(jnp.where(first, 0.0, sq), axis=-1, keepdims=True)
        ms = jnp.where(first, ss0, ss1) * (1.0 / HEAD_DIM)
        o_ref[0, pl.ds(r0, tq), :] = (o * lax.rsqrt(ms + LN_EPS) * gain).astype(BF16)
        return carry

    lax.fori_loop(0, seq // tq, qblock, 0, unroll=ATT_A_UNROLL)


def _attn_a(qkv, diff_lambda, norm_g, lam_init):
    B, S, _ = qkv.shape
    n_pairs = WIDTH_A // LANES
    g2 = jnp.tile(norm_g.astype(F32), LANES // HEAD_DIM)[None, :]
    blk = lambda off: pl.BlockSpec((1, S, LANES), lambda b, h, off=off: (b, 0, off + h))
    return pl.pallas_call(
        functools.partial(_attn_a_kernel, lam_init=lam_init, seq=S),
        grid=(B, n_pairs),
        in_specs=[
            pl.BlockSpec((4, DIFF_DIM), lambda b, h: (0, 0)),
            pl.BlockSpec((1, LANES), lambda b, h: (0, 0)),
            blk(0), blk(n_pairs), blk(2 * n_pairs),
        ],
        out_specs=pl.BlockSpec((1, S, LANES), lambda b, h: (b, 0, h)),
        out_shape=jax.ShapeDtypeStruct((B, S, WIDTH_A), BF16),
        compiler_params=_cparams(("arbitrary", "arbitrary")),
        name="attn_diff",
    )(diff_lambda.astype(F32), g2, qkv, qkv, qkv)


BAND_TQ = 128


def _stack_pair(q, lane):
    z = jnp.zeros_like(q)
    return jnp.concatenate([jnp.where(lane < HEAD_DIM, q, z), jnp.where(lane < HEAD_DIM, z, q)], axis=0)


def _attn_b_kernel(q_ref, k_ref, v_ref, o_ref, lse_ref, *, length, dil):
    tq = BAND_TQ
    win = min(tq + 2 * BAND_HALF, length)
    lane = lax.broadcasted_iota(jnp.int32, (1, LANES), 1)

    def residue(r, carry):
        for i in range(length // tq):
            q0 = i * tq
            k0 = min(max(q0 - BAND_HALF, 0), length - win)
            row = lax.broadcasted_iota(jnp.int32, (2 * tq, win), 0)
            qpos = q0 + jnp.where(row >= tq, row - tq, row)
            kpos = k0 + lax.broadcasted_iota(jnp.int32, (2 * tq, win), 1)
            band2 = jnp.abs(kpos - qpos) <= BAND_HALF
            lse_tile = jnp.zeros((tq, LANES), F32)
            for pr in range(WIDTH_B // LANES):
                cs = slice(pr * LANES, (pr + 1) * LANES)
                q2 = _stack_pair(q_ref[0, r, q0:q0 + tq, cs], lane)
                s = _nt_dot(q2, k_ref[0, r, k0:k0 + win, cs])
                s = jnp.where(band2, s, NEG)
                mx = jnp.max(s, axis=-1, keepdims=True)
                e = jnp.exp(s - mx).astype(BF16)
                nd = jnp.dot(e, _with_ones(v_ref[0, r, k0:k0 + win, cs]), preferred_element_type=F32)
                l = nd[:, LANES:]
                o2 = nd[:, :LANES] * (1.0 / l)
                o_ref[0, r, q0:q0 + tq, cs] = jnp.where(lane < HEAD_DIM, o2[:tq], o2[tq:]).astype(BF16)
                lse2 = mx + jnp.log(l)
                lse_tile = jnp.where(lane == 2 * pr, lse2[:tq], lse_tile)
                lse_tile = jnp.where(lane == 2 * pr + 1, lse2[tq:], lse_tile)
            lse_ref[0, r, q0:q0 + tq, :] = lse_tile
        return carry

    lax.fori_loop(0, dil, residue, 0)


def _attn_b(src, dil, first_blk):
    B, _, L, _ = src.shape
    blk = lambda off: pl.BlockSpec((1, dil, L, WIDTH_B), lambda b, off=off: (b, 0, 0, first_blk + off))
    return pl.pallas_call(
        functools.partial(_attn_b_kernel, length=L, dil=dil),
        grid=(B,),
        in_specs=[blk(0), blk(1), blk(2)],
        out_specs=[pl.BlockSpec((1, dil, L, WIDTH_B), lambda b: (b, 0, 0, 0)),
                   pl.BlockSpec((1, dil, L, LANES), lambda b: (b, 0, 0, 0))],
        out_shape=[jax.ShapeDtypeStruct((B, dil, L, WIDTH_B), BF16),
                   jax.ShapeDtypeStruct((B, dil, L, LANES), F32)],
        compiler_params=_cparams(("arbitrary",)),
        name=f"attn_band_d{dil}",
    )(src, src, src)


N_RPB_ROWS = 2 * NA_ROWS - 1
N_RPB_COLS = 2 * NA_COLS - 1
NA_KEYS = NA_ROWS * GRID_W
N_BIAS_VARIANTS = NA_ROWS
NA_ROWS_PER_STEP = 2


def _attn_c_kernel(u_ref, q_ref, k_ref, v_ref, o_ref, bias_ref, *, rows):
    lane = lax.broadcasted_iota(jnp.int32, (1, LANES), 1)
    n_pairs = WIDTH_C // LANES

    @pl.when(pl.program_id(0) == 0)
    def _():
        wide = N_RPB_ROWS * GRID_W
        c = lax.broadcasted_iota(jnp.int32, (GRID_W, wide), 0)
        kc = lax.broadcasted_iota(jnp.int32, (GRID_W, wide), 1) % GRID_W
        tidx = jnp.clip(kc - c + NA_COLS - 1, 0, N_RPB_COLS - 1)
        c5 = lax.broadcasted_iota(jnp.int32, (GRID_W, NA_KEYS), 0)
        kc5 = lax.broadcasted_iota(jnp.int32, (GRID_W, NA_KEYS), 1) % GRID_W
        cstart = jnp.clip(c5 - NA_COLS // 2, 0, GRID_W - NA_COLS)
        valid = (kc5 >= cstart) & (kc5 < cstart + NA_COLS)
        for h in range(N_HEADS_C):
            def tbody(t, acc, h=h):
                row = u_ref[h, pl.ds(t, 1), :]
                return jnp.where(tidx == t, row, acc)
            toep = lax.fori_loop(0, N_RPB_COLS, tbody, jnp.zeros((GRID_W, wide), F32))
            for var in range(N_BIAS_VARIANTS):
                tile = jnp.where(valid, toep[:, var * GRID_W:var * GRID_W + NA_KEYS], NEG)
                bias_ref[var, h // 2, (h % 2) * GRID_W:(h % 2 + 1) * GRID_W, :] = tile

    def rbody(step, carry):
        for sub in range(NA_ROWS_PER_STEP):
            r = step * NA_ROWS_PER_STEP + sub
            rs = jnp.clip(r - NA_ROWS // 2, 0, rows - NA_ROWS)
            var = rs - r + NA_ROWS - 1
            q0 = pl.multiple_of(r * GRID_W, GRID_W)
            k0 = pl.multiple_of(rs * GRID_W, GRID_W)
            for pr in range(n_pairs):
                cs = slice(pr * LANES, (pr + 1) * LANES)
                q2 = _stack_pair(q_ref[0, pl.ds(q0, GRID_W), cs], lane)
                s = _nt_dot(q2, k_ref[0, pl.ds(k0, NA_KEYS), cs]) + bias_ref[var, pr]
                e = jnp.exp(s - jnp.max(s, axis=-1, keepdims=True)).astype(BF16)
                o2 = _softmax_times_v(e, _with_ones(v_ref[0, pl.ds(k0, NA_KEYS), cs]))
                o_ref[0, pl.ds(q0, GRID_W), cs] = jnp.where(lane < HEAD_DIM, o2[:GRID_W], o2[GRID_W:]).astype(BF16)
        return carry

    lax.fori_loop(0, rows // NA_ROWS_PER_STEP, rbody, 0)


def _attn_c(qkv, rpb):
    B, S, _ = qkv.shape
    rows = S // GRID_W
    assert rows >= NA_ROWS
    u = jnp.repeat(jnp.transpose(rpb.astype(F32), (0, 2, 1)), GRID_W, axis=2)
    q_blk = (3 * WIDTH_A + 3 * WIDTH_B) // WIDTH_C
    blk = lambda off: pl.BlockSpec((1, S, WIDTH_C), lambda b, off=off: (b, 0, off))
    return pl.pallas_call(
        functools.partial(_attn_c_kernel, rows=rows),
        grid=(B,),
        in_specs=[pl.BlockSpec((N_HEADS_C, N_RPB_COLS, N_RPB_ROWS * GRID_W), lambda b: (0, 0, 0)),
                  blk(q_blk), blk(q_blk + 1), blk(q_blk + 2)],
        out_specs=pl.BlockSpec((1, S, WIDTH_C), lambda b: (b, 0, 0)),
        out_shape=jax.ShapeDtypeStruct((B, S, WIDTH_C), BF16),
        scratch_shapes=[pltpu.VMEM((N_BIAS_VARIANTS, WIDTH_C // LANES, LANES, NA_KEYS), F32)],
        compiler_params=_cparams(("arbitrary",)),
        name="attn_nbr",
    )(u, qkv, qkv, qkv)


OUT_TM = 512


def _split_bf16(x):
    hi = x.astype(BF16)
    return hi, (x - hi.astype(F32)).astype(BF16)


def _outproj_kernel(x_ref, oa_ref, ob0_ref, ob1_ref, ob2_ref, l0_ref, l1_ref, l2_ref, oc_ref,
                    beta_ref, w_ref, g_ref, b_ref, e_ref, o_ref, wbf_ref, il_ref, *, alpha, tile_out):
    @pl.when(pl.program_id(0) == 0)
    def _():
        _cast_rows_to_bf16(w_ref, wbf_ref, D_MODEL)

    tm = x_ref.shape[0]
    n_pages = WIDTH_B // LANES
    for slot, (dil, ob_r, l_r) in enumerate(zip(DILATIONS[1:], (ob1_ref, ob2_ref), (l1_ref, l2_ref))):
        for r in range(dil):
            rows = pl.ds(r, tm // dil, stride=dil)
            for page in range(n_pages):
                il_ref[slot, page, rows, :] = ob_r[0, r, :, page * LANES:(page + 1) * LANES].astype(F32)
            il_ref[slot, n_pages, rows, :] = l_r[0, r]

    l0, l1, l2 = l0_ref[0, 0], il_ref[0, n_pages], il_ref[1, n_pages]
    lm = jnp.maximum(jnp.maximum(l0, l1), l2)
    e0, e1, e2 = jnp.exp(l0 - lm), jnp.exp(l1 - lm), jnp.exp(l2 - lm)
    inv = 1.0 / (e0 + e1 + e2)
    splits = [_split_bf16(e * inv) for e in (e0, e1, e2)]
    pages = []
    for page in range(n_pages):
        cs = slice(page * LANES, (page + 1) * LANES)
        branch_vals = (ob0_ref[0, 0, :, cs].astype(F32), il_ref[0, page], il_ref[1, page])
        acc_p = jnp.zeros((tm, LANES), F32)
        for (hi, lo), val in zip(splits, branch_vals):
            wexp = (jnp.dot(hi, e_ref[:, cs], preferred_element_type=F32)
                    + jnp.dot(lo, e_ref[:, cs], preferred_element_type=F32))
            acc_p = acc_p + wexp * val
        pages.append(acc_p)
    ob = jnp.concatenate(pages, axis=1)

    beta = beta_ref[...]
    ma = (oa_ref[...].astype(F32) * beta[:, :WIDTH_A]).astype(BF16)
    mb = (ob * beta[:, WIDTH_A:WIDTH_A + WIDTH_B]).astype(BF16)
    mc = (oc_ref[...].astype(F32) * beta[:, WIDTH_A + WIDTH_B:]).astype(BF16)
    acc = jnp.dot(ma, wbf_ref[0:WIDTH_A, :], preferred_element_type=F32)
    acc = acc + jnp.dot(mb, wbf_ref[WIDTH_A:WIDTH_A + WIDTH_B, :], preferred_element_type=F32)
    acc = acc + jnp.dot(mc, wbf_ref[WIDTH_A + WIDTH_B:, :], preferred_element_type=F32)
    y = _layer_norm(alpha * x_ref[...] + acc, g_ref[...], b_ref[...])
    if tile_out:
        _store_token_tiles(o_ref, y)
    else:
        o_ref[...] = y


def _store_token_tiles(ref, y):
    rows = y.shape[0]
    for j in range(D_MODEL // LANES):
        ref[pl.ds(j, rows, stride=SUBLANES), :] = y[:, j * LANES:(j + 1) * LANES]


def _load_token_tiles(ref, rows, row0=0):
    return jnp.concatenate(
        [ref[pl.ds(row0 * SUBLANES + j, rows, stride=SUBLANES), :] for j in range(D_MODEL // LANES)], axis=1)


def _outproj(x2, oa, obs, lses, oc, beta, w, g, b, alpha, tile_out):
    N = x2.shape[0]
    tm = OUT_TM
    steps_per_seq = obs[0].shape[2] // tm
    expand = np.zeros((LANES, WIDTH_B), np.float32)
    for h in range(N_HEADS_B):
        expand[h, h * HEAD_DIM:(h + 1) * HEAD_DIM] = 1.0
    row = lambda w_: pl.BlockSpec((tm, w_), lambda i: (i, 0))
    full = lambda r, c, **kw: pl.BlockSpec((r, c), lambda i: (0, 0), **kw)
    res = lambda d, w_: pl.BlockSpec((1, d, tm // d, w_),
                                     lambda i: (i // steps_per_seq, 0, i % steps_per_seq, 0))
    if tile_out:
        out_spec = pl.BlockSpec((tm * SUBLANES, LANES), lambda i: (i, 0))
        out_shape = jax.ShapeDtypeStruct((N * SUBLANES, LANES), F32)
    else:
        out_spec = row(D_MODEL)
        out_shape = jax.ShapeDtypeStruct((N, D_MODEL), F32)
    return pl.pallas_call(
        functools.partial(_outproj_kernel, alpha=alpha, tile_out=tile_out),
        grid=(N // tm,),
        in_specs=[row(D_MODEL), row(WIDTH_A)] + [res(d, WIDTH_B) for d in DILATIONS]
                 + [res(d, LANES) for d in DILATIONS] + [row(WIDTH_C),
                  full(1, D_MODEL), full(D_MODEL, D_MODEL, pipeline_mode=pl.Buffered(1)),
                  full(1, D_MODEL), full(1, D_MODEL), full(LANES, WIDTH_B)],
        out_specs=out_spec,
        out_shape=out_shape,
        scratch_shapes=[pltpu.VMEM((D_MODEL, D_MODEL), BF16),
                        pltpu.VMEM((len(DILATIONS) - 1, WIDTH_B // LANES + 1, tm, LANES), F32)],
        compiler_params=_cparams(("arbitrary",)),
        name="merge_outproj_ln",
    )(x2, oa, obs[0], obs[1], obs[2], lses[0], lses[1], lses[2], oc,
      beta[None, :].astype(F32), w, g[None, :].astype(F32), b[None, :].astype(F32),
      jnp.asarray(expand, BF16))


FFN_TM = 1024


def _ffn_kernel(te_ref, nt_ref, x_ref, wg_ref, wu_ref, wd_ref, *rest, n_chunks, grouped, alpha):
    if grouped:
        o_ref, xb_ref, acc_ref = rest
    else:
        g_ref, b_ref, o_ref, xb_ref, acc_ref = rest
    i, j = pl.program_id(0), pl.program_id(1)
    tm = xb_ref.shape[0]

    @pl.when(i < nt_ref[0])
    def _():
        @pl.when(j == 0)
        def _():
            if grouped:
                xb_ref[...] = _load_token_tiles(x_ref, tm).astype(BF16)
            else:
                xb_ref[...] = x_ref[...].astype(BF16)
            acc_ref[...] = jnp.zeros_like(acc_ref)

        xb = xb_ref[...]
        gate = jnp.dot(xb, wg_ref[...].astype(BF16), preferred_element_type=F32)
        up = jnp.dot(xb, wu_ref[...].astype(BF16), preferred_element_type=F32)
        h = (gate * jax.nn.sigmoid(gate) * up).astype(BF16)
        acc_ref[...] += jnp.dot(h, wd_ref[...].astype(BF16), preferred_element_type=F32)

        @pl.when(j == n_chunks - 1)
        def _():
            if grouped:
                _store_token_tiles(o_ref, acc_ref[...])
            else:
                o_ref[...] = _layer_norm(alpha * x_ref[...] + acc_ref[...], g_ref[...], b_ref[...])

    @pl.when((i >= nt_ref[0]) & (j == n_chunks - 1))
    def _():
        o_ref[...] = jnp.zeros_like(o_ref)


def _ffn(x, w_gate, w_up, w_down, tile_expert, n_tiles, tf, *, grouped, ln=None, alpha=None):
    tm = FFN_TM
    F = w_gate.shape[-1]
    n_chunks = F // tf
    max_tiles = tile_expert.shape[0]

    def tile_ix(i, nt):
        return jnp.minimum(i, nt[0] - 1)

    def chunk_ix(i, j, nt):
        return jnp.where(i < nt[0], j, n_chunks - 1)

    blk = (tm * SUBLANES, LANES) if grouped else (tm, D_MODEL)
    x_spec = pl.BlockSpec(blk, lambda i, j, te, nt: (tile_ix(i, nt), 0))
    out_spec = pl.BlockSpec(blk, lambda i, j, te, nt: (i, 0))
    out_shape = jax.ShapeDtypeStruct(x.shape, F32)
    in_specs = [
        x_spec,
        pl.BlockSpec((None, D_MODEL, tf), lambda i, j, te, nt: (te[tile_ix(i, nt)], 0, chunk_ix(i, j, nt))),
        pl.BlockSpec((None, D_MODEL, tf), lambda i, j, te, nt: (te[tile_ix(i, nt)], 0, chunk_ix(i, j, nt))),
        pl.BlockSpec((None, tf, D_MODEL), lambda i, j, te, nt: (te[tile_ix(i, nt)], chunk_ix(i, j, nt), 0)),
    ]
    args = [x, w_gate, w_up, w_down]
    if not grouped:
        in_specs += [pl.BlockSpec((1, D_MODEL), lambda i, j, te, nt: (0, 0))] * 2
        args += [ln[0][None, :].astype(F32), ln[1][None, :].astype(F32)]
    return pl.pallas_call(
        functools.partial(_ffn_kernel, n_chunks=n_chunks, grouped=grouped, alpha=alpha),
        grid_spec=pltpu.PrefetchScalarGridSpec(
            num_scalar_prefetch=2,
            grid=(max_tiles, n_chunks),
            in_specs=in_specs,
            out_specs=out_spec,
            scratch_shapes=[pltpu.VMEM((tm, D_MODEL), BF16), pltpu.VMEM((tm, D_MODEL), F32)],
        ),
        out_shape=out_shape,
        compiler_params=_cparams(("arbitrary", "arbitrary")),
        name="swiglu_grouped" if grouped else "swiglu_dense",
    )(tile_expert, n_tiles, *args)


ROUTE_TB = 512
META_E1, META_E2, META_R1, META_R2, META_G1, META_G2 = range(6)


def _router_kernel(x_ref, wr_ref, meta_ref, cnt_ref, carry_ref):
    tb = ROUTE_TB

    @pl.when(pl.program_id(0) == 0)
    def _():
        carry_ref[...] = jnp.zeros_like(carry_ref)

    x = _load_token_tiles(x_ref, tb)
    xh, xl = _split_bf16(x)
    wh, wl = _split_bf16(wr_ref[...])
    logits = (jnp.dot(xh, wh, preferred_element_type=F32) + jnp.dot(xh, wl, preferred_element_type=F32)
              + jnp.dot(xl, wh, preferred_element_type=F32))
    lane = lax.broadcasted_iota(jnp.int32, (tb, LANES), 1).astype(F32)
    logits = jnp.where(lane < N_EXPERTS, logits, -jnp.inf)
    m1 = jnp.max(logits, axis=-1, keepdims=True)
    i1 = jnp.min(jnp.where(logits == m1, lane, float(LANES)), axis=-1, keepdims=True)
    rest = jnp.where(lane == i1, -jnp.inf, logits)
    m2 = jnp.max(rest, axis=-1, keepdims=True)
    i2 = jnp.min(jnp.where(rest == m2, lane, float(LANES)), axis=-1, keepdims=True)
    e21 = jnp.exp(m2 - m1)
    g1 = 1.0 / (1.0 + e21)
    g2 = e21 * g1

    member = ((lane == i1) | (lane == i2)).astype(F32)
    r_i = lax.broadcasted_iota(jnp.int32, (tb, tb), 0)
    c_i = lax.broadcasted_iota(jnp.int32, (tb, tb), 1)
    lower = (c_i < r_i).astype(BF16)
    before = jnp.dot(lower, member.astype(BF16), preferred_element_type=F32)
    rank = carry_ref[...] + before
    r1 = jnp.sum(jnp.where(lane == i1, rank, 0.0), axis=-1, keepdims=True)
    r2 = jnp.sum(jnp.where(lane == i2, rank, 0.0), axis=-1, keepdims=True)
    carry_ref[...] += jnp.sum(member, axis=0, keepdims=True)
    cnt_ref[...] = carry_ref[...]

    meta = jnp.zeros((tb, LANES), F32)
    for slot, val in ((META_E1, i1.astype(F32)), (META_E2, i2.astype(F32)), (META_R1, r1), (META_R2, r2),
                      (META_G1, g1), (META_G2, g2)):
        meta = jnp.where(lane == slot, val, meta)
    meta_ref[...] = meta


def _router(xt, w_router):
    n_tok = xt.shape[0] // SUBLANES
    tb = ROUTE_TB
    wr = jnp.zeros((D_MODEL, LANES), F32).at[:, :N_EXPERTS].set(w_router.astype(F32))
    return pl.pallas_call(
        _router_kernel,
        grid=(n_tok // tb,),
        in_specs=[pl.BlockSpec((tb * SUBLANES, LANES), lambda i: (i, 0)),
                  pl.BlockSpec((D_MODEL, LANES), lambda i: (0, 0))],
        out_specs=[pl.BlockSpec((tb, LANES), lambda i: (i, 0)),
                   pl.BlockSpec((1, LANES), lambda i: (0, 0))],
        out_shape=[jax.ShapeDtypeStruct((n_tok, LANES), F32), jax.ShapeDtypeStruct((1, LANES), F32)],
        scratch_shapes=[pltpu.VMEM((1, LANES), F32)],
        compiler_params=_cparams(("arbitrary",)),
        name="router_top2",
    )(xt, wr)


MOVE_TB = 1024


def _tile_rows(ref, idx):
    return ref.at[pl.ds(pl.multiple_of(idx * SUBLANES, SUBLANES), SUBLANES), :]


def _dispatch_kernel(dest_ref, x_ref, xs_in, xs_hbm, sem):
    del xs_in
    tb = MOVE_TB

    def body(t, c):
        src = _tile_rows(x_ref, t)
        for k in range(TOP_K):
            pltpu.make_async_copy(src, _tile_rows(xs_hbm, dest_ref[0, 0, TOP_K * t + k]), sem).start()
        return c

    lax.fori_loop(0, tb, body, 0)
    n_rows = TOP_K * tb * SUBLANES
    pltpu.make_async_copy(xs_hbm.at[pl.ds(0, n_rows), :], xs_hbm.at[pl.ds(0, n_rows), :], sem).wait()


def _dispatch(xt, dest, n_slots):
    n_tok = xt.shape[0] // SUBLANES
    tb = MOVE_TB
    dest3 = dest.reshape(n_tok // tb, 1, TOP_K * tb)
    zeros = jnp.zeros((n_slots * SUBLANES, LANES), F32)
    return pl.pallas_call(
        _dispatch_kernel,
        grid=(n_tok // tb,),
        in_specs=[pl.BlockSpec((1, 1, TOP_K * tb), lambda i: (i, 0, 0), memory_space=pltpu.SMEM),
                  pl.BlockSpec((tb * SUBLANES, LANES), lambda i: (i, 0)),
                  pl.BlockSpec(memory_space=pl.ANY)],
        out_specs=pl.BlockSpec(memory_space=pl.ANY),
        out_shape=jax.ShapeDtypeStruct(zeros.shape, F32),
        scratch_shapes=[pltpu.SemaphoreType.DMA(())],
        input_output_aliases={2: 0},
        compiler_params=_cparams(("arbitrary",)),
        name="moe_dispatch",
    )(dest3, xt, zeros)


COMB_TB = 256


def _combine_kernel(dest_ref, y_hbm, x_ref, meta_ref, g_ref, b_ref, o_ref, ybuf, sem, *, alpha):
    tb = COMB_TB

    def body(t, c):
        for k in range(TOP_K):
            pltpu.make_async_copy(_tile_rows(y_hbm, dest_ref[0, 0, TOP_K * t + k]),
                                  _tile_rows(ybuf, k * tb + t), sem).start()
        return c

    lax.fori_loop(0, tb, body, 0)
    pltpu.make_async_copy(ybuf, ybuf, sem).wait()

    meta = meta_ref[...]
    lane = lax.broadcasted_iota(jnp.int32, meta.shape, 1)
    g1 = jnp.sum(jnp.where(lane == META_G1, meta, 0.0), axis=-1, keepdims=True)
    g2 = jnp.sum(jnp.where(lane == META_G2, meta, 0.0), axis=-1, keepdims=True)
    x = _load_token_tiles(x_ref, tb)
    f = g1 * _load_token_tiles(ybuf, tb) + g2 * _load_token_tiles(ybuf, tb, row0=tb)
    o_ref[...] = _layer_norm(alpha * x + f, g_ref[...], b_ref[...])


def _combine(yt, xt, dest, meta, g, b, alpha):
    n_tok = xt.shape[0] // SUBLANES
    tb = COMB_TB
    dest3 = dest.reshape(n_tok // tb, 1, TOP_K * tb)
    return pl.pallas_call(
        functools.partial(_combine_kernel, alpha=alpha),
        grid=(n_tok // tb,),
        in_specs=[pl.BlockSpec((1, 1, TOP_K * tb), lambda i: (i, 0, 0), memory_space=pltpu.SMEM),
                  pl.BlockSpec(memory_space=pl.ANY),
                  pl.BlockSpec((tb * SUBLANES, LANES), lambda i: (i, 0)),
                  pl.BlockSpec((tb, LANES), lambda i: (i, 0)),
                  pl.BlockSpec((1, D_MODEL), lambda i: (0, 0)),
                  pl.BlockSpec((1, D_MODEL), lambda i: (0, 0))],
        out_specs=pl.BlockSpec((tb, D_MODEL), lambda i: (i, 0)),
        out_shape=jax.ShapeDtypeStruct((n_tok, D_MODEL), F32),
        scratch_shapes=[pltpu.VMEM((TOP_K * tb * SUBLANES, LANES), F32), pltpu.SemaphoreType.DMA(())],
        compiler_params=_cparams(("arbitrary",)),
        name="moe_combine_ln",
    )(dest3, yt, xt, meta, g[None, :].astype(F32), b[None, :].astype(F32))


def _moe_plan(meta, counts, n_tok):
    tm = FFN_TM
    max_tiles = n_tok * TOP_K // tm + N_EXPERTS
    cnt = counts[0, :N_EXPERTS].astype(jnp.int32)
    tiles = (cnt + tm - 1) // tm
    tile_end = jnp.cumsum(tiles)
    start = (tile_end - tiles) * tm
    e = meta[:, META_E1:META_E2 + 1].astype(jnp.int32)
    r = meta[:, META_R1:META_R2 + 1].astype(jnp.int32)
    dest = (start[e] + r).reshape(-1)
    tile_ids = jnp.arange(max_tiles, dtype=jnp.int32)
    tile_expert = jnp.minimum(jnp.sum((tile_ids[:, None] >= tile_end[None, :]).astype(jnp.int32), axis=1),
                              N_EXPERTS - 1)
    return dest, tile_expert, tile_end[-1:].astype(jnp.int32), max_tiles * tm


def kernel(x, w_in, w_out, beta_mix, diff_lambda, diff_norm_g, na_rpb, ln1_g, ln1_b, ln2_g, ln2_b,
           ffn_w_gate, ffn_w_up, ffn_w_down, moe_w_router, moe_w_gate, moe_w_up, moe_w_down):
    B, S, D = x.shape
    assert D == D_MODEL and S % (max(DILATIONS) * BAND_TQ) == 0 and S % PROJ_TM == 0
    depth = w_in.shape[0]
    alpha = (2 * depth) ** 0.25
    n_tok = B * S
    tables = _rope_tables(S)
    x = x.astype(F32)
    for l in range(depth):
        lam_init = 0.8 - 0.6 * math.exp(-0.3 * l)
        moe = l % 2 == 1
        qkv, *res_major = _project(x, w_in[l], tables)
        oa = _attn_a(qkv, diff_lambda[l], diff_norm_g[l], lam_init).reshape(n_tok, WIDTH_A)
        band = [_attn_b(qkv[:, None], DILATIONS[0], BAND_LO // WIDTH_B)]
        band += [_attn_b(src, d, 0) for d, src in zip(DILATIONS[1:], res_major)]
        oc = _attn_c(qkv, na_rpb[l]).reshape(n_tok, WIDTH_C)
        x1 = _outproj(x.reshape(n_tok, D), oa, [o for o, _ in band], [s for _, s in band], oc,
                      beta_mix[l], w_out[l], ln1_g[l], ln1_b[l], alpha, tile_out=moe)
        j = l // 2
        if not moe:
            n_tiles = n_tok // FFN_TM
            x = _ffn(x1, ffn_w_gate[j][None], ffn_w_up[j][None], ffn_w_down[j][None],
                     jnp.zeros((n_tiles,), jnp.int32), jnp.full((1,), n_tiles, jnp.int32), 256,
                     grouped=False, ln=(ln2_g[l], ln2_b[l]), alpha=alpha)
        else:
            meta, counts = _router(x1, moe_w_router[j])
            dest, tile_expert, n_tiles, n_slots = _moe_plan(meta, counts, n_tok)
            xs = _dispatch(x1, dest, n_slots)
            ys = _ffn(xs, moe_w_gate[j], moe_w_up[j], moe_w_down[j], tile_expert, n_tiles, 512, grouped=True)
            x = _combine(ys, x1, dest, meta, ln2_g[l], ln2_b[l], alpha)
        x = x.reshape(B, S, D)
    return x
```

```python
import functools
import math

import jax
import jax.numpy as jnp
import numpy as np
from jax import lax
from jax.experimental import pallas as pl
from jax.experimental.pallas import tpu as pltpu

F32 = jnp.float32
BF16 = jnp.bfloat16

D_MODEL = 1024
HEAD_DIM = 64
N_HEADS_A, N_HEADS_B, N_HEADS_C = 4, 6, 6
WIDTH_A, WIDTH_B, WIDTH_C = N_HEADS_A * HEAD_DIM, N_HEADS_B * HEAD_DIM, N_HEADS_C * HEAD_DIM
DIFF_DIM = HEAD_DIM // 2
ROPE_THETA = 500000.0
ROPE_FRACTION = 4
DILATIONS = (1, 4, 16)
BAND_HALF = 64
GRID_W = 64
NA_ROWS = 8
NA_COLS = 16
N_EXPERTS = 8
TOP_K = 2
LN_EPS = 1e-5
NEG = -1e30
LOG2E = math.log2(math.e)

LANES = 128
SUBLANES = 8
VMEM_LIMIT = 56 * 1024 * 1024

PROJ_W = 3 * (WIDTH_A + WIDTH_B + WIDTH_C)
ROPE_COLS = 3 * WIDTH_A + 2 * WIDTH_B
COL_CHUNK = 3 * WIDTH_A
BAND_LO = 3 * WIDTH_A
BAND_COLS = 3 * WIDTH_B


def _cparams(sem):
    return pltpu.CompilerParams(dimension_semantics=sem, vmem_limit_bytes=VMEM_LIMIT)


def _layer_norm(z, g, b):
    mu = jnp.mean(z, axis=-1, keepdims=True)
    zc = z - mu
    var = jnp.mean(zc * zc, axis=-1, keepdims=True)
    return zc * lax.rsqrt(var + LN_EPS) * g + b


def _rope_tables(seq):
    rot_a = DIFF_DIM // ROPE_FRACTION
    rot_b = HEAD_DIM // ROPE_FRACTION
    pos = jnp.arange(seq, dtype=F32)[:, None]
    inv_a = ROPE_THETA ** (-jnp.arange(0, rot_a, 2, dtype=F32) / rot_a)
    inv_b = ROPE_THETA ** (-jnp.arange(0, rot_b, 2, dtype=F32) / rot_b)
    ang_a, ang_b = pos * inv_a[None, :], pos * inv_b[None, :]

    cols = np.arange(ROPE_COLS)
    in_a = cols < 3 * WIDTH_A
    sub_a = cols // WIDTH_A
    within_a = cols % DIFF_DIM
    rot_mask_a = in_a & (sub_a < 2) & (within_a < rot_a)
    cb = cols - 3 * WIDTH_A
    sub_b = cb // WIDTH_B
    within_b = cb % HEAD_DIM
    rot_mask_b = (~in_a) & (within_b < rot_b)

    first_half = np.where(in_a, within_a < rot_a // 2, within_b < rot_b // 2)
    rot = rot_mask_a | rot_mask_b
    scale = np.ones(ROPE_COLS, np.float32)
    scale[in_a & (sub_a == 0)] = DIFF_DIM ** -0.5 * LOG2E
    scale[(~in_a) & (sub_b == 0)] = HEAD_DIM ** -0.5

    def spread(tab_a, tab_b, fill):
        def groups(tab, width, reps):
            pad = jnp.full((seq, width - 2 * tab.shape[1]), fill, F32)
            return jnp.tile(jnp.concatenate([tab, tab, pad], axis=1), (1, reps))
        return jnp.concatenate([groups(tab_a, DIFF_DIM, 2 * WIDTH_A // DIFF_DIM),
                                jnp.full((seq, WIDTH_A), fill, F32),
                                groups(tab_b, HEAD_DIM, 2 * WIDTH_B // HEAD_DIM)], axis=1)

    cm = spread(jnp.cos(ang_a), jnp.cos(ang_b), 1.0) * jnp.asarray(scale)[None, :]
    sn = spread(jnp.sin(ang_a), jnp.sin(ang_b), 0.0) * jnp.asarray(scale)[None, :]
    ma = np.where(rot & first_half, -1.0, 0.0).astype(np.float32)[None, :]
    mb = np.where(rot & ~first_half, 1.0, 0.0).astype(np.float32)[None, :]
    return cm, sn, jnp.asarray(ma), jnp.asarray(mb)


PROJ_TM = 512
W_CAST_ROWS = 128


def _cast_rows_to_bf16(src_ref, dst_ref, rows):
    def body(i, c):
        r = pl.multiple_of(i * W_CAST_ROWS, W_CAST_ROWS)
        dst_ref[pl.ds(r, W_CAST_ROWS), :] = src_ref[pl.ds(r, W_CAST_ROWS), :].astype(BF16)
        return c
    lax.fori_loop(0, rows // W_CAST_ROWS, body, 0)


def _proj_kernel(x_ref, w_ref, cm_ref, sn_ref, ma_ref, mb_ref, o_ref, *rest):
    res_refs, (wbf_ref, band_ref) = rest[:-2], rest[-2:]

    @pl.when((pl.program_id(0) == 0) & (pl.program_id(1) == 0))
    def _():
        _cast_rows_to_bf16(w_ref, wbf_ref, D_MODEL)

    xb = x_ref[0].astype(BF16)
    tm = xb.shape[0]
    n_chunks = PROJ_W // COL_CHUNK
    for c in range(n_chunks):
        lo, hi = c * COL_CHUNK, (c + 1) * COL_CHUNK
        acc = jnp.dot(xb, wbf_ref[:, lo:hi], preferred_element_type=F32)
        if hi <= ROPE_COLS:
            half = (DIFF_DIM if c == 0 else HEAD_DIM) // ROPE_FRACTION // 2
            up = pltpu.roll(acc, COL_CHUNK - half, 1)
            dn = pltpu.roll(acc, half, 1)
            acc = acc * cm_ref[:, lo:hi] + (up * ma_ref[:, lo:hi] + dn * mb_ref[:, lo:hi]) * sn_ref[:, lo:hi]
        elif c == 2:
            lane = lax.broadcasted_iota(jnp.int32, (1, COL_CHUNK), 1)
            acc = acc * jnp.where(lane >= WIDTH_B, HEAD_DIM ** -0.5, 1.0)
        o_ref[0, :, lo:hi] = acc.astype(BF16)
        for col in range(max(lo, BAND_LO), min(hi, BAND_LO + BAND_COLS), LANES):
            band_ref[(col - BAND_LO) // LANES] = acc[:, col - lo:col - lo + LANES]

    for dil, r_ref in zip(DILATIONS[1:], res_refs):
        for r in range(dil):
            for page in range(BAND_COLS // LANES):
                r_ref[0, r, :, page * LANES:(page + 1) * LANES] = (
                    band_ref[page, pl.ds(r, tm // dil, stride=dil), :].astype(BF16))


def _project(x, w, tables):
    B, S, _ = x.shape
    cm, sn, ma, mb = tables
    tm = PROJ_TM
    res_specs = [pl.BlockSpec((1, d, tm // d, BAND_COLS), lambda p, b: (b, 0, p, 0)) for d in DILATIONS[1:]]
    res_shapes = [jax.ShapeDtypeStruct((B, d, S // d, BAND_COLS), BF16) for d in DILATIONS[1:]]
    return pl.pallas_call(
        _proj_kernel,
        grid=(S // tm, B),
        in_specs=[
            pl.BlockSpec((1, tm, D_MODEL), lambda p, b: (b, p, 0)),
            pl.BlockSpec((D_MODEL, PROJ_W), lambda p, b: (0, 0), pipeline_mode=pl.Buffered(1)),
            pl.BlockSpec((tm, ROPE_COLS), lambda p, b: (p, 0)),
            pl.BlockSpec((tm, ROPE_COLS), lambda p, b: (p, 0)),
            pl.BlockSpec((1, ROPE_COLS), lambda p, b: (0, 0)),
            pl.BlockSpec((1, ROPE_COLS), lambda p, b: (0, 0)),
        ],
        out_specs=[pl.BlockSpec((1, tm, PROJ_W), lambda p, b: (b, p, 0))] + res_specs,
        out_shape=[jax.ShapeDtypeStruct((B, S, PROJ_W), BF16)] + res_shapes,
        scratch_shapes=[pltpu.VMEM((D_MODEL, PROJ_W), BF16),
                        pltpu.VMEM((BAND_COLS // LANES, tm, LANES), F32)],
        compiler_params=_cparams(("arbitrary", "arbitrary")),
        name="proj_in",
    )(x, w, cm, sn, ma, mb)


ATT_A_TQ = 256
ATT_A_UNROLL = 4


def _nt_dot(a, b):
    return lax.dot_general(a, b, (((1,), (1,)), ((), ())), preferred_element_type=F32)


def _with_ones(v):
    return jnp.concatenate([v, jnp.ones_like(v)], axis=1)


def _softmax_times_v(e, v1):
    nd = jnp.dot(e, v1, preferred_element_type=F32)
    return nd[:, :LANES] * (1.0 / nd[:, LANES:])


def _attn_a_kernel(lam_ref, g_ref, q_ref, k_ref, v_ref, o_ref, *, lam_init, seq):
    lp = lam_ref[...]
    lam = (jnp.exp(jnp.sum(lp[0:1] * lp[1:2], axis=-1, keepdims=True))
           - jnp.exp(jnp.sum(lp[2:3] * lp[3:4], axis=-1, keepdims=True)) + lam_init)
    lane = lax.broadcasted_iota(jnp.int32, (1, LANES), 1)
    first = lane < HEAD_DIM
    gain = g_ref[...] * (1.0 - lam_init)
    tq = ATT_A_TQ

    def qblock(i, carry):
        r0 = pl.multiple_of(i * tq, tq)
        q = q_ref[0, pl.ds(r0, tq), :]
        k = k_ref[0]
        v1 = _with_ones(v_ref[0])
        outs = []
        for hh in range(2):
            maps = []
            for m in range(2):
                lo = hh * HEAD_DIM + m * DIFF_DIM
                qm = jnp.where((lane >= lo) & (lane < lo + DIFF_DIM), q, jnp.zeros_like(q))
                s = _nt_dot(qm, k)
                e = jnp.exp2(s - jnp.max(s, axis=-1, keepdims=True)).astype(BF16)
                maps.append(_softmax_times_v(e, v1))
            outs.append(maps[0] - lam * maps[1])
        o = jnp.where(first, outs[0], outs[1])
        sq = o * o
        ss0 = jnp.sum(jnp.where(first, sq, 0.0), axis=-1, keepdims=True)
        ss1 = jnp.sum(jnp.where(first, 0.0, sq), axis=-1, keepdims=True)
        ms = jnp.where(first, ss0, ss1) * (1.0 / HEAD_DIM)
        o_ref[0, pl.ds(r0, tq), :] = (o * lax.rsqrt(ms + LN_EPS) * gain).astype(BF16)
        return carry

    lax.fori_loop(0, seq // tq, qblock, 0, unroll=ATT_A_UNROLL)


def _attn_a(qkv, diff_lambda, norm_g, lam_init):
    B, S, _ = qkv.shape
    n_pairs = WIDTH_A // LANES
    g2 = jnp.tile(norm_g.astype(F32), LANES // HEAD_DIM)[None, :]
    blk = lambda off: pl.BlockSpec((1, S, LANES), lambda b, h, off=off: (b, 0, off + h))
    return pl.pallas_call(
        functools.partial(_attn_a_kernel, lam_init=lam_init, seq=S),
        grid=(B, n_pairs),
        in_specs=[
            pl.BlockSpec((4, DIFF_DIM), lambda b, h: (0, 0)),
            pl.BlockSpec((1, LANES), lambda b, h: (0, 0)),
            blk(0), blk(n_pairs), blk(2 * n_pairs),
        ],
        out_specs=pl.BlockSpec((1, S, LANES), lambda b, h: (b, 0, h)),
        out_shape=jax.ShapeDtypeStruct((B, S, WIDTH_A), BF16),
        compiler_params=_cparams(("arbitrary", "arbitrary")),
        name="attn_diff",
    )(diff_lambda.astype(F32), g2, qkv, qkv, qkv)


BAND_TQ = 128


def _stack_pair(q, lane):
    z = jnp.zeros_like(q)
    return jnp.concatenate([jnp.where(lane < HEAD_DIM, q, z), jnp.where(lane < HEAD_DIM, z, q)], axis=0)


def _attn_b_kernel(q_ref, k_ref, v_ref, o_ref, lse_ref, *, length, dil):
    tq = BAND_TQ
    win = min(tq + 2 * BAND_HALF, length)
    lane = lax.broadcasted_iota(jnp.int32, (1, LANES), 1)

    def residue(r, carry):
        for i in range(length // tq):
            q0 = i * tq
            k0 = min(max(q0 - BAND_HALF, 0), length - win)
            row = lax.broadcasted_iota(jnp.int32, (2 * tq, win), 0)
            qpos = q0 + jnp.where(row >= tq, row - tq, row)
            kpos = k0 + lax.broadcasted_iota(jnp.int32, (2 * tq, win), 1)
            band2 = jnp.abs(kpos - qpos) <= BAND_HALF
            lse_tile = jnp.zeros((tq, LANES), F32)
            for pr in range(WIDTH_B // LANES):
                cs = slice(pr * LANES, (pr + 1) * LANES)
                q2 = _stack_pair(q_ref[0, r, q0:q0 + tq, cs], lane)
                s = _nt_dot(q2, k_ref[0, r, k0:k0 + win, cs])
                s = jnp.where(band2, s, NEG)
                mx = jnp.max(s, axis=-1, keepdims=True)
                e = jnp.exp(s - mx).astype(BF16)
                nd = jnp.dot(e, _with_ones(v_ref[0, r, k0:k0 + win, cs]), preferred_element_type=F32)
                l = nd[:, LANES:]
                o2 = nd[:, :LANES] * (1.0 / l)
                o_ref[0, r, q0:q0 + tq, cs] = jnp.where(lane < HEAD_DIM, o2[:tq], o2[tq:]).astype(BF16)
                lse2 = mx + jnp.log(l)
                lse_tile = jnp.where(lane == 2 * pr, lse2[:tq], lse_tile)
                lse_tile = jnp.where(lane == 2 * pr + 1, lse2[tq:], lse_tile)
            lse_ref[0, r, q0:q0 + tq, :] = lse_tile
        return carry

    lax.fori_loop(0, dil, residue, 0)


def _attn_b(src, dil, first_blk):
    B, _, L, _ = src.shape
    blk = lambda off: pl.BlockSpec((1, dil, L, WIDTH_B), lambda b, off=off: (b, 0, 0, first_blk + off))
    return pl.pallas_call(
        functools.partial(_attn_b_kernel, length=L, dil=dil),
        grid=(B,),
        in_specs=[blk(0), blk(1), blk(2)],
        out_specs=[pl.BlockSpec((1, dil, L, WIDTH_B), lambda b: (b, 0, 0, 0)),
                   pl.BlockSpec((1, dil, L, LANES), lambda b: (b, 0, 0, 0))],
        out_shape=[jax.ShapeDtypeStruct((B, dil, L, WIDTH_B), BF16),
                   jax.ShapeDtypeStruct((B, dil, L, LANES), F32)],
        compiler_params=_cparams(("arbitrary",)),
        name=f"attn_band_d{dil}",
    )(src, src, src)


N_RPB_ROWS = 2 * NA_ROWS - 1
N_RPB_COLS = 2 * NA_COLS - 1
NA_KEYS = NA_ROWS * GRID_W
N_BIAS_VARIANTS = NA_ROWS
NA_ROWS_PER_STEP = 2


def _attn_c_kernel(u_ref, q_ref, k_ref, v_ref, o_ref, bias_ref, *, rows):
    lane = lax.broadcasted_iota(jnp.int32, (1, LANES), 1)
    n_pairs = WIDTH_C // LANES

    @pl.when(pl.program_id(0) == 0)
    def _():
        wide = N_RPB_ROWS * GRID_W
        c = lax.broadcasted_iota(jnp.int32, (GRID_W, wide), 0)
        kc = lax.broadcasted_iota(jnp.int32, (GRID_W, wide), 1) % GRID_W
        tidx = jnp.clip(kc - c + NA_COLS - 1, 0, N_RPB_COLS - 1)
        c5 = lax.broadcasted_iota(jnp.int32, (GRID_W, NA_KEYS), 0)
        kc5 = lax.broadcasted_iota(jnp.int32, (GRID_W, NA_KEYS), 1) % GRID_W
        cstart = jnp.clip(c5 - NA_COLS // 2, 0, GRID_W - NA_COLS)
        valid = (kc5 >= cstart) & (kc5 < cstart + NA_COLS)
        for h in range(N_HEADS_C):
            def tbody(t, acc, h=h):
                row = u_ref[h, pl.ds(t, 1), :]
                return jnp.where(tidx == t, row, acc)
            toep = lax.fori_loop(0, N_RPB_COLS, tbody, jnp.zeros((GRID_W, wide), F32))
            for var in range(N_BIAS_VARIANTS):
                tile = jnp.where(valid, toep[:, var * GRID_W:var * GRID_W + NA_KEYS], NEG)
                bias_ref[var, h // 2, (h % 2) * GRID_W:(h % 2 + 1) * GRID_W, :] = tile

    def rbody(step, carry):
        for sub in range(NA_ROWS_PER_STEP):
            r = step * NA_ROWS_PER_STEP + sub
            rs = jnp.clip(r - NA_ROWS // 2, 0, rows - NA_ROWS)
            var = rs - r + NA_ROWS - 1
            q0 = pl.multiple_of(r * GRID_W, GRID_W)
            k0 = pl.multiple_of(rs * GRID_W, GRID_W)
            for pr in range(n_pairs):
                cs = slice(pr * LANES, (pr + 1) * LANES)
                q2 = _stack_pair(q_ref[0, pl.ds(q0, GRID_W), cs], lane)
                s = _nt_dot(q2, k_ref[0, pl.ds(k0, NA_KEYS), cs]) + bias_ref[var, pr]
                e = jnp.exp(s - jnp.max(s, axis=-1, keepdims=True)).astype(BF16)
                o2 = _softmax_times_v(e, _with_ones(v_ref[0, pl.ds(k0, NA_KEYS), cs]))
                o_ref[0, pl.ds(q0, GRID_W), cs] = jnp.where(lane < HEAD_DIM, o2[:GRID_W], o2[GRID_W:]).astype(BF16)
        return carry

    lax.fori_loop(0, rows // NA_ROWS_PER_STEP, rbody, 0)


def _attn_c(qkv, rpb):
    B, S, _ = qkv.shape
    rows = S // GRID_W
    assert rows >= NA_ROWS
    u = jnp.repeat(jnp.transpose(rpb.astype(F32), (0, 2, 1)), GRID_W, axis=2)
    q_blk = (3 * WIDTH_A + 3 * WIDTH_B) // WIDTH_C
    blk = lambda off: pl.BlockSpec((1, S, WIDTH_C), lambda b, off=off: (b, 0, off))
    return pl.pallas_call(
        functools.partial(_attn_c_kernel, rows=rows),
        grid=(B,),
        in_specs=[pl.BlockSpec((N_HEADS_C, N_RPB_COLS, N_RPB_ROWS * GRID_W), lambda b: (0, 0, 0)),
                  blk(q_blk), blk(q_blk + 1), blk(q_blk + 2)],
        out_specs=pl.BlockSpec((1, S, WIDTH_C), lambda b: (b, 0, 0)),
        out_shape=jax.ShapeDtypeStruct((B, S, WIDTH_C), BF16),
        scratch_shapes=[pltpu.VMEM((N_BIAS_VARIANTS, WIDTH_C // LANES, LANES, NA_KEYS), F32)],
        compiler_params=_cparams(("arbitrary",)),
        name="attn_nbr",
    )(u, qkv, qkv, qkv)


OUT_TM = 512
OUT_SPLIT = 2


def _split_bf16(x):
    hi = x.astype(BF16)
    return hi, (x - hi.astype(F32)).astype(BF16)


def _outproj_kernel(x_ref, oa_ref, ob0_ref, ob1_ref, ob2_ref, l0_ref, l1_ref, l2_ref, oc_ref,
                    beta_ref, w_ref, g_ref, b_ref, e_ref, o_ref, wbf_ref, il_ref, *, alpha, tile_out):
    @pl.when(pl.program_id(0) == 0)
    def _():
        _cast_rows_to_bf16(w_ref, wbf_ref, D_MODEL)

    tm = x_ref.shape[0]
    n_pages = WIDTH_B // LANES
    for slot, (dil, ob_r, l_r) in enumerate(zip(DILATIONS[1:], (ob1_ref, ob2_ref), (l1_ref, l2_ref))):
        for r in range(dil):
            rows = pl.ds(r, tm // dil, stride=dil)
            for page in range(n_pages):
                il_ref[slot, page, rows, :] = ob_r[0, r, :, page * LANES:(page + 1) * LANES].astype(F32)
            il_ref[slot, n_pages, rows, :] = l_r[0, r]

    beta = beta_ref[...]
    half = tm // OUT_SPLIT
    for part in range(OUT_SPLIT):
        rs = slice(part * half, (part + 1) * half)
        l0, l1, l2 = l0_ref[0, 0, rs, :], il_ref[0, n_pages, rs, :], il_ref[1, n_pages, rs, :]
        lm = jnp.maximum(jnp.maximum(l0, l1), l2)
        e0, e1, e2 = jnp.exp(l0 - lm), jnp.exp(l1 - lm), jnp.exp(l2 - lm)
        inv = 1.0 / (e0 + e1 + e2)
        wexp = [jnp.dot(jnp.concatenate(_split_bf16(e * inv), axis=1), e_ref[...], preferred_element_type=F32)
                for e in (e0, e1, e2)]
        pages = []
        for page in range(n_pages):
            cs = slice(page * LANES, (page + 1) * LANES)
            branch_vals = (ob0_ref[0, 0, rs, cs].astype(F32), il_ref[0, page, rs, :], il_ref[1, page, rs, :])
            pages.append(sum(w[:, cs] * val for w, val in zip(wexp, branch_vals)))
        ob = jnp.concatenate(pages, axis=1)

        ma = (oa_ref[rs, :].astype(F32) * beta[:, :WIDTH_A]).astype(BF16)
        mb = (ob * beta[:, WIDTH_A:WIDTH_A + WIDTH_B]).astype(BF16)
        mc = (oc_ref[rs, :].astype(F32) * beta[:, WIDTH_A + WIDTH_B:]).astype(BF16)
        acc = jnp.dot(jnp.concatenate([ma, mb, mc], axis=1), wbf_ref[...], preferred_element_type=F32)
        y = _layer_norm(alpha * x_ref[rs, :] + acc, g_ref[...], b_ref[...])
        if tile_out:
            _store_token_tiles(o_ref, y, row0=part * half)
        else:
            o_ref[rs, :] = y


def _store_token_tiles(ref, y, row0=0):
    rows = y.shape[0]
    for j in range(D_MODEL // LANES):
        ref[pl.ds(row0 * SUBLANES + j, rows, stride=SUBLANES), :] = y[:, j * LANES:(j + 1) * LANES]


def _load_token_tiles(ref, rows, row0=0):
    return jnp.concatenate(
        [ref[pl.ds(row0 * SUBLANES + j, rows, stride=SUBLANES), :] for j in range(D_MODEL // LANES)], axis=1)


def _outproj(x2, oa, obs, lses, oc, beta, w, g, b, alpha, tile_out):
    N = x2.shape[0]
    tm = OUT_TM
    steps_per_seq = obs[0].shape[2] // tm
    expand = np.zeros((2 * LANES, WIDTH_B), np.float32)
    for h in range(N_HEADS_B):
        expand[h, h * HEAD_DIM:(h + 1) * HEAD_DIM] = 1.0
        expand[LANES + h, h * HEAD_DIM:(h + 1) * HEAD_DIM] = 1.0
    row = lambda w_: pl.BlockSpec((tm, w_), lambda i: (i, 0))
    full = lambda r, c, **kw: pl.BlockSpec((r, c), lambda i: (0, 0), **kw)
    res = lambda d, w_: pl.BlockSpec((1, d, tm // d, w_),
                                     lambda i: (i // steps_per_seq, 0, i % steps_per_seq, 0))
    if tile_out:
        out_spec = pl.BlockSpec((tm * SUBLANES, LANES), lambda i: (i, 0))
        out_shape = jax.ShapeDtypeStruct((N * SUBLANES, LANES), F32)
    else:
        out_spec = row(D_MODEL)
        out_shape = jax.ShapeDtypeStruct((N, D_MODEL), F32)
    return pl.pallas_call(
        functools.partial(_outproj_kernel, alpha=alpha, tile_out=tile_out),
        grid=(N // tm,),
        in_specs=[row(D_MODEL), row(WIDTH_A)] + [res(d, WIDTH_B) for d in DILATIONS]
                 + [res(d, LANES) for d in DILATIONS] + [row(WIDTH_C),
                  full(1, D_MODEL), full(D_MODEL, D_MODEL, pipeline_mode=pl.Buffered(1)),
                  full(1, D_MODEL), full(1, D_MODEL), full(2 * LANES, WIDTH_B)],
        out_specs=out_spec,
        out_shape=out_shape,
        scratch_shapes=[pltpu.VMEM((D_MODEL, D_MODEL), BF16),
                        pltpu.VMEM((len(DILATIONS) - 1, WIDTH_B // LANES + 1, tm, LANES), F32)],
        compiler_params=_cparams(("arbitrary",)),
        name="merge_outproj_ln",
    )(x2, oa, obs[0], obs[1], obs[2], lses[0], lses[1], lses[2], oc,
      beta[None, :].astype(F32), w, g[None, :].astype(F32), b[None, :].astype(F32),
      jnp.asarray(expand, BF16))


CAST_ROWS = 256
DENSE_TM = 512
DENSE_TF = 256


def _cast_kernel(w_ref, o_ref):
    o_ref[...] = w_ref[...].astype(BF16)


def _to_bf16(w):
    rows, cols = w.shape
    return pl.pallas_call(
        _cast_kernel,
        grid=(rows // CAST_ROWS,),
        in_specs=[pl.BlockSpec((CAST_ROWS, cols), lambda i: (i, 0))],
        out_specs=pl.BlockSpec((CAST_ROWS, cols), lambda i: (i, 0)),
        out_shape=jax.ShapeDtypeStruct((rows, cols), BF16),
        compiler_params=_cparams(("arbitrary",)),
        name="cast_bf16",
    )(w)


def _dense_ffn_kernel(x_ref, wg_ref, wu_ref, wd_ref, g_ref, b_ref, o_ref, h_ref, *, alpha):
    x = x_ref[...]
    xb = x.astype(BF16)
    for c in range(h_ref.shape[1] // DENSE_TF):
        cs = slice(c * DENSE_TF, (c + 1) * DENSE_TF)
        gate = jnp.dot(xb, wg_ref[:, cs], preferred_element_type=F32)
        up = jnp.dot(xb, wu_ref[:, cs], preferred_element_type=F32)
        h_ref[:, cs] = (gate * jax.nn.sigmoid(gate) * up).astype(BF16)
    y = jnp.dot(h_ref[...], wd_ref[...], preferred_element_type=F32)
    o_ref[...] = _layer_norm(alpha * x + y, g_ref[...], b_ref[...])


def _dense_ffn(x, w_gate, w_up, w_down, g, b, alpha):
    n_tok = x.shape[0]
    ff = w_gate.shape[1]
    tm = DENSE_TM
    row = pl.BlockSpec((tm, D_MODEL), lambda i: (i, 0))
    resident = lambda r, c: pl.BlockSpec((r, c), lambda i: (0, 0), pipeline_mode=pl.Buffered(1))
    return pl.pallas_call(
        functools.partial(_dense_ffn_kernel, alpha=alpha),
        grid=(n_tok // tm,),
        in_specs=[row, resident(D_MODEL, ff), resident(D_MODEL, ff), resident(ff, D_MODEL),
                  pl.BlockSpec((1, D_MODEL), lambda i: (0, 0)), pl.BlockSpec((1, D_MODEL), lambda i: (0, 0))],
        out_specs=row,
        out_shape=jax.ShapeDtypeStruct((n_tok, D_MODEL), F32),
        scratch_shapes=[pltpu.VMEM((tm, ff), BF16)],
        compiler_params=_cparams(("arbitrary",)),
        name="swiglu_dense",
    )(x, _to_bf16(w_gate), _to_bf16(w_up), _to_bf16(w_down), g[None, :].astype(F32), b[None, :].astype(F32))


FFN_TM = 1024
FFN_TF = 512


def _ffn_kernel(te_ref, nt_ref, x_ref, wg_ref, wu_ref, wd_ref, o_ref, xb_ref, acc_ref, *, n_chunks):
    i, j = pl.program_id(0), pl.program_id(1)
    tm = xb_ref.shape[0]

    @pl.when(i < nt_ref[0])
    def _():
        @pl.when(j == 0)
        def _():
            xb_ref[...] = _load_token_tiles(x_ref, tm).astype(BF16)
            acc_ref[...] = jnp.zeros_like(acc_ref)

        xb = xb_ref[...]
        gate = jnp.dot(xb, wg_ref[...].astype(BF16), preferred_element_type=F32)
        up = jnp.dot(xb, wu_ref[...].astype(BF16), preferred_element_type=F32)
        h = (gate * jax.nn.sigmoid(gate) * up).astype(BF16)
        acc_ref[...] += jnp.dot(h, wd_ref[...].astype(BF16), preferred_element_type=F32)

        @pl.when(j == n_chunks - 1)
        def _():
            _store_token_tiles(o_ref, acc_ref[...])

    @pl.when((i >= nt_ref[0]) & (j == n_chunks - 1))
    def _():
        o_ref[...] = jnp.zeros_like(o_ref)


def _grouped_ffn(x, w_gate, w_up, w_down, tile_expert, n_tiles):
    tm, tf = FFN_TM, FFN_TF
    n_chunks = w_gate.shape[-1] // tf
    max_tiles = tile_expert.shape[0]

    def tile_ix(i, nt):
        return jnp.minimum(i, nt[0] - 1)

    def chunk_ix(i, j, nt):
        return jnp.where(i < nt[0], j, n_chunks - 1)

    blk = (tm * SUBLANES, LANES)
    in_specs = [
        pl.BlockSpec(blk, lambda i, j, te, nt: (tile_ix(i, nt), 0)),
        pl.BlockSpec((None, D_MODEL, tf), lambda i, j, te, nt: (te[tile_ix(i, nt)], 0, chunk_ix(i, j, nt))),
        pl.BlockSpec((None, D_MODEL, tf), lambda i, j, te, nt: (te[tile_ix(i, nt)], 0, chunk_ix(i, j, nt))),
        pl.BlockSpec((None, tf, D_MODEL), lambda i, j, te, nt: (te[tile_ix(i, nt)], chunk_ix(i, j, nt), 0)),
    ]
    return pl.pallas_call(
        functools.partial(_ffn_kernel, n_chunks=n_chunks),
        grid_spec=pltpu.PrefetchScalarGridSpec(
            num_scalar_prefetch=2,
            grid=(max_tiles, n_chunks),
            in_specs=in_specs,
            out_specs=pl.BlockSpec(blk, lambda i, j, te, nt: (i, 0)),
            scratch_shapes=[pltpu.VMEM((tm, D_MODEL), BF16), pltpu.VMEM((tm, D_MODEL), F32)],
        ),
        out_shape=jax.ShapeDtypeStruct(x.shape, F32),
        compiler_params=_cparams(("arbitrary", "arbitrary")),
        name="swiglu_grouped",
    )(tile_expert, n_tiles, x, w_gate, w_up, w_down)


ROUTE_TB = 512
META_E1, META_E2, META_R1, META_R2, META_G1, META_G2 = range(6)


def _router_kernel(x_ref, wr_ref, meta_ref, cnt_ref, carry_ref):
    tb = ROUTE_TB

    @pl.when(pl.program_id(0) == 0)
    def _():
        carry_ref[...] = jnp.zeros_like(carry_ref)

    x = _load_token_tiles(x_ref, tb)
    xh, xl = _split_bf16(x)
    wh, wl = _split_bf16(wr_ref[...])
    logits = (jnp.dot(xh, wh, preferred_element_type=F32) + jnp.dot(xh, wl, preferred_element_type=F32)
              + jnp.dot(xl, wh, preferred_element_type=F32))
    lane = lax.broadcasted_iota(jnp.int32, (tb, LANES), 1).astype(F32)
    logits = jnp.where(lane < N_EXPERTS, logits, -jnp.inf)
    m1 = jnp.max(logits, axis=-1, keepdims=True)
    i1 = jnp.min(jnp.where(logits == m1, lane, float(LANES)), axis=-1, keepdims=True)
    rest = jnp.where(lane == i1, -jnp.inf, logits)
    m2 = jnp.max(rest, axis=-1, keepdims=True)
    i2 = jnp.min(jnp.where(rest == m2, lane, float(LANES)), axis=-1, keepdims=True)
    e21 = jnp.exp(m2 - m1)
    g1 = 1.0 / (1.0 + e21)
    g2 = e21 * g1

    member = ((lane == i1) | (lane == i2)).astype(F32)
    r_i = lax.broadcasted_iota(jnp.int32, (tb, tb), 0)
    c_i = lax.broadcasted_iota(jnp.int32, (tb, tb), 1)
    lower = (c_i < r_i).astype(BF16)
    before = jnp.dot(lower, member.astype(BF16), preferred_element_type=F32)
    rank = carry_ref[...] + before
    r1 = jnp.sum(jnp.where(lane == i1, rank, 0.0), axis=-1, keepdims=True)
    r2 = jnp.sum(jnp.where(lane == i2, rank, 0.0), axis=-1, keepdims=True)
    carry_ref[...] += jnp.sum(member, axis=0, keepdims=True)
    cnt_ref[...] = carry_ref[...]

    meta = jnp.zeros((tb, LANES), F32)
    for slot, val in ((META_E1, i1.astype(F32)), (META_E2, i2.astype(F32)), (META_R1, r1), (META_R2, r2),
                      (META_G1, g1), (META_G2, g2)):
        meta = jnp.where(lane == slot, val, meta)
    meta_ref[...] = meta


def _router(xt, w_router):
    n_tok = xt.shape[0] // SUBLANES
    tb = ROUTE_TB
    wr = jnp.zeros((D_MODEL, LANES), F32).at[:, :N_EXPERTS].set(w_router.astype(F32))
    return pl.pallas_call(
        _router_kernel,
        grid=(n_tok // tb,),
        in_specs=[pl.BlockSpec((tb * SUBLANES, LANES), lambda i: (i, 0)),
                  pl.BlockSpec((D_MODEL, LANES), lambda i: (0, 0))],
        out_specs=[pl.BlockSpec((tb, LANES), lambda i: (i, 0)),
                   pl.BlockSpec((1, LANES), lambda i: (0, 0))],
        out_shape=[jax.ShapeDtypeStruct((n_tok, LANES), F32), jax.ShapeDtypeStruct((1, LANES), F32)],
        scratch_shapes=[pltpu.VMEM((1, LANES), F32)],
        compiler_params=_cparams(("arbitrary",)),
        name="router_top2",
    )(xt, wr)


MOVE_TB = 1024
PAD_CHUNK = 64
DMA_ISSUE_UNROLL = 8


def _tile_rows(ref, idx):
    return ref.at[pl.ds(pl.multiple_of(idx * SUBLANES, SUBLANES), SUBLANES), :]


def _zero_fill(pad_ref, xs_hbm, zbuf, zsem):
    zbuf[...] = jnp.zeros_like(zbuf)
    one = _tile_rows(zbuf, 0)
    for wait in (False, True):
        for seg in range(pad_ref.shape[1]):
            start, n = pad_ref[0, seg], pad_ref[1, seg]
            n_single = n % PAD_CHUNK

            def single(t, c):
                cp = pltpu.make_async_copy(one, _tile_rows(xs_hbm, start + t), zsem)
                cp.wait() if wait else cp.start()
                return c

            def bulk(t, c):
                r0 = pl.multiple_of((start + n_single + t * PAD_CHUNK) * SUBLANES, SUBLANES)
                cp = pltpu.make_async_copy(zbuf, xs_hbm.at[pl.ds(r0, PAD_CHUNK * SUBLANES), :], zsem)
                cp.wait() if wait else cp.start()
                return c

            lax.fori_loop(0, n_single, single, 0)
            lax.fori_loop(0, n // PAD_CHUNK, bulk, 0)


def _dispatch_kernel(dest_ref, pad_ref, x_ref, xs_hbm, zbuf, sem, zsem):
    tb = MOVE_TB

    @pl.when(pl.program_id(0) == 0)
    def _():
        _zero_fill(pad_ref, xs_hbm, zbuf, zsem)

    def body(t, c):
        src = _tile_rows(x_ref, t)
        for k in range(TOP_K):
            pltpu.make_async_copy(src, _tile_rows(xs_hbm, dest_ref[0, 0, TOP_K * t + k]), sem).start(priority=k)
        return c

    lax.fori_loop(0, tb, body, 0, unroll=DMA_ISSUE_UNROLL)
    n_rows = TOP_K * tb * SUBLANES
    pltpu.make_async_copy(xs_hbm.at[pl.ds(0, n_rows), :], xs_hbm.at[pl.ds(0, n_rows), :], sem).wait()


def _dispatch(xt, dest, pads, n_slots):
    n_tok = xt.shape[0] // SUBLANES
    tb = MOVE_TB
    dest3 = dest.reshape(n_tok // tb, 1, TOP_K * tb)
    return pl.pallas_call(
        _dispatch_kernel,
        grid=(n_tok // tb,),
        in_specs=[pl.BlockSpec((1, 1, TOP_K * tb), lambda i: (i, 0, 0), memory_space=pltpu.SMEM),
                  pl.BlockSpec(memory_space=pltpu.SMEM),
                  pl.BlockSpec((tb * SUBLANES, LANES), lambda i: (i, 0))],
        out_specs=pl.BlockSpec(memory_space=pl.ANY),
        out_shape=jax.ShapeDtypeStruct((n_slots * SUBLANES, LANES), F32),
        scratch_shapes=[pltpu.VMEM((PAD_CHUNK * SUBLANES, LANES), F32),
                        pltpu.SemaphoreType.DMA(()), pltpu.SemaphoreType.DMA(())],
        compiler_params=_cparams(("arbitrary",)),
        name="moe_dispatch",
    )(dest3, pads, xt)


COMB_TB = 256


def _combine_kernel(dest_ref, next_ref, y_hbm, x_ref, meta_ref, g_ref, b_ref, o_ref, ybuf, sem, *, alpha, n_steps):
    tb = COMB_TB
    i = pl.program_id(0)
    slot = i % 2

    def gather(d_ref, s):
        def body(t, c):
            for k in range(TOP_K):
                pltpu.make_async_copy(_tile_rows(y_hbm, d_ref[0, 0, TOP_K * t + k]),
                                      _tile_rows(ybuf.at[s], k * tb + t), sem.at[s]).start(priority=k)
            return c
        lax.fori_loop(0, tb, body, 0, unroll=DMA_ISSUE_UNROLL)

    @pl.when(i == 0)
    def _():
        gather(dest_ref, 0)

    @pl.when(i + 1 < n_steps)
    def _():
        gather(next_ref, 1 - slot)

    cur = ybuf.at[slot]
    pltpu.make_async_copy(cur, cur, sem.at[slot]).wait()

    meta = meta_ref[...]
    lane = lax.broadcasted_iota(jnp.int32, meta.shape, 1)
    g1 = jnp.sum(jnp.where(lane == META_G1, meta, 0.0), axis=-1, keepdims=True)
    g2 = jnp.sum(jnp.where(lane == META_G2, meta, 0.0), axis=-1, keepdims=True)
    x = _load_token_tiles(x_ref, tb)
    f = g1 * _load_token_tiles(cur, tb) + g2 * _load_token_tiles(cur, tb, row0=tb)
    o_ref[...] = _layer_norm(alpha * x + f, g_ref[...], b_ref[...])


def _combine(yt, xt, dest, meta, g, b, alpha):
    n_tok = xt.shape[0] // SUBLANES
    tb = COMB_TB
    n_steps = n_tok // tb
    dest3 = dest.reshape(n_steps, 1, TOP_K * tb)
    return pl.pallas_call(
        functools.partial(_combine_kernel, alpha=alpha, n_steps=n_steps),
        grid=(n_steps,),
        in_specs=[pl.BlockSpec((1, 1, TOP_K * tb), lambda i: (i, 0, 0), memory_space=pltpu.SMEM),
                  pl.BlockSpec((1, 1, TOP_K * tb), lambda i: (jnp.minimum(i + 1, n_steps - 1), 0, 0),
                               memory_space=pltpu.SMEM),
                  pl.BlockSpec(memory_space=pl.ANY),
                  pl.BlockSpec((tb * SUBLANES, LANES), lambda i: (i, 0)),
                  pl.BlockSpec((tb, LANES), lambda i: (i, 0)),
                  pl.BlockSpec((1, D_MODEL), lambda i: (0, 0)),
                  pl.BlockSpec((1, D_MODEL), lambda i: (0, 0))],
        out_specs=pl.BlockSpec((tb, D_MODEL), lambda i: (i, 0)),
        out_shape=jax.ShapeDtypeStruct((n_tok, D_MODEL), F32),
        scratch_shapes=[pltpu.VMEM((2, TOP_K * tb * SUBLANES, LANES), F32), pltpu.SemaphoreType.DMA((2,))],
        compiler_params=_cparams(("arbitrary",)),
        name="moe_combine_ln",
    )(dest3, dest3, yt, xt, meta, g[None, :].astype(F32), b[None, :].astype(F32))


def _moe_plan(meta, counts, n_tok):
    tm = FFN_TM
    max_tiles = n_tok * TOP_K // tm + N_EXPERTS
    cnt = counts[0, :N_EXPERTS].astype(jnp.int32)
    tiles = (cnt + tm - 1) // tm
    tile_end = jnp.cumsum(tiles)
    start = (tile_end - tiles) * tm
    e = meta[:, META_E1:META_E2 + 1].astype(jnp.int32)
    r = meta[:, META_R1:META_R2 + 1].astype(jnp.int32)
    dest = (start[e] + r).reshape(-1)
    tile_ids = jnp.arange(max_tiles, dtype=jnp.int32)
    tile_expert = jnp.minimum(jnp.sum((tile_ids[:, None] >= tile_end[None, :]).astype(jnp.int32), axis=1),
                              N_EXPERTS - 1)
    n_tiles = tile_end[-1:]
    pads = jnp.stack([jnp.concatenate([start + cnt, n_tiles * tm]),
                      jnp.concatenate([tiles * tm - cnt, (max_tiles - n_tiles) * tm])]).astype(jnp.int32)
    return dest, tile_expert, n_tiles.astype(jnp.int32), pads, max_tiles * tm


def kernel(x, w_in, w_out, beta_mix, diff_lambda, diff_norm_g, na_rpb, ln1_g, ln1_b, ln2_g, ln2_b,
           ffn_w_gate, ffn_w_up, ffn_w_down, moe_w_router, moe_w_gate, moe_w_up, moe_w_down):
    B, S, D = x.shape
    assert D == D_MODEL and S % (max(DILATIONS) * BAND_TQ) == 0 and S % PROJ_TM == 0
    depth = w_in.shape[0]
    alpha = (2 * depth) ** 0.25
    n_tok = B * S
    tables = _rope_tables(S)
    x = x.astype(F32)
    for l in range(depth):
        lam_init = 0.8 - 0.6 * math.exp(-0.3 * l)
        moe = l % 2 == 1
        qkv, *res_major = _project(x, w_in[l], tables)
        oa = _attn_a(qkv, diff_lambda[l], diff_norm_g[l], lam_init).reshape(n_tok, WIDTH_A)
        band = [_attn_b(qkv[:, None], DILATIONS[0], BAND_LO // WIDTH_B)]
        band += [_attn_b(src, d, 0) for d, src in zip(DILATIONS[1:], res_major)]
        oc = _attn_c(qkv, na_rpb[l]).reshape(n_tok, WIDTH_C)
        x1 = _outproj(x.reshape(n_tok, D), oa, [o for o, _ in band], [s for _, s in band], oc,
                      beta_mix[l], w_out[l], ln1_g[l], ln1_b[l], alpha, tile_out=moe)
        j = l // 2
        if not moe:
            x = _dense_ffn(x1, ffn_w_gate[j], ffn_w_up[j], ffn_w_down[j], ln2_g[l], ln2_b[l], alpha)
        else:
            meta, counts = _router(x1, moe_w_router[j])
            dest, tile_expert, n_tiles, pads, n_slots = _moe_plan(meta, counts, n_tok)
            xs = _dispatch(x1, dest, pads, n_slots)
            ys = _grouped_ffn(xs, moe_w_gate[j], moe_w_up[j], moe_w_down[j], tile_expert, n_tiles)
            x = _combine(ys, x1, dest, meta, ln2_g[l], ln2_b[l], alpha)
        x = x.reshape(B, S, D)
    return x
```

```python
import functools
import math
from typing import NamedTuple

import jax
import jax.numpy as jnp
import numpy as np
from jax import lax
from jax.experimental import pallas as pl
from jax.experimental.pallas import tpu as pltpu

F32 = jnp.float32
BF16 = jnp.bfloat16

D_MODEL = 1024
HEAD_DIM = 64
N_HEADS_A, N_HEADS_B, N_HEADS_C = 4, 6, 6
WIDTH_A, WIDTH_B, WIDTH_C = N_HEADS_A * HEAD_DIM, N_HEADS_B * HEAD_DIM, N_HEADS_C * HEAD_DIM
DIFF_DIM = HEAD_DIM // 2
ROPE_THETA = 500000.0
ROPE_FRACTION = 4
DILATIONS = (1, 4, 16)
assert DILATIONS[0] == 1 and DILATIONS[2] == DILATIONS[1] ** 2
BAND_HALF = 64
GRID_W = 64
NA_ROWS = 8
NA_COLS = 16
N_EXPERTS = 8
TOP_K = 2
LN_EPS = 1e-5
NEG = -1e30
LOG2E = math.log2(math.e)
LN2 = math.log(2.0)

LANES = 128
SUBLANES = 8
VMEM_LIMIT = 56 * 1024 * 1024

PROJ_W = 3 * (WIDTH_A + WIDTH_B + WIDTH_C)
ROPE_COLS = 3 * WIDTH_A + 2 * WIDTH_B
COL_CHUNK = 3 * WIDTH_A
BAND_LO = 3 * WIDTH_A
BAND_COLS = 3 * WIDTH_B


def _cparams(sem):
    return pltpu.CompilerParams(dimension_semantics=sem, vmem_limit_bytes=VMEM_LIMIT)


def _layer_norm(z, g, b):
    mu = jnp.mean(z, axis=-1, keepdims=True)
    zc = z - mu
    var = jnp.mean(zc * zc, axis=-1, keepdims=True)
    return zc * lax.rsqrt(var + LN_EPS) * g + b


def _rope_tables(seq):
    rot_a = DIFF_DIM // ROPE_FRACTION
    rot_b = HEAD_DIM // ROPE_FRACTION
    pos = jnp.arange(seq, dtype=F32)[:, None]
    inv_a = ROPE_THETA ** (-jnp.arange(0, rot_a, 2, dtype=F32) / rot_a)
    inv_b = ROPE_THETA ** (-jnp.arange(0, rot_b, 2, dtype=F32) / rot_b)
    ang_a, ang_b = pos * inv_a[None, :], pos * inv_b[None, :]

    cols = np.arange(ROPE_COLS)
    in_a = cols < 3 * WIDTH_A
    sub_a = cols // WIDTH_A
    within_a = cols % DIFF_DIM
    rot_mask_a = in_a & (sub_a < 2) & (within_a < rot_a)
    cb = cols - 3 * WIDTH_A
    sub_b = cb // WIDTH_B
    within_b = cb % HEAD_DIM
    rot_mask_b = (~in_a) & (within_b < rot_b)

    first_half = np.where(in_a, within_a < rot_a // 2, within_b < rot_b // 2)
    rot = rot_mask_a | rot_mask_b
    scale = np.ones(ROPE_COLS, np.float32)
    scale[in_a & (sub_a == 0)] = DIFF_DIM ** -0.5 * LOG2E
    scale[(~in_a) & (sub_b == 0)] = HEAD_DIM ** -0.5 * LOG2E

    def spread(tab_a, tab_b, fill):
        def groups(tab, width, reps):
            pad = jnp.full((seq, width - 2 * tab.shape[1]), fill, F32)
            return jnp.tile(jnp.concatenate([tab, tab, pad], axis=1), (1, reps))
        return jnp.concatenate([groups(tab_a, DIFF_DIM, 2 * WIDTH_A // DIFF_DIM),
                                jnp.full((seq, WIDTH_A), fill, F32),
                                groups(tab_b, HEAD_DIM, 2 * WIDTH_B // HEAD_DIM)], axis=1)

    cm = spread(jnp.cos(ang_a), jnp.cos(ang_b), 1.0) * jnp.asarray(scale)[None, :]
    sn = spread(jnp.sin(ang_a), jnp.sin(ang_b), 0.0) * jnp.asarray(scale)[None, :]
    ma = np.where(rot & first_half, -1.0, 0.0).astype(np.float32)[None, :]
    mb = np.where(rot & ~first_half, 1.0, 0.0).astype(np.float32)[None, :]
    return cm, sn, jnp.asarray(ma), jnp.asarray(mb)


PROJ_TM = 512
W_CAST_ROWS = 128


def _cast_rows_to_bf16(src_ref, dst_ref, rows):
    def body(i, c):
        r = pl.multiple_of(i * W_CAST_ROWS, W_CAST_ROWS)
        dst_ref[pl.ds(r, W_CAST_ROWS), :] = src_ref[pl.ds(r, W_CAST_ROWS), :].astype(BF16)
        return c
    lax.fori_loop(0, rows // W_CAST_ROWS, body, 0)


def _proj_kernel(x_ref, w_ref, cm_ref, sn_ref, ma_ref, mb_ref, o_ref, *rest):
    res_refs, (wbf_ref, band_ref, split_ref) = rest[:-3], rest[-3:]

    @pl.when((pl.program_id(0) == 0) & (pl.program_id(1) == 0))
    def _():
        _cast_rows_to_bf16(w_ref, wbf_ref, D_MODEL)

    xb = x_ref[0].astype(BF16)
    tm = xb.shape[0]
    n_chunks = PROJ_W // COL_CHUNK
    for c in range(n_chunks):
        lo, hi = c * COL_CHUNK, (c + 1) * COL_CHUNK
        acc = jnp.dot(xb, wbf_ref[:, lo:hi], preferred_element_type=F32)
        if hi <= ROPE_COLS:
            half = (DIFF_DIM if c == 0 else HEAD_DIM) // ROPE_FRACTION // 2
            up = pltpu.roll(acc, COL_CHUNK - half, 1)
            dn = pltpu.roll(acc, half, 1)
            acc = acc * cm_ref[:, lo:hi] + (up * ma_ref[:, lo:hi] + dn * mb_ref[:, lo:hi]) * sn_ref[:, lo:hi]
        elif c == 2:
            lane = lax.broadcasted_iota(jnp.int32, (1, COL_CHUNK), 1)
            acc = acc * jnp.where(lane >= WIDTH_B, HEAD_DIM ** -0.5 * LOG2E, 1.0)
        o_ref[0, :, lo:hi] = acc.astype(BF16)
        for col in range(max(lo, BAND_LO), min(hi, BAND_LO + BAND_COLS), LANES):
            band_ref[(col - BAND_LO) // LANES] = acc[:, col - lo:col - lo + LANES]

    (d1, d2), (r1_ref, r2_ref) = DILATIONS[1:], res_refs
    for page in range(BAND_COLS // LANES):
        cs = slice(page * LANES, (page + 1) * LANES)
        for r in range(d1):
            rows = band_ref[page, pl.ds(r, tm // d1, stride=d1), :]
            r1_ref[0, r, :, cs] = rows.astype(BF16)
            split_ref[page, r] = rows
        for r in range(d2):
            r2_ref[0, r, :, cs] = split_ref[page, r % d1, pl.ds(r // d1, tm // d2, stride=d1), :].astype(BF16)


def _project(x, w, tables):
    B, S, _ = x.shape
    cm, sn, ma, mb = tables
    tm = PROJ_TM
    res_specs = [pl.BlockSpec((1, d, tm // d, BAND_COLS), lambda p, b: (b, 0, p, 0)) for d in DILATIONS[1:]]
    res_shapes = [jax.ShapeDtypeStruct((B, d, S // d, BAND_COLS), BF16) for d in DILATIONS[1:]]
    return pl.pallas_call(
        _proj_kernel,
        grid=(S // tm, B),
        in_specs=[
            pl.BlockSpec((1, tm, D_MODEL), lambda p, b: (b, p, 0)),
            pl.BlockSpec((D_MODEL, PROJ_W), lambda p, b: (0, 0), pipeline_mode=pl.Buffered(1)),
            pl.BlockSpec((tm, ROPE_COLS), lambda p, b: (p, 0)),
            pl.BlockSpec((tm, ROPE_COLS), lambda p, b: (p, 0)),
            pl.BlockSpec((1, ROPE_COLS), lambda p, b: (0, 0)),
            pl.BlockSpec((1, ROPE_COLS), lambda p, b: (0, 0)),
        ],
        out_specs=[pl.BlockSpec((1, tm, PROJ_W), lambda p, b: (b, p, 0))] + res_specs,
        out_shape=[jax.ShapeDtypeStruct((B, S, PROJ_W), BF16)] + res_shapes,
        scratch_shapes=[pltpu.VMEM((D_MODEL, PROJ_W), BF16),
                        pltpu.VMEM((BAND_COLS // LANES, tm, LANES), F32),
                        pltpu.VMEM((BAND_COLS // LANES, DILATIONS[1], tm // DILATIONS[1], LANES), F32)],
        compiler_params=_cparams(("arbitrary", "arbitrary")),
        name="proj_in",
    )(x, w, cm, sn, ma, mb)


ATT_A_TQ = 256
ATT_A_UNROLL = 4


def _nt_dot(a, b):
    return lax.dot_general(a, b, (((1,), (1,)), ((), ())), preferred_element_type=F32)


def _with_ones(v):
    return jnp.concatenate([v, jnp.ones_like(v)], axis=1)


def _softmax_times_v(e, v1):
    nd = jnp.dot(e, v1, preferred_element_type=F32)
    return nd[:, :LANES] * (1.0 / nd[:, LANES:])


def _attn_a_kernel(lam_ref, g_ref, q_ref, k_ref, v_ref, o_ref, *, lam_init, seq):
    lp = lam_ref[...]
    lam = (jnp.exp(jnp.sum(lp[0:1] * lp[1:2], axis=-1, keepdims=True))
           - jnp.exp(jnp.sum(lp[2:3] * lp[3:4], axis=-1, keepdims=True)) + lam_init)
    lane = lax.broadcasted_iota(jnp.int32, (1, LANES), 1)
    first = lane < HEAD_DIM
    gain = g_ref[...] * (1.0 - lam_init)
    tq = ATT_A_TQ

    def qblock(i, carry):
        r0 = pl.multiple_of(i * tq, tq)
        q = q_ref[0, pl.ds(r0, tq), :]
        k = k_ref[0]
        v1 = _with_ones(v_ref[0])
        outs = []
        for hh in range(2):
            maps = []
            for m in range(2):
                lo = hh * HEAD_DIM + m * DIFF_DIM
                qm = jnp.where((lane >= lo) & (lane < lo + DIFF_DIM), q, jnp.zeros_like(q))
                s = _nt_dot(qm, k)
                e = jnp.exp2(s - jnp.max(s, axis=-1, keepdims=True)).astype(BF16)
                maps.append(_softmax_times_v(e, v1))
            outs.append(maps[0] - lam * maps[1])
        o = jnp.where(first, outs[0], outs[1])
        sq = o * o
        ss0 = jnp.sum(jnp.where(first, sq, 0.0), axis=-1, keepdims=True)
        ss1 = jnp.sum(jnp.where(first, 0.0, sq), axis=-1, keepdims=True)
        ms = jnp.where(first, ss0, ss1) * (1.0 / HEAD_DIM)
        o_ref[0, pl.ds(r0, tq), :] = (o * lax.rsqrt(ms + LN_EPS) * gain).astype(BF16)
        return carry

    lax.fori_loop(0, seq // tq, qblock, 0, unroll=ATT_A_UNROLL)


def _attn_a(qkv, diff_lambda, norm_g, lam_init):
    B, S, _ = qkv.shape
    n_pairs = WIDTH_A // LANES
    g2 = jnp.tile(norm_g.astype(F32), LANES // HEAD_DIM)[None, :]
    blk = lambda off: pl.BlockSpec((1, S, LANES), lambda b, h, off=off: (b, 0, off + h))
    return pl.pallas_call(
        functools.partial(_attn_a_kernel, lam_init=lam_init, seq=S),
        grid=(B, n_pairs),
        in_specs=[
            pl.BlockSpec((4, DIFF_DIM), lambda b, h: (0, 0)),
            pl.BlockSpec((1, LANES), lambda b, h: (0, 0)),
            blk(0), blk(n_pairs), blk(2 * n_pairs),
        ],
        out_specs=pl.BlockSpec((1, S, LANES), lambda b, h: (b, 0, h)),
        out_shape=jax.ShapeDtypeStruct((B, S, WIDTH_A), BF16),
        compiler_params=_cparams(("arbitrary", "arbitrary")),
        name="attn_diff",
    )(diff_lambda.astype(F32), g2, qkv, qkv, qkv)


BAND_TQ = 128
BAND_CHAINS = 12


def _stack_pair(q, lane):
    z = jnp.zeros_like(q)
    return jnp.concatenate([jnp.where(lane < HEAD_DIM, q, z), jnp.where(lane < HEAD_DIM, z, q)], axis=0)


def _attn_b_kernel(q_ref, k_ref, v_ref, o_ref, lse_ref, *, length, dil):
    tq = BAND_TQ
    win = min(tq + 2 * BAND_HALF, length)
    lane = lax.broadcasted_iota(jnp.int32, (1, LANES), 1)

    blocks = [(i * tq, min(max(i * tq - BAND_HALF, 0), length - win)) for i in range(length // tq)]
    row = lax.broadcasted_iota(jnp.int32, (2 * tq, win), 0)
    delta = lax.broadcasted_iota(jnp.int32, (2 * tq, win), 1) - jnp.where(row >= tq, row - tq, row)
    masks = {off: jnp.where(jnp.abs(delta - off) <= BAND_HALF, 0.0, NEG) for off in {q0 - k0 for q0, k0 in blocks}}

    def residue(r, carry):
        for q0, k0 in blocks:
            lse_tile = jnp.zeros((tq, LANES), F32)
            for pr in range(WIDTH_B // LANES):
                cs = slice(pr * LANES, (pr + 1) * LANES)
                q2 = _stack_pair(q_ref[0, r, q0:q0 + tq, cs], lane)
                s = _nt_dot(q2, k_ref[0, r, k0:k0 + win, cs]) + masks[q0 - k0]
                mx = jnp.max(s, axis=-1, keepdims=True)
                e = jnp.exp2(s - mx).astype(BF16)
                nd = jnp.dot(e, _with_ones(v_ref[0, r, k0:k0 + win, cs]), preferred_element_type=F32)
                l = nd[:, LANES:]
                o2 = nd[:, :LANES] * (1.0 / l)
                o_ref[0, r, q0:q0 + tq, cs] = jnp.where(lane < HEAD_DIM, o2[:tq], o2[tq:]).astype(BF16)
                lse2 = (mx + jnp.log2(l)) * LN2
                lse_tile = jnp.where(lane == 2 * pr, lse2[:tq], lse_tile)
                lse_tile = jnp.where(lane == 2 * pr + 1, lse2[tq:], lse_tile)
            lse_ref[0, r, q0:q0 + tq, :] = lse_tile
        return carry

    chains = len(blocks) * (WIDTH_B // LANES)
    lax.fori_loop(0, dil, residue, 0, unroll=min(dil, max(1, BAND_CHAINS // chains)))


def _attn_b(src, dil, first_blk):
    B, _, L, _ = src.shape
    blk = lambda off: pl.BlockSpec((1, dil, L, WIDTH_B), lambda b, off=off: (b, 0, 0, first_blk + off))
    return pl.pallas_call(
        functools.partial(_attn_b_kernel, length=L, dil=dil),
        grid=(B,),
        in_specs=[blk(0), blk(1), blk(2)],
        out_specs=[pl.BlockSpec((1, dil, L, WIDTH_B), lambda b: (b, 0, 0, 0)),
                   pl.BlockSpec((1, dil, L, LANES), lambda b: (b, 0, 0, 0))],
        out_shape=[jax.ShapeDtypeStruct((B, dil, L, WIDTH_B), BF16),
                   jax.ShapeDtypeStruct((B, dil, L, LANES), F32)],
        compiler_params=_cparams(("arbitrary",)),
        name=f"attn_band_d{dil}",
    )(src, src, src)


N_RPB_ROWS = 2 * NA_ROWS - 1
N_RPB_COLS = 2 * NA_COLS - 1
NA_KEYS = NA_ROWS * GRID_W
N_BIAS_VARIANTS = NA_ROWS
NA_ROWS_PER_STEP = 4


def _attn_c_kernel(u_ref, q_ref, k_ref, v_ref, o_ref, bias_ref, *, rows):
    lane = lax.broadcasted_iota(jnp.int32, (1, LANES), 1)
    n_pairs = WIDTH_C // LANES

    @pl.when(pl.program_id(0) == 0)
    def _():
        wide = N_RPB_ROWS * GRID_W
        c = lax.broadcasted_iota(jnp.int32, (GRID_W, wide), 0)
        kc = lax.broadcasted_iota(jnp.int32, (GRID_W, wide), 1) % GRID_W
        tidx = jnp.clip(kc - c + NA_COLS - 1, 0, N_RPB_COLS - 1)
        c5 = lax.broadcasted_iota(jnp.int32, (GRID_W, NA_KEYS), 0)
        kc5 = lax.broadcasted_iota(jnp.int32, (GRID_W, NA_KEYS), 1) % GRID_W
        cstart = jnp.clip(c5 - NA_COLS // 2, 0, GRID_W - NA_COLS)
        valid = (kc5 >= cstart) & (kc5 < cstart + NA_COLS)
        for h in range(N_HEADS_C):
            def tbody(t, acc, h=h):
                row = u_ref[h, pl.ds(t, 1), :]
                return jnp.where(tidx == t, row, acc)
            toep = lax.fori_loop(0, N_RPB_COLS, tbody, jnp.zeros((GRID_W, wide), F32))
            for var in range(N_BIAS_VARIANTS):
                tile = jnp.where(valid, toep[:, var * GRID_W:var * GRID_W + NA_KEYS] * LOG2E, NEG)
                bias_ref[var, h // 2, (h % 2) * GRID_W:(h % 2 + 1) * GRID_W, :] = tile

    def rbody(step, carry):
        for sub in range(NA_ROWS_PER_STEP):
            r = step * NA_ROWS_PER_STEP + sub
            rs = jnp.clip(r - NA_ROWS // 2, 0, rows - NA_ROWS)
            var = rs - r + NA_ROWS - 1
            q0 = pl.multiple_of(r * GRID_W, GRID_W)
            k0 = pl.multiple_of(rs * GRID_W, GRID_W)
            for pr in range(n_pairs):
                cs = slice(pr * LANES, (pr + 1) * LANES)
                q2 = _stack_pair(q_ref[0, pl.ds(q0, GRID_W), cs], lane)
                s = _nt_dot(q2, k_ref[0, pl.ds(k0, NA_KEYS), cs]) + bias_ref[var, pr]
                e = jnp.exp2(s - jnp.max(s, axis=-1, keepdims=True)).astype(BF16)
                o2 = _softmax_times_v(e, _with_ones(v_ref[0, pl.ds(k0, NA_KEYS), cs]))
                o_ref[0, pl.ds(q0, GRID_W), cs] = jnp.where(lane < HEAD_DIM, o2[:GRID_W], o2[GRID_W:]).astype(BF16)
        return carry

    lax.fori_loop(0, rows // NA_ROWS_PER_STEP, rbody, 0)


def _attn_c(qkv, rpb):
    B, S, _ = qkv.shape
    rows = S // GRID_W
    assert rows >= NA_ROWS
    u = jnp.repeat(jnp.transpose(rpb.astype(F32), (0, 2, 1)), GRID_W, axis=2)
    q_blk = (3 * WIDTH_A + 3 * WIDTH_B) // WIDTH_C
    blk = lambda off: pl.BlockSpec((1, S, WIDTH_C), lambda b, off=off: (b, 0, off))
    return pl.pallas_call(
        functools.partial(_attn_c_kernel, rows=rows),
        grid=(B,),
        in_specs=[pl.BlockSpec((N_HEADS_C, N_RPB_COLS, N_RPB_ROWS * GRID_W), lambda b: (0, 0, 0)),
                  blk(q_blk), blk(q_blk + 1), blk(q_blk + 2)],
        out_specs=pl.BlockSpec((1, S, WIDTH_C), lambda b: (b, 0, 0)),
        out_shape=jax.ShapeDtypeStruct((B, S, WIDTH_C), BF16),
        scratch_shapes=[pltpu.VMEM((N_BIAS_VARIANTS, WIDTH_C // LANES, LANES, NA_KEYS), F32)],
        compiler_params=_cparams(("arbitrary",)),
        name="attn_nbr",
    )(u, qkv, qkv, qkv)


OUT_TM = 512
OUT_SPLIT = 2


def _split_bf16(x):
    hi = x.astype(BF16)
    return hi, (x - hi.astype(F32)).astype(BF16)


def _outproj_kernel(x_ref, oa_ref, ob0_ref, ob1_ref, ob2_ref, l0_ref, l1_ref, l2_ref, oc_ref,
                    beta_ref, w_ref, g_ref, b_ref, e_ref, o_ref, wbf_ref, il_ref, *, alpha, tile_out):
    @pl.when(pl.program_id(0) == 0)
    def _():
        _cast_rows_to_bf16(w_ref, wbf_ref, D_MODEL)

    tm = x_ref.shape[0]
    n_pages = WIDTH_B // LANES
    for slot, (dil, ob_r, l_r) in enumerate(zip(DILATIONS[1:], (ob1_ref, ob2_ref), (l1_ref, l2_ref))):
        for r in range(dil):
            rows = pl.ds(r, tm // dil, stride=dil)
            for page in range(n_pages):
                il_ref[slot, page, rows, :] = ob_r[0, r, :, page * LANES:(page + 1) * LANES].astype(F32)
            il_ref[slot, n_pages, rows, :] = l_r[0, r]

    beta = beta_ref[...]
    half = tm // OUT_SPLIT
    for part in range(OUT_SPLIT):
        rs = slice(part * half, (part + 1) * half)
        l0, l1, l2 = l0_ref[0, 0, rs, :], il_ref[0, n_pages, rs, :], il_ref[1, n_pages, rs, :]
        lm = jnp.maximum(jnp.maximum(l0, l1), l2)
        e0, e1, e2 = jnp.exp(l0 - lm), jnp.exp(l1 - lm), jnp.exp(l2 - lm)
        inv = 1.0 / (e0 + e1 + e2)
        wexp = [jnp.dot(jnp.concatenate(_split_bf16(e * inv), axis=1), e_ref[...], preferred_element_type=F32)
                for e in (e0, e1, e2)]
        pages = []
        for page in range(n_pages):
            cs = slice(page * LANES, (page + 1) * LANES)
            branch_vals = (ob0_ref[0, 0, rs, cs].astype(F32), il_ref[0, page, rs, :], il_ref[1, page, rs, :])
            pages.append(sum(w[:, cs] * val for w, val in zip(wexp, branch_vals)))
        ob = jnp.concatenate(pages, axis=1)

        ma = (oa_ref[rs, :].astype(F32) * beta[:, :WIDTH_A]).astype(BF16)
        mb = (ob * beta[:, WIDTH_A:WIDTH_A + WIDTH_B]).astype(BF16)
        mc = (oc_ref[rs, :].astype(F32) * beta[:, WIDTH_A + WIDTH_B:]).astype(BF16)
        acc = jnp.dot(jnp.concatenate([ma, mb, mc], axis=1), wbf_ref[...], preferred_element_type=F32)
        y = _layer_norm(alpha * x_ref[rs, :] + acc, g_ref[...], b_ref[...])
        if tile_out:
            _store_token_tiles(o_ref, y, row0=part * half)
        else:
            o_ref[rs, :] = y


def _store_token_tiles(ref, y, row0=0):
    rows = y.shape[0]
    for j in range(D_MODEL // LANES):
        ref[pl.ds(row0 * SUBLANES + j, rows, stride=SUBLANES), :] = y[:, j * LANES:(j + 1) * LANES]


def _load_token_tiles(ref, rows, row0=0):
    return jnp.concatenate(
        [ref[pl.ds(row0 * SUBLANES + j, rows, stride=SUBLANES), :] for j in range(D_MODEL // LANES)], axis=1)


def _outproj(x2, oa, obs, lses, oc, beta, w, g, b, alpha, tile_out):
    N = x2.shape[0]
    tm = OUT_TM
    steps_per_seq = obs[0].shape[2] // tm
    expand = np.zeros((2 * LANES, WIDTH_B), np.float32)
    for h in range(N_HEADS_B):
        expand[h, h * HEAD_DIM:(h + 1) * HEAD_DIM] = 1.0
        expand[LANES + h, h * HEAD_DIM:(h + 1) * HEAD_DIM] = 1.0
    row = lambda w_: pl.BlockSpec((tm, w_), lambda i: (i, 0))
    full = lambda r, c, **kw: pl.BlockSpec((r, c), lambda i: (0, 0), **kw)
    res = lambda d, w_: pl.BlockSpec((1, d, tm // d, w_),
                                     lambda i: (i // steps_per_seq, 0, i % steps_per_seq, 0))
    if tile_out:
        out_spec = pl.BlockSpec((tm * SUBLANES, LANES), lambda i: (i, 0))
        out_shape = jax.ShapeDtypeStruct((N * SUBLANES, LANES), F32)
    else:
        out_spec = row(D_MODEL)
        out_shape = jax.ShapeDtypeStruct((N, D_MODEL), F32)
    return pl.pallas_call(
        functools.partial(_outproj_kernel, alpha=alpha, tile_out=tile_out),
        grid=(N // tm,),
        in_specs=[row(D_MODEL), row(WIDTH_A)] + [res(d, WIDTH_B) for d in DILATIONS]
                 + [res(d, LANES) for d in DILATIONS] + [row(WIDTH_C),
                  full(1, D_MODEL), full(D_MODEL, D_MODEL, pipeline_mode=pl.Buffered(1)),
                  full(1, D_MODEL), full(1, D_MODEL), full(2 * LANES, WIDTH_B)],
        out_specs=out_spec,
        out_shape=out_shape,
        scratch_shapes=[pltpu.VMEM((D_MODEL, D_MODEL), BF16),
                        pltpu.VMEM((len(DILATIONS) - 1, WIDTH_B // LANES + 1, tm, LANES), F32)],
        compiler_params=_cparams(("arbitrary",)),
        name="merge_outproj_ln",
    )(x2, oa, obs[0], obs[1], obs[2], lses[0], lses[1], lses[2], oc,
      beta[None, :].astype(F32), w, g[None, :].astype(F32), b[None, :].astype(F32),
      jnp.asarray(expand, BF16))


CAST_ROWS = 256
DENSE_TM = 512
DENSE_TF = 256


def _cast_kernel(w_ref, o_ref):
    o_ref[...] = w_ref[...].astype(BF16)


def _to_bf16(w):
    rows, cols = w.shape
    return pl.pallas_call(
        _cast_kernel,
        grid=(rows // CAST_ROWS,),
        in_specs=[pl.BlockSpec((CAST_ROWS, cols), lambda i: (i, 0))],
        out_specs=pl.BlockSpec((CAST_ROWS, cols), lambda i: (i, 0)),
        out_shape=jax.ShapeDtypeStruct((rows, cols), BF16),
        compiler_params=_cparams(("arbitrary",)),
        name="cast_bf16",
    )(w)


def _dense_ffn_kernel(x_ref, wg_ref, wu_ref, wd_ref, g_ref, b_ref, o_ref, h_ref, *, alpha):
    x = x_ref[...]
    xb = x.astype(BF16)
    for c in range(h_ref.shape[1] // DENSE_TF):
        cs = slice(c * DENSE_TF, (c + 1) * DENSE_TF)
        gate = jnp.dot(xb, wg_ref[:, cs], preferred_element_type=F32)
        up = jnp.dot(xb, wu_ref[:, cs], preferred_element_type=F32)
        h_ref[:, cs] = (gate * jax.nn.sigmoid(gate) * up).astype(BF16)
    y = jnp.dot(h_ref[...], wd_ref[...], preferred_element_type=F32)
    o_ref[...] = _layer_norm(alpha * x + y, g_ref[...], b_ref[...])


def _dense_ffn(x, w_gate, w_up, w_down, g, b, alpha):
    n_tok = x.shape[0]
    ff = w_gate.shape[1]
    tm = DENSE_TM
    row = pl.BlockSpec((tm, D_MODEL), lambda i: (i, 0))
    resident = lambda r, c: pl.BlockSpec((r, c), lambda i: (0, 0), pipeline_mode=pl.Buffered(1))
    return pl.pallas_call(
        functools.partial(_dense_ffn_kernel, alpha=alpha),
        grid=(n_tok // tm,),
        in_specs=[row, resident(D_MODEL, ff), resident(D_MODEL, ff), resident(ff, D_MODEL),
                  pl.BlockSpec((1, D_MODEL), lambda i: (0, 0)), pl.BlockSpec((1, D_MODEL), lambda i: (0, 0))],
        out_specs=row,
        out_shape=jax.ShapeDtypeStruct((n_tok, D_MODEL), F32),
        scratch_shapes=[pltpu.VMEM((tm, ff), BF16)],
        compiler_params=_cparams(("arbitrary",)),
        name="swiglu_dense",
    )(x, _to_bf16(w_gate), _to_bf16(w_up), _to_bf16(w_down), g[None, :].astype(F32), b[None, :].astype(F32))


FFN_TM = 1024
FFN_TF = 512
FFN_SUB = 256


def _ffn_kernel(te_ref, nt_ref, rows_ref, x_ref, wg_ref, wu_ref, wd_ref, o_ref, xb_ref, acc_ref, *, n_chunks):
    i, j = pl.program_id(0), pl.program_id(1)
    tm = xb_ref.shape[0]

    def swiglu_rows(rs):
        xb = xb_ref[rs, :]
        gate = jnp.dot(xb, wg_ref[...].astype(BF16), preferred_element_type=F32)
        up = jnp.dot(xb, wu_ref[...].astype(BF16), preferred_element_type=F32)
        h = (gate * jax.nn.sigmoid(gate) * up).astype(BF16)
        acc_ref[rs, :] += jnp.dot(h, wd_ref[...].astype(BF16), preferred_element_type=F32)

    @pl.when(i < nt_ref[0])
    def _():
        @pl.when(j == 0)
        def _():
            xb_ref[...] = _load_token_tiles(x_ref, tm).astype(BF16)
            acc_ref[...] = jnp.zeros_like(acc_ref)

        @pl.when(rows_ref[i] == tm)
        def _():
            swiglu_rows(slice(None))

        for sb in range(tm // FFN_SUB):
            @pl.when((rows_ref[i] < tm) & (sb * FFN_SUB < rows_ref[i]))
            def _(sb=sb):
                swiglu_rows(slice(sb * FFN_SUB, (sb + 1) * FFN_SUB))

        @pl.when(j == n_chunks - 1)
        def _():
            _store_token_tiles(o_ref, acc_ref[...])

    @pl.when((i >= nt_ref[0]) & (j == n_chunks - 1))
    def _():
        o_ref[...] = jnp.zeros_like(o_ref)


def _grouped_ffn(x, w_gate, w_up, w_down, tile_expert, n_tiles, tile_rows):
    tm, tf = FFN_TM, FFN_TF
    n_chunks = w_gate.shape[-1] // tf
    max_tiles = tile_expert.shape[0]

    def tile_ix(i, nt):
        return jnp.minimum(i, nt[0] - 1)

    def chunk_ix(i, j, nt):
        return jnp.where(i < nt[0], j, n_chunks - 1)

    blk = (tm * SUBLANES, LANES)
    in_specs = [
        pl.BlockSpec(blk, lambda i, j, te, nt, tr: (tile_ix(i, nt), 0)),
        pl.BlockSpec((None, D_MODEL, tf), lambda i, j, te, nt, tr: (te[tile_ix(i, nt)], 0, chunk_ix(i, j, nt))),
        pl.BlockSpec((None, D_MODEL, tf), lambda i, j, te, nt, tr: (te[tile_ix(i, nt)], 0, chunk_ix(i, j, nt))),
        pl.BlockSpec((None, tf, D_MODEL), lambda i, j, te, nt, tr: (te[tile_ix(i, nt)], chunk_ix(i, j, nt), 0)),
    ]
    return pl.pallas_call(
        functools.partial(_ffn_kernel, n_chunks=n_chunks),
        grid_spec=pltpu.PrefetchScalarGridSpec(
            num_scalar_prefetch=3,
            grid=(max_tiles, n_chunks),
            in_specs=in_specs,
            out_specs=pl.BlockSpec(blk, lambda i, j, te, nt, tr: (i, 0)),
            scratch_shapes=[pltpu.VMEM((tm, D_MODEL), BF16), pltpu.VMEM((tm, D_MODEL), F32)],
        ),
        out_shape=jax.ShapeDtypeStruct(x.shape, F32),
        compiler_params=_cparams(("arbitrary", "arbitrary")),
        name="swiglu_grouped",
    )(tile_expert, n_tiles, tile_rows, x, w_gate, w_up, w_down)


ROUTE_TB = 512
META_E1, META_E2, META_R1, META_R2, META_G1, META_G2 = range(6)


def _router_kernel(x_ref, wr_ref, meta_ref, cnt_ref, carry_ref):
    tb = ROUTE_TB

    @pl.when(pl.program_id(0) == 0)
    def _():
        carry_ref[...] = jnp.zeros_like(carry_ref)

    x = _load_token_tiles(x_ref, tb)
    xh, xl = _split_bf16(x)
    wh, wl = _split_bf16(wr_ref[...])
    logits = (jnp.dot(xh, wh, preferred_element_type=F32) + jnp.dot(xh, wl, preferred_element_type=F32)
              + jnp.dot(xl, wh, preferred_element_type=F32))
    lane = lax.broadcasted_iota(jnp.int32, (tb, LANES), 1).astype(F32)
    logits = jnp.where(lane < N_EXPERTS, logits, -jnp.inf)
    m1 = jnp.max(logits, axis=-1, keepdims=True)
    i1 = jnp.min(jnp.where(logits == m1, lane, float(LANES)), axis=-1, keepdims=True)
    rest = jnp.where(lane == i1, -jnp.inf, logits)
    m2 = jnp.max(rest, axis=-1, keepdims=True)
    i2 = jnp.min(jnp.where(rest == m2, lane, float(LANES)), axis=-1, keepdims=True)
    e21 = jnp.exp(m2 - m1)
    g1 = 1.0 / (1.0 + e21)
    g2 = e21 * g1

    member = ((lane == i1) | (lane == i2)).astype(F32)
    r_i = lax.broadcasted_iota(jnp.int32, (tb, tb), 0)
    c_i = lax.broadcasted_iota(jnp.int32, (tb, tb), 1)
    lower = (c_i < r_i).astype(BF16)
    before = jnp.dot(lower, member.astype(BF16), preferred_element_type=F32)
    rank = carry_ref[...] + before
    r1 = jnp.sum(jnp.where(lane == i1, rank, 0.0), axis=-1, keepdims=True)
    r2 = jnp.sum(jnp.where(lane == i2, rank, 0.0), axis=-1, keepdims=True)
    carry_ref[...] += jnp.sum(member, axis=0, keepdims=True)
    cnt_ref[...] = carry_ref[...]

    meta = jnp.zeros((tb, LANES), F32)
    for slot, val in ((META_E1, i1.astype(F32)), (META_E2, i2.astype(F32)), (META_R1, r1), (META_R2, r2),
                      (META_G1, g1), (META_G2, g2)):
        meta = jnp.where(lane == slot, val, meta)
    meta_ref[...] = meta


def _router(xt, w_router):
    n_tok = xt.shape[0] // SUBLANES
    tb = ROUTE_TB
    wr = jnp.zeros((D_MODEL, LANES), F32).at[:, :N_EXPERTS].set(w_router.astype(F32))
    return pl.pallas_call(
        _router_kernel,
        grid=(n_tok // tb,),
        in_specs=[pl.BlockSpec((tb * SUBLANES, LANES), lambda i: (i, 0)),
                  pl.BlockSpec((D_MODEL, LANES), lambda i: (0, 0))],
        out_specs=[pl.BlockSpec((tb, LANES), lambda i: (i, 0)),
                   pl.BlockSpec((1, LANES), lambda i: (0, 0))],
        out_shape=[jax.ShapeDtypeStruct((n_tok, LANES), F32), jax.ShapeDtypeStruct((1, LANES), F32)],
        scratch_shapes=[pltpu.VMEM((1, LANES), F32)],
        compiler_params=_cparams(("arbitrary",)),
        name="router_top2",
    )(xt, wr)


MOVE_TB = 1024
PAD_CHUNK = 64
DMA_ISSUE_UNROLL = 8


def _tile_rows(ref, idx):
    return ref.at[pl.ds(pl.multiple_of(idx * SUBLANES, SUBLANES), SUBLANES), :]


def _zero_fill(pad_ref, xs_hbm, zbuf, zsem):
    zbuf[...] = jnp.zeros_like(zbuf)
    one = _tile_rows(zbuf, 0)
    for wait in (False, True):
        for seg in range(pad_ref.shape[1]):
            start, n = pad_ref[0, seg], pad_ref[1, seg]
            n_single = n % PAD_CHUNK

            def single(t, c):
                cp = pltpu.make_async_copy(one, _tile_rows(xs_hbm, start + t), zsem)
                cp.wait() if wait else cp.start()
                return c

            def bulk(t, c):
                r0 = pl.multiple_of((start + n_single + t * PAD_CHUNK) * SUBLANES, SUBLANES)
                cp = pltpu.make_async_copy(zbuf, xs_hbm.at[pl.ds(r0, PAD_CHUNK * SUBLANES), :], zsem)
                cp.wait() if wait else cp.start()
                return c

            lax.fori_loop(0, n_single, single, 0)
            lax.fori_loop(0, n // PAD_CHUNK, bulk, 0)


def _dispatch_kernel(dest_ref, pad_ref, x_ref, xs_hbm, zbuf, sem, zsem):
    tb = MOVE_TB

    @pl.when(pl.program_id(0) == 0)
    def _():
        _zero_fill(pad_ref, xs_hbm, zbuf, zsem)

    def body(t, c):
        src = _tile_rows(x_ref, t)
        for k in range(TOP_K):
            pltpu.make_async_copy(src, _tile_rows(xs_hbm, dest_ref[0, 0, TOP_K * t + k]), sem).start(priority=k)
        return c

    lax.fori_loop(0, tb, body, 0, unroll=DMA_ISSUE_UNROLL)
    n_rows = TOP_K * tb * SUBLANES
    pltpu.make_async_copy(xs_hbm.at[pl.ds(0, n_rows), :], xs_hbm.at[pl.ds(0, n_rows), :], sem).wait()


def _dispatch(xt, dest, pads, n_slots):
    n_tok = xt.shape[0] // SUBLANES
    tb = MOVE_TB
    dest3 = dest.reshape(n_tok // tb, 1, TOP_K * tb)
    return pl.pallas_call(
        _dispatch_kernel,
        grid=(n_tok // tb,),
        in_specs=[pl.BlockSpec((1, 1, TOP_K * tb), lambda i: (i, 0, 0), memory_space=pltpu.SMEM),
                  pl.BlockSpec(memory_space=pltpu.SMEM),
                  pl.BlockSpec((tb * SUBLANES, LANES), lambda i: (i, 0))],
        out_specs=pl.BlockSpec(memory_space=pl.ANY),
        out_shape=jax.ShapeDtypeStruct((n_slots * SUBLANES, LANES), F32),
        scratch_shapes=[pltpu.VMEM((PAD_CHUNK * SUBLANES, LANES), F32),
                        pltpu.SemaphoreType.DMA(()), pltpu.SemaphoreType.DMA(())],
        compiler_params=_cparams(("arbitrary",)),
        name="moe_dispatch",
    )(dest3, pads, xt)


COMB_TB = 256


def _combine_kernel(dest_ref, next_ref, y_hbm, x_ref, meta_ref, g_ref, b_ref, o_ref, ybuf, sem, *, alpha, n_steps):
    tb = COMB_TB
    i = pl.program_id(0)
    slot = i % 2

    def gather(d_ref, s):
        def body(t, c):
            for k in range(TOP_K):
                pltpu.make_async_copy(_tile_rows(y_hbm, d_ref[0, 0, TOP_K * t + k]),
                                      _tile_rows(ybuf.at[s], k * tb + t), sem.at[s]).start(priority=k)
            return c
        lax.fori_loop(0, tb, body, 0, unroll=DMA_ISSUE_UNROLL)

    @pl.when(i == 0)
    def _():
        gather(dest_ref, 0)

    @pl.when(i + 1 < n_steps)
    def _():
        gather(next_ref, 1 - slot)

    cur = ybuf.at[slot]
    pltpu.make_async_copy(cur, cur, sem.at[slot]).wait()

    meta = meta_ref[...]
    lane = lax.broadcasted_iota(jnp.int32, meta.shape, 1)
    g1 = jnp.sum(jnp.where(lane == META_G1, meta, 0.0), axis=-1, keepdims=True)
    g2 = jnp.sum(jnp.where(lane == META_G2, meta, 0.0), axis=-1, keepdims=True)
    x = _load_token_tiles(x_ref, tb)
    f = g1 * _load_token_tiles(cur, tb) + g2 * _load_token_tiles(cur, tb, row0=tb)
    o_ref[...] = _layer_norm(alpha * x + f, g_ref[...], b_ref[...])


def _combine(yt, xt, dest, meta, g, b, alpha):
    n_tok = xt.shape[0] // SUBLANES
    tb = COMB_TB
    n_steps = n_tok // tb
    dest3 = dest.reshape(n_steps, 1, TOP_K * tb)
    return pl.pallas_call(
        functools.partial(_combine_kernel, alpha=alpha, n_steps=n_steps),
        grid=(n_steps,),
        in_specs=[pl.BlockSpec((1, 1, TOP_K * tb), lambda i: (i, 0, 0), memory_space=pltpu.SMEM),
                  pl.BlockSpec((1, 1, TOP_K * tb), lambda i: (jnp.minimum(i + 1, n_steps - 1), 0, 0),
                               memory_space=pltpu.SMEM),
                  pl.BlockSpec(memory_space=pl.ANY),
                  pl.BlockSpec((tb * SUBLANES, LANES), lambda i: (i, 0)),
                  pl.BlockSpec((tb, LANES), lambda i: (i, 0)),
                  pl.BlockSpec((1, D_MODEL), lambda i: (0, 0)),
                  pl.BlockSpec((1, D_MODEL), lambda i: (0, 0))],
        out_specs=pl.BlockSpec((tb, D_MODEL), lambda i: (i, 0)),
        out_shape=jax.ShapeDtypeStruct((n_tok, D_MODEL), F32),
        scratch_shapes=[pltpu.VMEM((2, TOP_K * tb * SUBLANES, LANES), F32), pltpu.SemaphoreType.DMA((2,))],
        compiler_params=_cparams(("arbitrary",)),
        name="moe_combine_ln",
    )(dest3, dest3, yt, xt, meta, g[None, :].astype(F32), b[None, :].astype(F32))


class MoePlan(NamedTuple):
    dest: jax.Array
    tile_expert: jax.Array
    tile_rows: jax.Array
    n_tiles: jax.Array
    pads: jax.Array
    n_slots: int


def _moe_plan(meta, counts, n_tok):
    tm = FFN_TM
    max_tiles = n_tok * TOP_K // tm + N_EXPERTS
    cnt = counts[0, :N_EXPERTS].astype(jnp.int32)
    tiles = (cnt + tm - 1) // tm
    tile_end = jnp.cumsum(tiles)
    start = (tile_end - tiles) * tm
    e = meta[:, META_E1:META_E2 + 1].astype(jnp.int32)
    r = meta[:, META_R1:META_R2 + 1].astype(jnp.int32)
    dest = (start[e] + r).reshape(-1)
    tile_ids = jnp.arange(max_tiles, dtype=jnp.int32)
    tile_expert = jnp.minimum(jnp.sum((tile_ids[:, None] >= tile_end[None, :]).astype(jnp.int32), axis=1),
                              N_EXPERTS - 1)
    first_tile = (tile_end - tiles)[tile_expert]
    tile_rows = jnp.clip(cnt[tile_expert] - (tile_ids - first_tile) * tm, 0, tm).astype(jnp.int32)
    n_tiles = tile_end[-1:]
    pads = jnp.stack([jnp.concatenate([start + cnt, n_tiles * tm]),
                      jnp.concatenate([tiles * tm - cnt, (max_tiles - n_tiles) * tm])]).astype(jnp.int32)
    return MoePlan(dest, tile_expert, tile_rows, n_tiles.astype(jnp.int32), pads, max_tiles * tm)


def kernel(x, w_in, w_out, beta_mix, diff_lambda, diff_norm_g, na_rpb, ln1_g, ln1_b, ln2_g, ln2_b,
           ffn_w_gate, ffn_w_up, ffn_w_down, moe_w_router, moe_w_gate, moe_w_up, moe_w_down):
    B, S, D = x.shape
    assert D == D_MODEL and S % (max(DILATIONS) * BAND_TQ) == 0 and S % PROJ_TM == 0
    depth = w_in.shape[0]
    alpha = (2 * depth) ** 0.25
    n_tok = B * S
    tables = _rope_tables(S)
    x = x.astype(F32)
    for l in range(depth):
        lam_init = 0.8 - 0.6 * math.exp(-0.3 * l)
        moe = l % 2 == 1
        qkv, *res_major = _project(x, w_in[l], tables)
        oa = _attn_a(qkv, diff_lambda[l], diff_norm_g[l], lam_init).reshape(n_tok, WIDTH_A)
        band = [_attn_b(qkv[:, None], DILATIONS[0], BAND_LO // WIDTH_B)]
        band += [_attn_b(src, d, 0) for d, src in zip(DILATIONS[1:], res_major)]
        oc = _attn_c(qkv, na_rpb[l]).reshape(n_tok, WIDTH_C)
        x1 = _outproj(x.reshape(n_tok, D), oa, [o for o, _ in band], [s for _, s in band], oc,
                      beta_mix[l], w_out[l], ln1_g[l], ln1_b[l], alpha, tile_out=moe)
        j = l // 2
        if not moe:
            x = _dense_ffn(x1, ffn_w_gate[j], ffn_w_up[j], ffn_w_down[j], ln2_g[l], ln2_b[l], alpha)
        else:
            meta, counts = _router(x1, moe_w_router[j])
            plan = _moe_plan(meta, counts, n_tok)
            xs = _dispatch(x1, plan.dest, plan.pads, plan.n_slots)
            ys = _grouped_ffn(xs, moe_w_gate[j], moe_w_up[j], moe_w_down[j],
                              plan.tile_expert, plan.n_tiles, plan.tile_rows)
            x = _combine(ys, x1, plan.dest, meta, ln2_g[l], ln2_b[l], alpha)
        x = x.reshape(B, S, D)
    return x
```

```python
import functools
import math
from typing import NamedTuple

import jax
import jax.numpy as jnp
import numpy as np
from jax import lax
from jax.experimental import pallas as pl
from jax.experimental.pallas import tpu as pltpu

F32 = jnp.float32
BF16 = jnp.bfloat16

D_MODEL = 1024
HEAD_DIM = 64
N_HEADS_A, N_HEADS_B, N_HEADS_C = 4, 6, 6
WIDTH_A, WIDTH_B, WIDTH_C = N_HEADS_A * HEAD_DIM, N_HEADS_B * HEAD_DIM, N_HEADS_C * HEAD_DIM
DIFF_DIM = HEAD_DIM // 2
ROPE_THETA = 500000.0
ROPE_FRACTION = 4
DILATIONS = (1, 4, 16)
assert DILATIONS[0] == 1 and DILATIONS[2] == DILATIONS[1] ** 2
BAND_HALF = 64
GRID_W = 64
NA_ROWS = 8
NA_COLS = 16
N_EXPERTS = 8
TOP_K = 2
LN_EPS = 1e-5
NEG = -1e30
LOG2E = math.log2(math.e)
LN2 = math.log(2.0)

LANES = 128
SUBLANES = 8
VMEM_LIMIT = 56 * 1024 * 1024

PROJ_W = 3 * (WIDTH_A + WIDTH_B + WIDTH_C)
ROPE_COLS = 3 * WIDTH_A + 2 * WIDTH_B
COL_CHUNK = 3 * WIDTH_A
BAND_LO = 3 * WIDTH_A
BAND_COLS = 3 * WIDTH_B


def _cparams(sem):
    return pltpu.CompilerParams(dimension_semantics=sem, vmem_limit_bytes=VMEM_LIMIT)


def _layer_norm(z, g, b):
    mu = jnp.mean(z, axis=-1, keepdims=True)
    zc = z - mu
    var = jnp.mean(zc * zc, axis=-1, keepdims=True)
    return zc * lax.rsqrt(var + LN_EPS) * g + b


def _rope_tables(seq):
    rot_a = DIFF_DIM // ROPE_FRACTION
    rot_b = HEAD_DIM // ROPE_FRACTION
    pos = jnp.arange(seq, dtype=F32)[:, None]
    inv_a = ROPE_THETA ** (-jnp.arange(0, rot_a, 2, dtype=F32) / rot_a)
    inv_b = ROPE_THETA ** (-jnp.arange(0, rot_b, 2, dtype=F32) / rot_b)
    ang_a, ang_b = pos * inv_a[None, :], pos * inv_b[None, :]

    cols = np.arange(ROPE_COLS)
    in_a = cols < 3 * WIDTH_A
    sub_a = cols // WIDTH_A
    within_a = cols % DIFF_DIM
    rot_mask_a = in_a & (sub_a < 2) & (within_a < rot_a)
    cb = cols - 3 * WIDTH_A
    sub_b = cb // WIDTH_B
    within_b = cb % HEAD_DIM
    rot_mask_b = (~in_a) & (within_b < rot_b)

    first_half = np.where(in_a, within_a < rot_a // 2, within_b < rot_b // 2)
    rot = rot_mask_a | rot_mask_b
    scale = np.ones(ROPE_COLS, np.float32)
    scale[in_a & (sub_a == 0)] = DIFF_DIM ** -0.5 * LOG2E
    scale[(~in_a) & (sub_b == 0)] = HEAD_DIM ** -0.5 * LOG2E

    def spread(tab_a, tab_b, fill):
        def groups(tab, width, reps):
            pad = jnp.full((seq, width - 2 * tab.shape[1]), fill, F32)
            return jnp.tile(jnp.concatenate([tab, tab, pad], axis=1), (1, reps))
        return jnp.concatenate([groups(tab_a, DIFF_DIM, 2 * WIDTH_A // DIFF_DIM),
                                jnp.full((seq, WIDTH_A), fill, F32),
                                groups(tab_b, HEAD_DIM, 2 * WIDTH_B // HEAD_DIM)], axis=1)

    cm = spread(jnp.cos(ang_a), jnp.cos(ang_b), 1.0) * jnp.asarray(scale)[None, :]
    sn = spread(jnp.sin(ang_a), jnp.sin(ang_b), 0.0) * jnp.asarray(scale)[None, :]
    ma = np.where(rot & first_half, -1.0, 0.0).astype(np.float32)[None, :]
    mb = np.where(rot & ~first_half, 1.0, 0.0).astype(np.float32)[None, :]
    return cm, sn, jnp.asarray(ma), jnp.asarray(mb)


PROJ_TM = 512
W_CAST_ROWS = 128


def _cast_rows_to_bf16(src_ref, dst_ref, rows):
    def body(i, c):
        r = pl.multiple_of(i * W_CAST_ROWS, W_CAST_ROWS)
        dst_ref[pl.ds(r, W_CAST_ROWS), :] = src_ref[pl.ds(r, W_CAST_ROWS), :].astype(BF16)
        return c
    lax.fori_loop(0, rows // W_CAST_ROWS, body, 0)


def _proj_kernel(x_ref, w_ref, cm_ref, sn_ref, ma_ref, mb_ref, o_ref, *rest):
    res_refs, (wbf_ref, band_ref, split_ref) = rest[:-3], rest[-3:]

    @pl.when((pl.program_id(0) == 0) & (pl.program_id(1) == 0))
    def _():
        _cast_rows_to_bf16(w_ref, wbf_ref, D_MODEL)

    xb = x_ref[0].astype(BF16)
    tm = xb.shape[0]
    n_chunks = PROJ_W // COL_CHUNK
    for c in range(n_chunks):
        lo, hi = c * COL_CHUNK, (c + 1) * COL_CHUNK
        acc = jnp.dot(xb, wbf_ref[:, lo:hi], preferred_element_type=F32)
        if hi <= ROPE_COLS:
            half = (DIFF_DIM if c == 0 else HEAD_DIM) // ROPE_FRACTION // 2
            up = pltpu.roll(acc, COL_CHUNK - half, 1)
            dn = pltpu.roll(acc, half, 1)
            acc = acc * cm_ref[:, lo:hi] + (up * ma_ref[:, lo:hi] + dn * mb_ref[:, lo:hi]) * sn_ref[:, lo:hi]
        elif c == 2:
            lane = lax.broadcasted_iota(jnp.int32, (1, COL_CHUNK), 1)
            acc = acc * jnp.where(lane >= WIDTH_B, HEAD_DIM ** -0.5 * LOG2E, 1.0)
        o_ref[0, :, lo:hi] = acc.astype(BF16)
        for col in range(max(lo, BAND_LO), min(hi, BAND_LO + BAND_COLS), LANES):
            band_ref[(col - BAND_LO) // LANES] = acc[:, col - lo:col - lo + LANES]

    (d1, d2), (r1_ref, r2_ref) = DILATIONS[1:], res_refs
    for page in range(BAND_COLS // LANES):
        cs = slice(page * LANES, (page + 1) * LANES)
        for r in range(d1):
            rows = band_ref[page, pl.ds(r, tm // d1, stride=d1), :]
            r1_ref[0, r, :, cs] = rows.astype(BF16)
            split_ref[page, r] = rows
        for r in range(d2):
            r2_ref[0, r, :, cs] = split_ref[page, r % d1, pl.ds(r // d1, tm // d2, stride=d1), :].astype(BF16)


def _project(x, w, tables):
    B, S, _ = x.shape
    cm, sn, ma, mb = tables
    tm = PROJ_TM
    res_specs = [pl.BlockSpec((1, d, tm // d, BAND_COLS), lambda p, b: (b, 0, p, 0)) for d in DILATIONS[1:]]
    res_shapes = [jax.ShapeDtypeStruct((B, d, S // d, BAND_COLS), BF16) for d in DILATIONS[1:]]
    return pl.pallas_call(
        _proj_kernel,
        grid=(S // tm, B),
        in_specs=[
            pl.BlockSpec((1, tm, D_MODEL), lambda p, b: (b, p, 0)),
            pl.BlockSpec((D_MODEL, PROJ_W), lambda p, b: (0, 0), pipeline_mode=pl.Buffered(1)),
            pl.BlockSpec((tm, ROPE_COLS), lambda p, b: (p, 0)),
            pl.BlockSpec((tm, ROPE_COLS), lambda p, b: (p, 0)),
            pl.BlockSpec((1, ROPE_COLS), lambda p, b: (0, 0)),
            pl.BlockSpec((1, ROPE_COLS), lambda p, b: (0, 0)),
        ],
        out_specs=[pl.BlockSpec((1, tm, PROJ_W), lambda p, b: (b, p, 0))] + res_specs,
        out_shape=[jax.ShapeDtypeStruct((B, S, PROJ_W), BF16)] + res_shapes,
        scratch_shapes=[pltpu.VMEM((D_MODEL, PROJ_W), BF16),
                        pltpu.VMEM((BAND_COLS // LANES, tm, LANES), F32),
                        pltpu.VMEM((BAND_COLS // LANES, DILATIONS[1], tm // DILATIONS[1], LANES), F32)],
        compiler_params=_cparams(("arbitrary", "arbitrary")),
        name="proj_in",
    )(x, w, cm, sn, ma, mb)


ATT_A_TQ = 256
ATT_A_UNROLL = 4


def _nt_dot(a, b):
    return lax.dot_general(a, b, (((1,), (1,)), ((), ())), preferred_element_type=F32)


def _with_ones(v):
    return jnp.concatenate([v, jnp.ones_like(v)], axis=1)


def _softmax_times_v(e, v1):
    nd = jnp.dot(e, v1, preferred_element_type=F32)
    return nd[:, :LANES] * (1.0 / nd[:, LANES:])


def _attn_a_kernel(lam_ref, g_ref, q_ref, k_ref, v_ref, o_ref, *, lam_init, seq):
    lp = lam_ref[...]
    lam = (jnp.exp(jnp.sum(lp[0:1] * lp[1:2], axis=-1, keepdims=True))
           - jnp.exp(jnp.sum(lp[2:3] * lp[3:4], axis=-1, keepdims=True)) + lam_init)
    lane = lax.broadcasted_iota(jnp.int32, (1, LANES), 1)
    first = lane < HEAD_DIM
    gain = g_ref[...] * (1.0 - lam_init)
    tq = ATT_A_TQ

    def qblock(i, carry):
        r0 = pl.multiple_of(i * tq, tq)
        q = q_ref[0, pl.ds(r0, tq), :]
        k = k_ref[0]
        v1 = _with_ones(v_ref[0])
        outs = []
        for hh in range(2):
            maps = []
            for m in range(2):
                lo = hh * HEAD_DIM + m * DIFF_DIM
                qm = jnp.where((lane >= lo) & (lane < lo + DIFF_DIM), q, jnp.zeros_like(q))
                s = _nt_dot(qm, k).astype(BF16)
                e = jnp.exp2(s - jnp.max(s, axis=-1, keepdims=True))
                maps.append(_softmax_times_v(e, v1))
            outs.append(maps[0] - lam * maps[1])
        o = jnp.where(first, outs[0], outs[1])
        sq = o * o
        ss0 = jnp.sum(jnp.where(first, sq, 0.0), axis=-1, keepdims=True)
        ss1 = jnp.sum(jnp.where(first, 0.0, sq), axis=-1, keepdims=True)
        ms = jnp.where(first, ss0, ss1) * (1.0 / HEAD_DIM)
        o_ref[0, pl.ds(r0, tq), :] = (o * lax.rsqrt(ms + LN_EPS) * gain).astype(BF16)
        return carry

    lax.fori_loop(0, seq // tq, qblock, 0, unroll=ATT_A_UNROLL)


def _attn_a(qkv, diff_lambda, norm_g, lam_init):
    B, S, _ = qkv.shape
    n_pairs = WIDTH_A // LANES
    g2 = jnp.tile(norm_g.astype(F32), LANES // HEAD_DIM)[None, :]
    blk = lambda off: pl.BlockSpec((1, S, LANES), lambda b, h, off=off: (b, 0, off + h))
    return pl.pallas_call(
        functools.partial(_attn_a_kernel, lam_init=lam_init, seq=S),
        grid=(B, n_pairs),
        in_specs=[
            pl.BlockSpec((4, DIFF_DIM), lambda b, h: (0, 0)),
            pl.BlockSpec((1, LANES), lambda b, h: (0, 0)),
            blk(0), blk(n_pairs), blk(2 * n_pairs),
        ],
        out_specs=pl.BlockSpec((1, S, LANES), lambda b, h: (b, 0, h)),
        out_shape=jax.ShapeDtypeStruct((B, S, WIDTH_A), BF16),
        compiler_params=_cparams(("arbitrary", "arbitrary")),
        name="attn_diff",
    )(diff_lambda.astype(F32), g2, qkv, qkv, qkv)


BAND_TQ = 128
BAND_CHAINS = 12


def _stack_pair(q, lane):
    z = jnp.zeros_like(q)
    return jnp.concatenate([jnp.where(lane < HEAD_DIM, q, z), jnp.where(lane < HEAD_DIM, z, q)], axis=0)


def _attn_b_kernel(q_ref, k_ref, v_ref, o_ref, lse_ref, *, length, dil):
    tq = BAND_TQ
    win = min(tq + 2 * BAND_HALF, length)
    lane = lax.broadcasted_iota(jnp.int32, (1, LANES), 1)

    blocks = [(i * tq, min(max(i * tq - BAND_HALF, 0), length - win)) for i in range(length // tq)]
    row = lax.broadcasted_iota(jnp.int32, (2 * tq, win), 0)
    delta = lax.broadcasted_iota(jnp.int32, (2 * tq, win), 1) - jnp.where(row >= tq, row - tq, row)
    masks = {off: jnp.where(jnp.abs(delta - off) <= BAND_HALF, 0.0, NEG) for off in {q0 - k0 for q0, k0 in blocks}}

    def residue(r, carry):
        for q0, k0 in blocks:
            lse_tile = jnp.zeros((tq, LANES), F32)
            for pr in range(WIDTH_B // LANES):
                cs = slice(pr * LANES, (pr + 1) * LANES)
                q2 = _stack_pair(q_ref[0, r, q0:q0 + tq, cs], lane)
                s = (_nt_dot(q2, k_ref[0, r, k0:k0 + win, cs]) + masks[q0 - k0]).astype(BF16)
                mx = jnp.max(s, axis=-1, keepdims=True)
                e = jnp.exp2(s - mx)
                mx = mx.astype(F32)
                nd = jnp.dot(e, _with_ones(v_ref[0, r, k0:k0 + win, cs]), preferred_element_type=F32)
                l = nd[:, LANES:]
                o2 = nd[:, :LANES] * (1.0 / l)
                o_ref[0, r, q0:q0 + tq, cs] = jnp.where(lane < HEAD_DIM, o2[:tq], o2[tq:]).astype(BF16)
                lse2 = (mx + jnp.log2(l)) * LN2
                lse_tile = jnp.where(lane == 2 * pr, lse2[:tq], lse_tile)
                lse_tile = jnp.where(lane == 2 * pr + 1, lse2[tq:], lse_tile)
            lse_ref[0, r, q0:q0 + tq, :] = lse_tile
        return carry

    chains = len(blocks) * (WIDTH_B // LANES)
    lax.fori_loop(0, dil, residue, 0, unroll=min(dil, max(1, BAND_CHAINS // chains)))


def _attn_b(src, dil, first_blk):
    B, _, L, _ = src.shape
    blk = lambda off: pl.BlockSpec((1, dil, L, WIDTH_B), lambda b, off=off: (b, 0, 0, first_blk + off))
    return pl.pallas_call(
        functools.partial(_attn_b_kernel, length=L, dil=dil),
        grid=(B,),
        in_specs=[blk(0), blk(1), blk(2)],
        out_specs=[pl.BlockSpec((1, dil, L, WIDTH_B), lambda b: (b, 0, 0, 0)),
                   pl.BlockSpec((1, dil, L, LANES), lambda b: (b, 0, 0, 0))],
        out_shape=[jax.ShapeDtypeStruct((B, dil, L, WIDTH_B), BF16),
                   jax.ShapeDtypeStruct((B, dil, L, LANES), F32)],
        compiler_params=_cparams(("arbitrary",)),
        name=f"attn_band_d{dil}",
    )(src, src, src)


N_RPB_ROWS = 2 * NA_ROWS - 1
N_RPB_COLS = 2 * NA_COLS - 1
NA_KEYS = NA_ROWS * GRID_W
N_BIAS_VARIANTS = NA_ROWS
NA_ROWS_PER_STEP = 4


def _attn_c_kernel(u_ref, q_ref, k_ref, v_ref, o_ref, bias_ref, *, rows):
    lane = lax.broadcasted_iota(jnp.int32, (1, LANES), 1)
    n_pairs = WIDTH_C // LANES

    @pl.when(pl.program_id(0) == 0)
    def _():
        wide = N_RPB_ROWS * GRID_W
        c = lax.broadcasted_iota(jnp.int32, (GRID_W, wide), 0)
        kc = lax.broadcasted_iota(jnp.int32, (GRID_W, wide), 1) % GRID_W
        tidx = jnp.clip(kc - c + NA_COLS - 1, 0, N_RPB_COLS - 1)
        c5 = lax.broadcasted_iota(jnp.int32, (GRID_W, NA_KEYS), 0)
        kc5 = lax.broadcasted_iota(jnp.int32, (GRID_W, NA_KEYS), 1) % GRID_W
        cstart = jnp.clip(c5 - NA_COLS // 2, 0, GRID_W - NA_COLS)
        valid = (kc5 >= cstart) & (kc5 < cstart + NA_COLS)
        for h in range(N_HEADS_C):
            def tbody(t, acc, h=h):
                row = u_ref[h, pl.ds(t, 1), :]
                return jnp.where(tidx == t, row, acc)
            toep = lax.fori_loop(0, N_RPB_COLS, tbody, jnp.zeros((GRID_W, wide), F32))
            for var in range(N_BIAS_VARIANTS):
                tile = jnp.where(valid, toep[:, var * GRID_W:var * GRID_W + NA_KEYS] * LOG2E, NEG)
                bias_ref[var, h // 2, (h % 2) * GRID_W:(h % 2 + 1) * GRID_W, :] = tile

    def rbody(step, carry):
        for sub in range(NA_ROWS_PER_STEP):
            r = step * NA_ROWS_PER_STEP + sub
            rs = jnp.clip(r - NA_ROWS // 2, 0, rows - NA_ROWS)
            var = rs - r + NA_ROWS - 1
            q0 = pl.multiple_of(r * GRID_W, GRID_W)
            k0 = pl.multiple_of(rs * GRID_W, GRID_W)
            for pr in range(n_pairs):
                cs = slice(pr * LANES, (pr + 1) * LANES)
                q2 = _stack_pair(q_ref[0, pl.ds(q0, GRID_W), cs], lane)
                s = _nt_dot(q2, k_ref[0, pl.ds(k0, NA_KEYS), cs]) + bias_ref[var, pr]
                e = jnp.exp2(s - jnp.max(s, axis=-1, keepdims=True)).astype(BF16)
                o2 = _softmax_times_v(e, _with_ones(v_ref[0, pl.ds(k0, NA_KEYS), cs]))
                o_ref[0, pl.ds(q0, GRID_W), cs] = jnp.where(lane < HEAD_DIM, o2[:GRID_W], o2[GRID_W:]).astype(BF16)
        return carry

    lax.fori_loop(0, rows // NA_ROWS_PER_STEP, rbody, 0)


def _attn_c(qkv, rpb):
    B, S, _ = qkv.shape
    rows = S // GRID_W
    assert rows >= NA_ROWS
    u = jnp.repeat(jnp.transpose(rpb.astype(F32), (0, 2, 1)), GRID_W, axis=2)
    q_blk = (3 * WIDTH_A + 3 * WIDTH_B) // WIDTH_C
    blk = lambda off: pl.BlockSpec((1, S, WIDTH_C), lambda b, off=off: (b, 0, off))
    return pl.pallas_call(
        functools.partial(_attn_c_kernel, rows=rows),
        grid=(B,),
        in_specs=[pl.BlockSpec((N_HEADS_C, N_RPB_COLS, N_RPB_ROWS * GRID_W), lambda b: (0, 0, 0)),
                  blk(q_blk), blk(q_blk + 1), blk(q_blk + 2)],
        out_specs=pl.BlockSpec((1, S, WIDTH_C), lambda b: (b, 0, 0)),
        out_shape=jax.ShapeDtypeStruct((B, S, WIDTH_C), BF16),
        scratch_shapes=[pltpu.VMEM((N_BIAS_VARIANTS, WIDTH_C // LANES, LANES, NA_KEYS), F32)],
        compiler_params=_cparams(("arbitrary",)),
        name="attn_nbr",
    )(u, qkv, qkv, qkv)


OUT_TM = 512
OUT_SPLIT = 2


def _split_bf16(x):
    hi = x.astype(BF16)
    return hi, (x - hi.astype(F32)).astype(BF16)


def _outproj_kernel(x_ref, oa_ref, ob0_ref, ob1_ref, ob2_ref, l0_ref, l1_ref, l2_ref, oc_ref,
                    beta_ref, w_ref, g_ref, b_ref, e_ref, o_ref, wbf_ref, il_ref, *, alpha, tile_out):
    @pl.when(pl.program_id(0) == 0)
    def _():
        _cast_rows_to_bf16(w_ref, wbf_ref, D_MODEL)

    tm = x_ref.shape[0]
    n_pages = WIDTH_B // LANES
    for slot, (dil, ob_r, l_r) in enumerate(zip(DILATIONS[1:], (ob1_ref, ob2_ref), (l1_ref, l2_ref))):
        for r in range(dil):
            rows = pl.ds(r, tm // dil, stride=dil)
            for page in range(n_pages):
                il_ref[slot, page, rows, :] = ob_r[0, r, :, page * LANES:(page + 1) * LANES].astype(F32)
            il_ref[slot, n_pages, rows, :] = l_r[0, r]

    beta = beta_ref[...]
    half = tm // OUT_SPLIT
    for part in range(OUT_SPLIT):
        rs = slice(part * half, (part + 1) * half)
        l0, l1, l2 = l0_ref[0, 0, rs, :], il_ref[0, n_pages, rs, :], il_ref[1, n_pages, rs, :]
        lm = jnp.maximum(jnp.maximum(l0, l1), l2)
        e0, e1, e2 = jnp.exp(l0 - lm), jnp.exp(l1 - lm), jnp.exp(l2 - lm)
        inv = 1.0 / (e0 + e1 + e2)
        wexp = [jnp.dot(jnp.concatenate(_split_bf16(e * inv), axis=1), e_ref[...], preferred_element_type=F32)
                for e in (e0, e1, e2)]
        pages = []
        for page in range(n_pages):
            cs = slice(page * LANES, (page + 1) * LANES)
            branch_vals = (ob0_ref[0, 0, rs, cs].astype(F32), il_ref[0, page, rs, :], il_ref[1, page, rs, :])
            pages.append(sum(w[:, cs] * val for w, val in zip(wexp, branch_vals)))
        ob = jnp.concatenate(pages, axis=1)

        ma = (oa_ref[rs, :].astype(F32) * beta[:, :WIDTH_A]).astype(BF16)
        mb = (ob * beta[:, WIDTH_A:WIDTH_A + WIDTH_B]).astype(BF16)
        mc = (oc_ref[rs, :].astype(F32) * beta[:, WIDTH_A + WIDTH_B:]).astype(BF16)
        acc = jnp.dot(jnp.concatenate([ma, mb, mc], axis=1), wbf_ref[...], preferred_element_type=F32)
        y = _layer_norm(alpha * x_ref[rs, :] + acc, g_ref[...], b_ref[...])
        if tile_out:
            _store_token_tiles(o_ref, y, row0=part * half)
        else:
            o_ref[rs, :] = y


def _store_token_tiles(ref, y, row0=0):
    rows = y.shape[0]
    for j in range(D_MODEL // LANES):
        ref[pl.ds(row0 * SUBLANES + j, rows, stride=SUBLANES), :] = y[:, j * LANES:(j + 1) * LANES]


def _load_token_tiles(ref, rows, row0=0):
    return jnp.concatenate(
        [ref[pl.ds(row0 * SUBLANES + j, rows, stride=SUBLANES), :] for j in range(D_MODEL // LANES)], axis=1)


def _outproj(x2, oa, obs, lses, oc, beta, w, g, b, alpha, tile_out):
    N = x2.shape[0]
    tm = OUT_TM
    steps_per_seq = obs[0].shape[2] // tm
    expand = np.zeros((2 * LANES, WIDTH_B), np.float32)
    for h in range(N_HEADS_B):
        expand[h, h * HEAD_DIM:(h + 1) * HEAD_DIM] = 1.0
        expand[LANES + h, h * HEAD_DIM:(h + 1) * HEAD_DIM] = 1.0
    row = lambda w_: pl.BlockSpec((tm, w_), lambda i: (i, 0))
    full = lambda r, c, **kw: pl.BlockSpec((r, c), lambda i: (0, 0), **kw)
    res = lambda d, w_: pl.BlockSpec((1, d, tm // d, w_),
                                     lambda i: (i // steps_per_seq, 0, i % steps_per_seq, 0))
    if tile_out:
        out_spec = pl.BlockSpec((tm * SUBLANES, LANES), lambda i: (i, 0))
        out_shape = jax.ShapeDtypeStruct((N * SUBLANES, LANES), F32)
    else:
        out_spec = row(D_MODEL)
        out_shape = jax.ShapeDtypeStruct((N, D_MODEL), F32)
    return pl.pallas_call(
        functools.partial(_outproj_kernel, alpha=alpha, tile_out=tile_out),
        grid=(N // tm,),
        in_specs=[row(D_MODEL), row(WIDTH_A)] + [res(d, WIDTH_B) for d in DILATIONS]
                 + [res(d, LANES) for d in DILATIONS] + [row(WIDTH_C),
                  full(1, D_MODEL), full(D_MODEL, D_MODEL, pipeline_mode=pl.Buffered(1)),
                  full(1, D_MODEL), full(1, D_MODEL), full(2 * LANES, WIDTH_B)],
        out_specs=out_spec,
        out_shape=out_shape,
        scratch_shapes=[pltpu.VMEM((D_MODEL, D_MODEL), BF16),
                        pltpu.VMEM((len(DILATIONS) - 1, WIDTH_B // LANES + 1, tm, LANES), F32)],
        compiler_params=_cparams(("arbitrary",)),
        name="merge_outproj_ln",
    )(x2, oa, obs[0], obs[1], obs[2], lses[0], lses[1], lses[2], oc,
      beta[None, :].astype(F32), w, g[None, :].astype(F32), b[None, :].astype(F32),
      jnp.asarray(expand, BF16))


CAST_ROWS = 256
DENSE_TM = 512
DENSE_TF = 256


def _cast_kernel(w_ref, o_ref):
    o_ref[...] = w_ref[...].astype(BF16)


def _to_bf16(w):
    rows, cols = w.shape
    return pl.pallas_call(
        _cast_kernel,
        grid=(rows // CAST_ROWS,),
        in_specs=[pl.BlockSpec((CAST_ROWS, cols), lambda i: (i, 0))],
        out_specs=pl.BlockSpec((CAST_ROWS, cols), lambda i: (i, 0)),
        out_shape=jax.ShapeDtypeStruct((rows, cols), BF16),
        compiler_params=_cparams(("arbitrary",)),
        name="cast_bf16",
    )(w)


def _dense_ffn_kernel(x_ref, wg_ref, wu_ref, wd_ref, g_ref, b_ref, o_ref, h_ref, *, alpha):
    x = x_ref[...]
    xb = x.astype(BF16)
    for c in range(h_ref.shape[1] // DENSE_TF):
        cs = slice(c * DENSE_TF, (c + 1) * DENSE_TF)
        gate = jnp.dot(xb, wg_ref[:, cs], preferred_element_type=F32)
        up = jnp.dot(xb, wu_ref[:, cs], preferred_element_type=F32)
        h_ref[:, cs] = (gate * jax.nn.sigmoid(gate) * up).astype(BF16)
    y = jnp.dot(h_ref[...], wd_ref[...], preferred_element_type=F32)
    o_ref[...] = _layer_norm(alpha * x + y, g_ref[...], b_ref[...])


def _dense_ffn(x, w_gate, w_up, w_down, g, b, alpha):
    n_tok = x.shape[0]
    ff = w_gate.shape[1]
    tm = DENSE_TM
    row = pl.BlockSpec((tm, D_MODEL), lambda i: (i, 0))
    resident = lambda r, c: pl.BlockSpec((r, c), lambda i: (0, 0), pipeline_mode=pl.Buffered(1))
    return pl.pallas_call(
        functools.partial(_dense_ffn_kernel, alpha=alpha),
        grid=(n_tok // tm,),
        in_specs=[row, resident(D_MODEL, ff), resident(D_MODEL, ff), resident(ff, D_MODEL),
                  pl.BlockSpec((1, D_MODEL), lambda i: (0, 0)), pl.BlockSpec((1, D_MODEL), lambda i: (0, 0))],
        out_specs=row,
        out_shape=jax.ShapeDtypeStruct((n_tok, D_MODEL), F32),
        scratch_shapes=[pltpu.VMEM((tm, ff), BF16)],
        compiler_params=_cparams(("arbitrary",)),
        name="swiglu_dense",
    )(x, _to_bf16(w_gate), _to_bf16(w_up), _to_bf16(w_down), g[None, :].astype(F32), b[None, :].astype(F32))


FFN_TM = 1024
FFN_TF = 512
FFN_SUB = 256


def _ffn_kernel(te_ref, nt_ref, rows_ref, x_ref, wg_ref, wu_ref, wd_ref, o_ref, xb_ref, acc_ref, *, n_chunks):
    i, j = pl.program_id(0), pl.program_id(1)
    tm = xb_ref.shape[0]

    def swiglu_rows(rs):
        xb = xb_ref[rs, :]
        gate = jnp.dot(xb, wg_ref[...].astype(BF16), preferred_element_type=F32)
        up = jnp.dot(xb, wu_ref[...].astype(BF16), preferred_element_type=F32)
        h = (gate * jax.nn.sigmoid(gate) * up).astype(BF16)
        acc_ref[rs, :] += jnp.dot(h, wd_ref[...].astype(BF16), preferred_element_type=F32)

    @pl.when(i < nt_ref[0])
    def _():
        @pl.when(j == 0)
        def _():
            xb_ref[...] = _load_token_tiles(x_ref, tm).astype(BF16)
            acc_ref[...] = jnp.zeros_like(acc_ref)

        sparse_tile = rows_ref[i] <= tm - FFN_SUB

        @pl.when(jnp.logical_not(sparse_tile))
        def _():
            swiglu_rows(slice(None))

        for sb in range(tm // FFN_SUB - 1):
            @pl.when(sparse_tile & (sb * FFN_SUB < rows_ref[i]))
            def _(sb=sb):
                swiglu_rows(slice(sb * FFN_SUB, (sb + 1) * FFN_SUB))

        @pl.when(j == n_chunks - 1)
        def _():
            _store_token_tiles(o_ref, acc_ref[...])

    @pl.when((i >= nt_ref[0]) & (j == n_chunks - 1))
    def _():
        o_ref[...] = jnp.zeros_like(o_ref)


def _grouped_ffn(x, w_gate, w_up, w_down, tile_expert, n_tiles, tile_rows):
    tm, tf = FFN_TM, FFN_TF
    n_chunks = w_gate.shape[-1] // tf
    max_tiles = tile_expert.shape[0]

    def tile_ix(i, nt):
        return jnp.minimum(i, nt[0] - 1)

    def chunk_ix(i, j, nt):
        return jnp.where(i < nt[0], j, n_chunks - 1)

    blk = (tm * SUBLANES, LANES)
    in_specs = [
        pl.BlockSpec(blk, lambda i, j, te, nt, tr: (tile_ix(i, nt), 0)),
        pl.BlockSpec((None, D_MODEL, tf), lambda i, j, te, nt, tr: (te[tile_ix(i, nt)], 0, chunk_ix(i, j, nt))),
        pl.BlockSpec((None, D_MODEL, tf), lambda i, j, te, nt, tr: (te[tile_ix(i, nt)], 0, chunk_ix(i, j, nt))),
        pl.BlockSpec((None, tf, D_MODEL), lambda i, j, te, nt, tr: (te[tile_ix(i, nt)], chunk_ix(i, j, nt), 0)),
    ]
    return pl.pallas_call(
        functools.partial(_ffn_kernel, n_chunks=n_chunks),
        grid_spec=pltpu.PrefetchScalarGridSpec(
            num_scalar_prefetch=3,
            grid=(max_tiles, n_chunks),
            in_specs=in_specs,
            out_specs=pl.BlockSpec(blk, lambda i, j, te, nt, tr: (i, 0)),
            scratch_shapes=[pltpu.VMEM((tm, D_MODEL), BF16), pltpu.VMEM((tm, D_MODEL), F32)],
        ),
        out_shape=jax.ShapeDtypeStruct(x.shape, F32),
        compiler_params=_cparams(("arbitrary", "arbitrary")),
        name="swiglu_grouped",
    )(tile_expert, n_tiles, tile_rows, x, w_gate, w_up, w_down)


ROUTE_TB = 512
META_E1, META_E2, META_R1, META_R2, META_G1, META_G2 = range(6)


def _router_kernel(x_ref, wr_ref, meta_ref, fields_ref, cnt_ref, carry_ref):
    tb = ROUTE_TB

    @pl.when(pl.program_id(0) == 0)
    def _():
        carry_ref[...] = jnp.zeros_like(carry_ref)

    x = _load_token_tiles(x_ref, tb)
    xh, xl = _split_bf16(x)
    wh, wl = _split_bf16(wr_ref[...])
    logits = (jnp.dot(xh, wh, preferred_element_type=F32) + jnp.dot(xh, wl, preferred_element_type=F32)
              + jnp.dot(xl, wh, preferred_element_type=F32))
    lane = lax.broadcasted_iota(jnp.int32, (tb, LANES), 1).astype(F32)
    logits = jnp.where(lane < N_EXPERTS, logits, -jnp.inf)
    m1 = jnp.max(logits, axis=-1, keepdims=True)
    i1 = jnp.min(jnp.where(logits == m1, lane, float(LANES)), axis=-1, keepdims=True)
    rest = jnp.where(lane == i1, -jnp.inf, logits)
    m2 = jnp.max(rest, axis=-1, keepdims=True)
    i2 = jnp.min(jnp.where(rest == m2, lane, float(LANES)), axis=-1, keepdims=True)
    e21 = jnp.exp(m2 - m1)
    g1 = 1.0 / (1.0 + e21)
    g2 = e21 * g1

    member = ((lane == i1) | (lane == i2)).astype(F32)
    r_i = lax.broadcasted_iota(jnp.int32, (tb, tb), 0)
    c_i = lax.broadcasted_iota(jnp.int32, (tb, tb), 1)
    lower = (c_i < r_i).astype(BF16)
    before = jnp.dot(lower, member.astype(BF16), preferred_element_type=F32)
    rank = carry_ref[...] + before
    r1 = jnp.sum(jnp.where(lane == i1, rank, 0.0), axis=-1, keepdims=True)
    r2 = jnp.sum(jnp.where(lane == i2, rank, 0.0), axis=-1, keepdims=True)
    carry_ref[...] += jnp.sum(member, axis=0, keepdims=True)
    cnt_ref[...] = carry_ref[...]

    meta = jnp.zeros((tb, LANES), F32)
    for slot, val in ((META_E1, i1.astype(F32)), (META_E2, i2.astype(F32)), (META_R1, r1), (META_R2, r2),
                      (META_G1, g1), (META_G2, g2)):
        meta = jnp.where(lane == slot, val, meta)
    meta_ref[...] = meta
    fields_ref[...] = meta.T[:SUBLANES]


def _router(xt, w_router):
    n_tok = xt.shape[0] // SUBLANES
    tb = ROUTE_TB
    wr = jnp.zeros((D_MODEL, LANES), F32).at[:, :N_EXPERTS].set(w_router.astype(F32))
    return pl.pallas_call(
        _router_kernel,
        grid=(n_tok // tb,),
        in_specs=[pl.BlockSpec((tb * SUBLANES, LANES), lambda i: (i, 0)),
                  pl.BlockSpec((D_MODEL, LANES), lambda i: (0, 0))],
        out_specs=[pl.BlockSpec((tb, LANES), lambda i: (i, 0)),
                   pl.BlockSpec((SUBLANES, tb), lambda i: (0, i)),
                   pl.BlockSpec((1, LANES), lambda i: (0, 0))],
        out_shape=[jax.ShapeDtypeStruct((n_tok, LANES), F32), jax.ShapeDtypeStruct((SUBLANES, n_tok), F32),
                   jax.ShapeDtypeStruct((1, LANES), F32)],
        scratch_shapes=[pltpu.VMEM((1, LANES), F32)],
        compiler_params=_cparams(("arbitrary",)),
        name="router_top2",
    )(xt, wr)


MOVE_TB = 1024
PAD_CHUNK = 64
DMA_ISSUE_UNROLL = 8


def _tile_rows(ref, idx):
    return ref.at[pl.ds(pl.multiple_of(idx * SUBLANES, SUBLANES), SUBLANES), :]


def _zero_fill(pad_ref, xs_hbm, zbuf, zsem):
    zbuf[...] = jnp.zeros_like(zbuf)
    one = _tile_rows(zbuf, 0)
    for wait in (False, True):
        for seg in range(pad_ref.shape[1]):
            start, n = pad_ref[0, seg], pad_ref[1, seg]
            n_single = n % PAD_CHUNK

            def single(t, c):
                cp = pltpu.make_async_copy(one, _tile_rows(xs_hbm, start + t), zsem)
                cp.wait() if wait else cp.start()
                return c

            def bulk(t, c):
                r0 = pl.multiple_of((start + n_single + t * PAD_CHUNK) * SUBLANES, SUBLANES)
                cp = pltpu.make_async_copy(zbuf, xs_hbm.at[pl.ds(r0, PAD_CHUNK * SUBLANES), :], zsem)
                cp.wait() if wait else cp.start()
                return c

            lax.fori_loop(0, n_single, single, 0)
            lax.fori_loop(0, n // PAD_CHUNK, bulk, 0)


def _dispatch_kernel(dest_ref, pad_ref, x_ref, xs_hbm, zbuf, sem, zsem):
    tb = MOVE_TB

    @pl.when(pl.program_id(0) == 0)
    def _():
        _zero_fill(pad_ref, xs_hbm, zbuf, zsem)

    def body(t, c):
        src = _tile_rows(x_ref, t)
        for k in range(TOP_K):
            pltpu.make_async_copy(src, _tile_rows(xs_hbm, dest_ref[0, 0, TOP_K * t + k]), sem).start(priority=k)
        return c

    lax.fori_loop(0, tb, body, 0, unroll=DMA_ISSUE_UNROLL)
    n_rows = TOP_K * tb * SUBLANES
    pltpu.make_async_copy(xs_hbm.at[pl.ds(0, n_rows), :], xs_hbm.at[pl.ds(0, n_rows), :], sem).wait()


def _dispatch(xt, dest, pads, n_slots):
    n_tok = xt.shape[0] // SUBLANES
    tb = MOVE_TB
    dest3 = dest.reshape(n_tok // tb, 1, TOP_K * tb)
    return pl.pallas_call(
        _dispatch_kernel,
        grid=(n_tok // tb,),
        in_specs=[pl.BlockSpec((1, 1, TOP_K * tb), lambda i: (i, 0, 0), memory_space=pltpu.SMEM),
                  pl.BlockSpec(memory_space=pltpu.SMEM),
                  pl.BlockSpec((tb * SUBLANES, LANES), lambda i: (i, 0))],
        out_specs=pl.BlockSpec(memory_space=pl.ANY),
        out_shape=jax.ShapeDtypeStruct((n_slots * SUBLANES, LANES), F32),
        scratch_shapes=[pltpu.VMEM((PAD_CHUNK * SUBLANES, LANES), F32),
                        pltpu.SemaphoreType.DMA(()), pltpu.SemaphoreType.DMA(())],
        compiler_params=_cparams(("arbitrary",)),
        name="moe_dispatch",
    )(dest3, pads, xt)


COMB_TB = 256


def _combine_kernel(dest_ref, next_ref, y_hbm, x_ref, meta_ref, g_ref, b_ref, o_ref, ybuf, sem, *, alpha, n_steps):
    tb = COMB_TB
    i = pl.program_id(0)
    slot = i % 2

    def gather(d_ref, s):
        def body(t, c):
            for k in range(TOP_K):
                pltpu.make_async_copy(_tile_rows(y_hbm, d_ref[0, 0, TOP_K * t + k]),
                                      _tile_rows(ybuf.at[s], k * tb + t), sem.at[s]).start(priority=k)
            return c
        lax.fori_loop(0, tb, body, 0, unroll=DMA_ISSUE_UNROLL)

    @pl.when(i == 0)
    def _():
        gather(dest_ref, 0)

    @pl.when(i + 1 < n_steps)
    def _():
        gather(next_ref, 1 - slot)

    cur = ybuf.at[slot]
    pltpu.make_async_copy(cur, cur, sem.at[slot]).wait()

    meta = meta_ref[...]
    lane = lax.broadcasted_iota(jnp.int32, meta.shape, 1)
    g1 = jnp.sum(jnp.where(lane == META_G1, meta, 0.0), axis=-1, keepdims=True)
    g2 = jnp.sum(jnp.where(lane == META_G2, meta, 0.0), axis=-1, keepdims=True)
    x = _load_token_tiles(x_ref, tb)
    f = g1 * _load_token_tiles(cur, tb) + g2 * _load_token_tiles(cur, tb, row0=tb)
    o_ref[...] = _layer_norm(alpha * x + f, g_ref[...], b_ref[...])


def _combine(yt, xt, dest, meta, g, b, alpha):
    n_tok = xt.shape[0] // SUBLANES
    tb = COMB_TB
    n_steps = n_tok // tb
    dest3 = dest.reshape(n_steps, 1, TOP_K * tb)
    return pl.pallas_call(
        functools.partial(_combine_kernel, alpha=alpha, n_steps=n_steps),
        grid=(n_steps,),
        in_specs=[pl.BlockSpec((1, 1, TOP_K * tb), lambda i: (i, 0, 0), memory_space=pltpu.SMEM),
                  pl.BlockSpec((1, 1, TOP_K * tb), lambda i: (jnp.minimum(i + 1, n_steps - 1), 0, 0),
                               memory_space=pltpu.SMEM),
                  pl.BlockSpec(memory_space=pl.ANY),
                  pl.BlockSpec((tb * SUBLANES, LANES), lambda i: (i, 0)),
                  pl.BlockSpec((tb, LANES), lambda i: (i, 0)),
                  pl.BlockSpec((1, D_MODEL), lambda i: (0, 0)),
                  pl.BlockSpec((1, D_MODEL), lambda i: (0, 0))],
        out_specs=pl.BlockSpec((tb, D_MODEL), lambda i: (i, 0)),
        out_shape=jax.ShapeDtypeStruct((n_tok, D_MODEL), F32),
        scratch_shapes=[pltpu.VMEM((2, TOP_K * tb * SUBLANES, LANES), F32), pltpu.SemaphoreType.DMA((2,))],
        compiler_params=_cparams(("arbitrary",)),
        name="moe_combine_ln",
    )(dest3, dest3, yt, xt, meta, g[None, :].astype(F32), b[None, :].astype(F32))


class MoePlan(NamedTuple):
    dest: jax.Array
    tile_expert: jax.Array
    tile_rows: jax.Array
    n_tiles: jax.Array
    pads: jax.Array
    n_slots: int


def _moe_plan(fields, counts, n_tok):
    tm = FFN_TM
    max_tiles = n_tok * TOP_K // tm + N_EXPERTS
    cnt = counts[0, :N_EXPERTS].astype(jnp.int32)
    tiles = (cnt + tm - 1) // tm
    tile_end = jnp.cumsum(tiles)
    start = (tile_end - tiles) * tm
    e = fields[META_E1:META_E2 + 1].astype(jnp.int32)
    r = fields[META_R1:META_R2 + 1].astype(jnp.int32)
    dest = (start[e] + r).T.reshape(-1)
    tile_ids = jnp.arange(max_tiles, dtype=jnp.int32)
    tile_expert = jnp.minimum(jnp.sum((tile_ids[:, None] >= tile_end[None, :]).astype(jnp.int32), axis=1),
                              N_EXPERTS - 1)
    first_tile = (tile_end - tiles)[tile_expert]
    tile_rows = jnp.clip(cnt[tile_expert] - (tile_ids - first_tile) * tm, 0, tm).astype(jnp.int32)
    n_tiles = tile_end[-1:]
    pads = jnp.stack([jnp.concatenate([start + cnt, n_tiles * tm]),
                      jnp.concatenate([tiles * tm - cnt, (max_tiles - n_tiles) * tm])]).astype(jnp.int32)
    return MoePlan(dest, tile_expert, tile_rows, n_tiles.astype(jnp.int32), pads, max_tiles * tm)


def kernel(x, w_in, w_out, beta_mix, diff_lambda, diff_norm_g, na_rpb, ln1_g, ln1_b, ln2_g, ln2_b,
           ffn_w_gate, ffn_w_up, ffn_w_down, moe_w_router, moe_w_gate, moe_w_up, moe_w_down):
    B, S, D = x.shape
    assert D == D_MODEL and S % (max(DILATIONS) * BAND_TQ) == 0 and S % PROJ_TM == 0
    depth = w_in.shape[0]
    alpha = (2 * depth) ** 0.25
    n_tok = B * S
    tables = _rope_tables(S)
    x = x.astype(F32)
    for l in range(depth):
        lam_init = 0.8 - 0.6 * math.exp(-0.3 * l)
        moe = l % 2 == 1
        qkv, *res_major = _project(x, w_in[l], tables)
        oa = _attn_a(qkv, diff_lambda[l], diff_norm_g[l], lam_init).reshape(n_tok, WIDTH_A)
        band = [_attn_b(qkv[:, None], DILATIONS[0], BAND_LO // WIDTH_B)]
        band += [_attn_b(src, d, 0) for d, src in zip(DILATIONS[1:], res_major)]
        oc = _attn_c(qkv, na_rpb[l]).reshape(n_tok, WIDTH_C)
        x1 = _outproj(x.reshape(n_tok, D), oa, [o for o, _ in band], [s for _, s in band], oc,
                      beta_mix[l], w_out[l], ln1_g[l], ln1_b[l], alpha, tile_out=moe)
        j = l // 2
        if not moe:
            x = _dense_ffn(x1, ffn_w_gate[j], ffn_w_up[j], ffn_w_down[j], ln2_g[l], ln2_b[l], alpha)
        else:
            meta, fields, counts = _router(x1, moe_w_router[j])
            plan = _moe_plan(fields, counts, n_tok)
            xs = _dispatch(x1, plan.dest, plan.pads, plan.n_slots)
            ys = _grouped_ffn(xs, moe_w_gate[j], moe_w_up[j], moe_w_down[j],
                              plan.tile_expert, plan.n_tiles, plan.tile_rows)
            x = _combine(ys, x1, plan.dest, meta, ln2_g[l], ln2_b[l], alpha)
        x = x.reshape(B, S, D)
    return x
```

```python
import functools
import math
from typing import NamedTuple

import jax
import jax.numpy as jnp
import numpy as np
from jax import lax
from jax.experimental import pallas as pl
from jax.experimental.pallas import tpu as pltpu

F32 = jnp.float32
BF16 = jnp.bfloat16

D_MODEL = 1024
HEAD_DIM = 64
N_HEADS_A, N_HEADS_B, N_HEADS_C = 4, 6, 6
WIDTH_A, WIDTH_B, WIDTH_C = N_HEADS_A * HEAD_DIM, N_HEADS_B * HEAD_DIM, N_HEADS_C * HEAD_DIM
DIFF_DIM = HEAD_DIM // 2
ROPE_THETA = 500000.0
ROPE_FRACTION = 4
DILATIONS = (1, 4, 16)
assert DILATIONS[0] == 1 and DILATIONS[2] == DILATIONS[1] ** 2
BAND_HALF = 64
GRID_W = 64
NA_ROWS = 8
NA_COLS = 16
N_EXPERTS = 8
TOP_K = 2
LN_EPS = 1e-5
NEG = -1e30
LOG2E = math.log2(math.e)
LN2 = math.log(2.0)

LANES = 128
SUBLANES = 8
VMEM_LIMIT = 56 * 1024 * 1024

PROJ_W = 3 * (WIDTH_A + WIDTH_B + WIDTH_C)
ROPE_COLS = 3 * WIDTH_A + 2 * WIDTH_B
COL_CHUNK = 3 * WIDTH_A
BAND_LO = 3 * WIDTH_A
BAND_COLS = 3 * WIDTH_B


def _cparams(sem):
    return pltpu.CompilerParams(dimension_semantics=sem, vmem_limit_bytes=VMEM_LIMIT)


def _layer_norm(z, g, b):
    mu = jnp.mean(z, axis=-1, keepdims=True)
    zc = z - mu
    var = jnp.mean(zc * zc, axis=-1, keepdims=True)
    return zc * lax.rsqrt(var + LN_EPS) * g + b


def _rope_tables(seq):
    rot_a = DIFF_DIM // ROPE_FRACTION
    rot_b = HEAD_DIM // ROPE_FRACTION
    pos = jnp.arange(seq, dtype=F32)[:, None]
    inv_a = ROPE_THETA ** (-jnp.arange(0, rot_a, 2, dtype=F32) / rot_a)
    inv_b = ROPE_THETA ** (-jnp.arange(0, rot_b, 2, dtype=F32) / rot_b)
    ang_a, ang_b = pos * inv_a[None, :], pos * inv_b[None, :]

    cols = np.arange(ROPE_COLS)
    in_a = cols < 3 * WIDTH_A
    sub_a = cols // WIDTH_A
    within_a = cols % DIFF_DIM
    rot_mask_a = in_a & (sub_a < 2) & (within_a < rot_a)
    cb = cols - 3 * WIDTH_A
    sub_b = cb // WIDTH_B
    within_b = cb % HEAD_DIM
    rot_mask_b = (~in_a) & (within_b < rot_b)

    first_half = np.where(in_a, within_a < rot_a // 2, within_b < rot_b // 2)
    rot = rot_mask_a | rot_mask_b
    scale = np.ones(ROPE_COLS, np.float32)
    scale[in_a & (sub_a == 0)] = DIFF_DIM ** -0.5 * LOG2E
    scale[(~in_a) & (sub_b == 0)] = HEAD_DIM ** -0.5 * LOG2E

    def spread(tab_a, tab_b, fill):
        def groups(tab, width, reps):
            pad = jnp.full((seq, width - 2 * tab.shape[1]), fill, F32)
            return jnp.tile(jnp.concatenate([tab, tab, pad], axis=1), (1, reps))
        return jnp.concatenate([groups(tab_a, DIFF_DIM, 2 * WIDTH_A // DIFF_DIM),
                                jnp.full((seq, WIDTH_A), fill, F32),
                                groups(tab_b, HEAD_DIM, 2 * WIDTH_B // HEAD_DIM)], axis=1)

    cm = spread(jnp.cos(ang_a), jnp.cos(ang_b), 1.0) * jnp.asarray(scale)[None, :]
    sn = spread(jnp.sin(ang_a), jnp.sin(ang_b), 0.0) * jnp.asarray(scale)[None, :]
    ma = np.where(rot & first_half, -1.0, 0.0).astype(np.float32)[None, :]
    mb = np.where(rot & ~first_half, 1.0, 0.0).astype(np.float32)[None, :]
    return cm, sn, jnp.asarray(ma), jnp.asarray(mb)


PROJ_TM = 512
W_CAST_ROWS = 128


def _cast_rows_to_bf16(src_ref, dst_ref, rows):
    def body(i, c):
        r = pl.multiple_of(i * W_CAST_ROWS, W_CAST_ROWS)
        dst_ref[pl.ds(r, W_CAST_ROWS), :] = src_ref[pl.ds(r, W_CAST_ROWS), :].astype(BF16)
        return c
    lax.fori_loop(0, rows // W_CAST_ROWS, body, 0)


def _proj_kernel(x_ref, w_ref, cm_ref, sn_ref, ma_ref, mb_ref, o_ref, *rest):
    res_refs, (wbf_ref, band_ref, split_ref) = rest[:-3], rest[-3:]

    @pl.when((pl.program_id(0) == 0) & (pl.program_id(1) == 0))
    def _():
        _cast_rows_to_bf16(w_ref, wbf_ref, D_MODEL)

    xb = x_ref[0].astype(BF16)
    tm = xb.shape[0]
    n_chunks = PROJ_W // COL_CHUNK
    for c in range(n_chunks):
        lo, hi = c * COL_CHUNK, (c + 1) * COL_CHUNK
        acc = jnp.dot(xb, wbf_ref[:, lo:hi], preferred_element_type=F32)
        if hi <= ROPE_COLS:
            half = (DIFF_DIM if c == 0 else HEAD_DIM) // ROPE_FRACTION // 2
            up = pltpu.roll(acc, COL_CHUNK - half, 1)
            dn = pltpu.roll(acc, half, 1)
            acc = acc * cm_ref[:, lo:hi] + (up * ma_ref[:, lo:hi] + dn * mb_ref[:, lo:hi]) * sn_ref[:, lo:hi]
        elif c == 2:
            lane = lax.broadcasted_iota(jnp.int32, (1, COL_CHUNK), 1)
            acc = acc * jnp.where(lane >= WIDTH_B, HEAD_DIM ** -0.5 * LOG2E, 1.0)
        o_ref[0, :, lo:hi] = acc.astype(BF16)
        for col in range(max(lo, BAND_LO), min(hi, BAND_LO + BAND_COLS), LANES):
            band_ref[(col - BAND_LO) // LANES] = acc[:, col - lo:col - lo + LANES]

    (d1, d2), (r1_ref, r2_ref) = DILATIONS[1:], res_refs
    for page in range(BAND_COLS // LANES):
        cs = slice(page * LANES, (page + 1) * LANES)
        for r in range(d1):
            rows = band_ref[page, pl.ds(r, tm // d1, stride=d1), :]
            r1_ref[0, r, :, cs] = rows.astype(BF16)
            split_ref[page, r] = rows
        for r in range(d2):
            r2_ref[0, r, :, cs] = split_ref[page, r % d1, pl.ds(r // d1, tm // d2, stride=d1), :].astype(BF16)


def _project(x, w, layer, tables):
    B, S, _ = x.shape
    cm, sn, ma, mb = tables
    tm = PROJ_TM
    res_specs = [pl.BlockSpec((1, d, tm // d, BAND_COLS), lambda p, b: (b, 0, p, 0)) for d in DILATIONS[1:]]
    res_shapes = [jax.ShapeDtypeStruct((B, d, S // d, BAND_COLS), BF16) for d in DILATIONS[1:]]
    return pl.pallas_call(
        _proj_kernel,
        grid=(S // tm, B),
        in_specs=[
            pl.BlockSpec((1, tm, D_MODEL), lambda p, b: (b, p, 0)),
            pl.BlockSpec((None, D_MODEL, PROJ_W), lambda p, b: (layer, 0, 0), pipeline_mode=pl.Buffered(1)),
            pl.BlockSpec((tm, ROPE_COLS), lambda p, b: (p, 0)),
            pl.BlockSpec((tm, ROPE_COLS), lambda p, b: (p, 0)),
            pl.BlockSpec((1, ROPE_COLS), lambda p, b: (0, 0)),
            pl.BlockSpec((1, ROPE_COLS), lambda p, b: (0, 0)),
        ],
        out_specs=[pl.BlockSpec((1, tm, PROJ_W), lambda p, b: (b, p, 0))] + res_specs,
        out_shape=[jax.ShapeDtypeStruct((B, S, PROJ_W), BF16)] + res_shapes,
        scratch_shapes=[pltpu.VMEM((D_MODEL, PROJ_W), BF16),
                        pltpu.VMEM((BAND_COLS // LANES, tm, LANES), F32),
                        pltpu.VMEM((BAND_COLS // LANES, DILATIONS[1], tm // DILATIONS[1], LANES), F32)],
        compiler_params=_cparams(("arbitrary", "arbitrary")),
        name="proj_in",
    )(x, w, cm, sn, ma, mb)


ATT_A_TQ = 256
ONES_ROWS = 16
ATT_A_UNROLL = 4


def _nt_dot(a, b):
    return lax.dot_general(a, b, (((1,), (1,)), ((), ())), preferred_element_type=F32)


def _with_ones(v):
    return jnp.concatenate([v, jnp.ones_like(v)], axis=1)


def _softmax_times_v(e, v1):
    nd = jnp.dot(e, v1, preferred_element_type=F32)
    return nd[:, :LANES] * (1.0 / nd[:, LANES:])


def _attn_a_kernel(lam_ref, g_ref, q_ref, k_ref, v_ref, o_ref, vt_ref, *, lam_init, seq):
    lp = lam_ref[...]
    lam = (jnp.exp(jnp.sum(lp[0:1] * lp[1:2], axis=-1, keepdims=True))
           - jnp.exp(jnp.sum(lp[2:3] * lp[3:4], axis=-1, keepdims=True)) + lam_init)
    lane = lax.broadcasted_iota(jnp.int32, (1, LANES), 1)
    first = lane < HEAD_DIM
    gain = g_ref[...] * (1.0 - lam_init)
    tq = ATT_A_TQ

    vt = v_ref[0].astype(F32).T.astype(BF16)
    for hh in range(2):
        vt_ref[hh, 0:HEAD_DIM, :] = vt[hh * HEAD_DIM:(hh + 1) * HEAD_DIM]
        vt_ref[hh, HEAD_DIM:, :] = jnp.ones((ONES_ROWS, seq), BF16)

    def qblock(i, carry):
        r0 = pl.multiple_of(i * tq, tq)
        q = q_ref[0, pl.ds(r0, tq), :]
        k = k_ref[0]
        scores = []
        for hh in range(2):
            for m in range(2):
                lo = hh * HEAD_DIM + m * DIFF_DIM
                qm = jnp.where((lane >= lo) & (lane < lo + DIFF_DIM), q, jnp.zeros_like(q))
                scores.append(_nt_dot(qm, k).astype(BF16))
        parts = []
        for idx, s in enumerate(scores):
            e = jnp.exp2(s - jnp.max(s, axis=-1, keepdims=True))
            nd = _nt_dot(vt_ref[idx // 2], e)
            parts.append(nd[:HEAD_DIM] * (1.0 / nd[HEAD_DIM:HEAD_DIM + 1]))
        outs = [parts[0] - lam * parts[1], parts[2] - lam * parts[3]]
        o = jnp.concatenate(outs, axis=0).T
        sq = o * o
        ss0 = jnp.sum(jnp.where(first, sq, 0.0), axis=-1, keepdims=True)
        ss1 = jnp.sum(jnp.where(first, 0.0, sq), axis=-1, keepdims=True)
        ms = jnp.where(first, ss0, ss1) * (1.0 / HEAD_DIM)
        o_ref[0, pl.ds(r0, tq), :] = (o * lax.rsqrt(ms + LN_EPS) * gain).astype(BF16)
        return carry

    lax.fori_loop(0, seq // tq, qblock, 0, unroll=ATT_A_UNROLL)


def _attn_a(qkv, diff_lambda, norm_g, lam_init):
    B, S, _ = qkv.shape
    n_pairs = WIDTH_A // LANES
    g2 = jnp.tile(norm_g.astype(F32), LANES // HEAD_DIM)[None, :]
    blk = lambda off: pl.BlockSpec((1, S, LANES), lambda b, h, off=off: (b, 0, off + h))
    return pl.pallas_call(
        functools.partial(_attn_a_kernel, lam_init=lam_init, seq=S),
        grid=(B, n_pairs),
        in_specs=[
            pl.BlockSpec((4, DIFF_DIM), lambda b, h: (0, 0)),
            pl.BlockSpec((1, LANES), lambda b, h: (0, 0)),
            blk(0), blk(n_pairs), blk(2 * n_pairs),
        ],
        out_specs=pl.BlockSpec((1, S, LANES), lambda b, h: (b, 0, h)),
        out_shape=jax.ShapeDtypeStruct((B, S, WIDTH_A), BF16),
        scratch_shapes=[pltpu.VMEM((2, HEAD_DIM + ONES_ROWS, S), BF16)],
        compiler_params=_cparams(("arbitrary", "arbitrary")),
        name="attn_diff",
    )(diff_lambda.astype(F32), g2, qkv, qkv, qkv)


BAND_TQ = 128
BAND_CHAINS = 12


def _stack_pair(q, lane):
    z = jnp.zeros_like(q)
    return jnp.concatenate([jnp.where(lane < HEAD_DIM, q, z), jnp.where(lane < HEAD_DIM, z, q)], axis=0)


def _attn_b_kernel(q_ref, k_ref, v_ref, o_ref, lse_ref, *, length, dil):
    tq = BAND_TQ
    win = min(tq + 2 * BAND_HALF, length)
    lane = lax.broadcasted_iota(jnp.int32, (1, LANES), 1)

    blocks = [(i * tq, min(max(i * tq - BAND_HALF, 0), length - win)) for i in range(length // tq)]
    row = lax.broadcasted_iota(jnp.int32, (2 * tq, win), 0)
    delta = lax.broadcasted_iota(jnp.int32, (2 * tq, win), 1) - jnp.where(row >= tq, row - tq, row)
    masks = {off: jnp.where(jnp.abs(delta - off) <= BAND_HALF, 0.0, NEG) for off in {q0 - k0 for q0, k0 in blocks}}

    def residue(r, carry):
        for q0, k0 in blocks:
            lse_tile = jnp.zeros((tq, LANES), F32)
            for pr in range(WIDTH_B // LANES):
                cs = slice(pr * LANES, (pr + 1) * LANES)
                q2 = _stack_pair(q_ref[0, r, q0:q0 + tq, cs], lane)
                s = _nt_dot(q2, k_ref[0, r, k0:k0 + win, cs]) + masks[q0 - k0]
                mx = jnp.max(s, axis=-1, keepdims=True)
                e = jnp.exp2(s - mx).astype(BF16)
                nd = jnp.dot(e, _with_ones(v_ref[0, r, k0:k0 + win, cs]), preferred_element_type=F32)
                l = nd[:, LANES:]
                o2 = nd[:, :LANES] * (1.0 / l)
                o_ref[0, r, q0:q0 + tq, cs] = jnp.where(lane < HEAD_DIM, o2[:tq], o2[tq:]).astype(BF16)
                lse2 = (mx + jnp.log2(l)) * LN2
                lse_tile = jnp.where(lane == 2 * pr, lse2[:tq], lse_tile)
                lse_tile = jnp.where(lane == 2 * pr + 1, lse2[tq:], lse_tile)
            lse_ref[0, r, q0:q0 + tq, :] = lse_tile
        return carry

    chains = len(blocks) * (WIDTH_B // LANES)
    lax.fori_loop(0, dil, residue, 0, unroll=min(dil, max(1, BAND_CHAINS // chains)))


def _attn_b(src, dil, first_blk):
    B, _, L, _ = src.shape
    blk = lambda off: pl.BlockSpec((1, dil, L, WIDTH_B), lambda b, off=off: (b, 0, 0, first_blk + off))
    return pl.pallas_call(
        functools.partial(_attn_b_kernel, length=L, dil=dil),
        grid=(B,),
        in_specs=[blk(0), blk(1), blk(2)],
        out_specs=[pl.BlockSpec((1, dil, L, WIDTH_B), lambda b: (b, 0, 0, 0)),
                   pl.BlockSpec((1, dil, L, LANES), lambda b: (b, 0, 0, 0))],
        out_shape=[jax.ShapeDtypeStruct((B, dil, L, WIDTH_B), BF16),
                   jax.ShapeDtypeStruct((B, dil, L, LANES), F32)],
        compiler_params=_cparams(("arbitrary",)),
        name=f"attn_band_d{dil}",
    )(src, src, src)


N_RPB_ROWS = 2 * NA_ROWS - 1
N_RPB_COLS = 2 * NA_COLS - 1
NA_KEYS = NA_ROWS * GRID_W
N_BIAS_VARIANTS = NA_ROWS
NA_ROWS_PER_STEP = 4


def _attn_c_kernel(u_ref, q_ref, k_ref, v_ref, o_ref, bias_ref, *, rows):
    lane = lax.broadcasted_iota(jnp.int32, (1, LANES), 1)
    n_pairs = WIDTH_C // LANES

    @pl.when(pl.program_id(0) == 0)
    def _():
        wide = N_RPB_ROWS * GRID_W
        c = lax.broadcasted_iota(jnp.int32, (GRID_W, wide), 0)
        kc = lax.broadcasted_iota(jnp.int32, (GRID_W, wide), 1) % GRID_W
        tidx = jnp.clip(kc - c + NA_COLS - 1, 0, N_RPB_COLS - 1)
        c5 = lax.broadcasted_iota(jnp.int32, (GRID_W, NA_KEYS), 0)
        kc5 = lax.broadcasted_iota(jnp.int32, (GRID_W, NA_KEYS), 1) % GRID_W
        cstart = jnp.clip(c5 - NA_COLS // 2, 0, GRID_W - NA_COLS)
        valid = (kc5 >= cstart) & (kc5 < cstart + NA_COLS)
        for h in range(N_HEADS_C):
            def tbody(t, acc, h=h):
                row = u_ref[h, pl.ds(t, 1), :]
                return jnp.where(tidx == t, row, acc)
            toep = lax.fori_loop(0, N_RPB_COLS, tbody, jnp.zeros((GRID_W, wide), F32))
            for var in range(N_BIAS_VARIANTS):
                tile = jnp.where(valid, toep[:, var * GRID_W:var * GRID_W + NA_KEYS] * LOG2E, NEG)
                bias_ref[var, h // 2, (h % 2) * GRID_W:(h % 2 + 1) * GRID_W, :] = tile

    def rbody(step, carry):
        for sub in range(NA_ROWS_PER_STEP):
            r = step * NA_ROWS_PER_STEP + sub
            rs = jnp.clip(r - NA_ROWS // 2, 0, rows - NA_ROWS)
            var = rs - r + NA_ROWS - 1
            q0 = pl.multiple_of(r * GRID_W, GRID_W)
            k0 = pl.multiple_of(rs * GRID_W, GRID_W)
            for pr in range(n_pairs):
                cs = slice(pr * LANES, (pr + 1) * LANES)
                q2 = _stack_pair(q_ref[0, pl.ds(q0, GRID_W), cs], lane)
                s = _nt_dot(q2, k_ref[0, pl.ds(k0, NA_KEYS), cs]) + bias_ref[var, pr]
                e = jnp.exp2(s - jnp.max(s, axis=-1, keepdims=True)).astype(BF16)
                o2 = _softmax_times_v(e, _with_ones(v_ref[0, pl.ds(k0, NA_KEYS), cs]))
                o_ref[0, pl.ds(q0, GRID_W), cs] = jnp.where(lane < HEAD_DIM, o2[:GRID_W], o2[GRID_W:]).astype(BF16)
        return carry

    lax.fori_loop(0, rows // NA_ROWS_PER_STEP, rbody, 0)


def _attn_c(qkv, rpb):
    B, S, _ = qkv.shape
    rows = S // GRID_W
    assert rows >= NA_ROWS
    u = jnp.repeat(jnp.transpose(rpb.astype(F32), (0, 2, 1)), GRID_W, axis=2)
    q_blk = (3 * WIDTH_A + 3 * WIDTH_B) // WIDTH_C
    blk = lambda off: pl.BlockSpec((1, S, WIDTH_C), lambda b, off=off: (b, 0, off))
    return pl.pallas_call(
        functools.partial(_attn_c_kernel, rows=rows),
        grid=(B,),
        in_specs=[pl.BlockSpec((N_HEADS_C, N_RPB_COLS, N_RPB_ROWS * GRID_W), lambda b: (0, 0, 0)),
                  blk(q_blk), blk(q_blk + 1), blk(q_blk + 2)],
        out_specs=pl.BlockSpec((1, S, WIDTH_C), lambda b: (b, 0, 0)),
        out_shape=jax.ShapeDtypeStruct((B, S, WIDTH_C), BF16),
        scratch_shapes=[pltpu.VMEM((N_BIAS_VARIANTS, WIDTH_C // LANES, LANES, NA_KEYS), F32)],
        compiler_params=_cparams(("arbitrary",)),
        name="attn_nbr",
    )(u, qkv, qkv, qkv)


OUT_TM = 512
OUT_SPLIT = 2


def _split_bf16(x):
    hi = x.astype(BF16)
    return hi, (x - hi.astype(F32)).astype(BF16)


def _outproj_kernel(x_ref, oa_ref, ob0_ref, ob1_ref, ob2_ref, l0_ref, l1_ref, l2_ref, oc_ref,
                    beta_ref, w_ref, g_ref, b_ref, e_ref, o_ref, wbf_ref, il_ref, *, alpha, tile_out):
    @pl.when(pl.program_id(0) == 0)
    def _():
        _cast_rows_to_bf16(w_ref, wbf_ref, D_MODEL)

    tm = x_ref.shape[0]
    n_pages = WIDTH_B // LANES
    for slot, (dil, ob_r, l_r) in enumerate(zip(DILATIONS[1:], (ob1_ref, ob2_ref), (l1_ref, l2_ref))):
        for r in range(dil):
            rows = pl.ds(r, tm // dil, stride=dil)
            for page in range(n_pages):
                il_ref[slot, page, rows, :] = ob_r[0, r, :, page * LANES:(page + 1) * LANES].astype(F32)
            il_ref[slot, n_pages, rows, :] = l_r[0, r]

    beta = beta_ref[...]
    half = tm // OUT_SPLIT
    for part in range(OUT_SPLIT):
        rs = slice(part * half, (part + 1) * half)
        l0, l1, l2 = l0_ref[0, 0, rs, :], il_ref[0, n_pages, rs, :], il_ref[1, n_pages, rs, :]
        lm = jnp.maximum(jnp.maximum(l0, l1), l2)
        e0, e1, e2 = jnp.exp(l0 - lm), jnp.exp(l1 - lm), jnp.exp(l2 - lm)
        inv = 1.0 / (e0 + e1 + e2)
        wexp = [jnp.dot(jnp.concatenate(_split_bf16(e * inv), axis=1), e_ref[...], preferred_element_type=F32)
                for e in (e0, e1, e2)]
        pages = []
        for page in range(n_pages):
            cs = slice(page * LANES, (page + 1) * LANES)
            branch_vals = (ob0_ref[0, 0, rs, cs].astype(F32), il_ref[0, page, rs, :], il_ref[1, page, rs, :])
            pages.append(sum(w[:, cs] * val for w, val in zip(wexp, branch_vals)))
        ob = jnp.concatenate(pages, axis=1)

        ma = (oa_ref[rs, :].astype(F32) * beta[:, :WIDTH_A]).astype(BF16)
        mb = (ob * beta[:, WIDTH_A:WIDTH_A + WIDTH_B]).astype(BF16)
        mc = (oc_ref[rs, :].astype(F32) * beta[:, WIDTH_A + WIDTH_B:]).astype(BF16)
        acc = jnp.dot(jnp.concatenate([ma, mb, mc], axis=1), wbf_ref[...], preferred_element_type=F32)
        y = _layer_norm(alpha * x_ref[rs, :] + acc, g_ref[...], b_ref[...])
        if tile_out:
            _store_token_tiles(o_ref, y, row0=part * half)
        else:
            o_ref[rs, :] = y


def _store_token_tiles(ref, y, row0=0):
    rows = y.shape[0]
    for j in range(D_MODEL // LANES):
        ref[pl.ds(row0 * SUBLANES + j, rows, stride=SUBLANES), :] = y[:, j * LANES:(j + 1) * LANES]


def _load_token_tiles(ref, rows, row0=0):
    return jnp.concatenate(
        [ref[pl.ds(row0 * SUBLANES + j, rows, stride=SUBLANES), :] for j in range(D_MODEL // LANES)], axis=1)


def _outproj(x2, oa, obs, lses, oc, beta, w, layer, g, b, alpha, tile_out):
    N = x2.shape[0]
    tm = OUT_TM
    steps_per_seq = obs[0].shape[2] // tm
    expand = np.zeros((2 * LANES, WIDTH_B), np.float32)
    for h in range(N_HEADS_B):
        expand[h, h * HEAD_DIM:(h + 1) * HEAD_DIM] = 1.0
        expand[LANES + h, h * HEAD_DIM:(h + 1) * HEAD_DIM] = 1.0
    row = lambda w_: pl.BlockSpec((tm, w_), lambda i: (i, 0))
    full = lambda r, c, **kw: pl.BlockSpec((r, c), lambda i: (0, 0), **kw)
    res = lambda d, w_: pl.BlockSpec((1, d, tm // d, w_),
                                     lambda i: (i // steps_per_seq, 0, i % steps_per_seq, 0))
    if tile_out:
        out_spec = pl.BlockSpec((tm * SUBLANES, LANES), lambda i: (i, 0))
        out_shape = jax.ShapeDtypeStruct((N * SUBLANES, LANES), F32)
    else:
        out_spec = row(D_MODEL)
        out_shape = jax.ShapeDtypeStruct((N, D_MODEL), F32)
    return pl.pallas_call(
        functools.partial(_outproj_kernel, alpha=alpha, tile_out=tile_out),
        grid=(N // tm,),
        in_specs=[row(D_MODEL), row(WIDTH_A)] + [res(d, WIDTH_B) for d in DILATIONS]
                 + [res(d, LANES) for d in DILATIONS] + [row(WIDTH_C),
                  full(1, D_MODEL),
                  pl.BlockSpec((None, D_MODEL, D_MODEL), lambda i: (layer, 0, 0), pipeline_mode=pl.Buffered(1)),
                  full(1, D_MODEL), full(1, D_MODEL), full(2 * LANES, WIDTH_B)],
        out_specs=out_spec,
        out_shape=out_shape,
        scratch_shapes=[pltpu.VMEM((D_MODEL, D_MODEL), BF16),
                        pltpu.VMEM((len(DILATIONS) - 1, WIDTH_B // LANES + 1, tm, LANES), F32)],
        compiler_params=_cparams(("arbitrary",)),
        name="merge_outproj_ln",
    )(x2, oa, obs[0], obs[1], obs[2], lses[0], lses[1], lses[2], oc,
      beta[None, :].astype(F32), w, g[None, :].astype(F32), b[None, :].astype(F32),
      jnp.asarray(expand, BF16))


CAST_ROWS = 256
DENSE_TM = 512
DENSE_TF = 256


def _cast_kernel(w_ref, o_ref):
    o_ref[...] = w_ref[...].astype(BF16)


def _to_bf16(w):
    rows, cols = w.shape
    return pl.pallas_call(
        _cast_kernel,
        grid=(rows // CAST_ROWS,),
        in_specs=[pl.BlockSpec((CAST_ROWS, cols), lambda i: (i, 0))],
        out_specs=pl.BlockSpec((CAST_ROWS, cols), lambda i: (i, 0)),
        out_shape=jax.ShapeDtypeStruct((rows, cols), BF16),
        compiler_params=_cparams(("arbitrary",)),
        name="cast_bf16",
    )(w)


def _dense_ffn_kernel(x_ref, wg_ref, wu_ref, wd_ref, g_ref, b_ref, o_ref, h_ref, *, alpha):
    x = x_ref[...]
    xb = x.astype(BF16)
    for c in range(h_ref.shape[1] // DENSE_TF):
        cs = slice(c * DENSE_TF, (c + 1) * DENSE_TF)
        gate = jnp.dot(xb, wg_ref[:, cs], preferred_element_type=F32)
        up = jnp.dot(xb, wu_ref[:, cs], preferred_element_type=F32)
        h_ref[:, cs] = (gate * jax.nn.sigmoid(gate) * up).astype(BF16)
    y = jnp.dot(h_ref[...], wd_ref[...], preferred_element_type=F32)
    o_ref[...] = _layer_norm(alpha * x + y, g_ref[...], b_ref[...])


def _dense_ffn(x, w_gate, w_up, w_down, g, b, alpha):
    n_tok = x.shape[0]
    ff = w_gate.shape[1]
    tm = DENSE_TM
    row = pl.BlockSpec((tm, D_MODEL), lambda i: (i, 0))
    resident = lambda r, c: pl.BlockSpec((r, c), lambda i: (0, 0), pipeline_mode=pl.Buffered(1))
    return pl.pallas_call(
        functools.partial(_dense_ffn_kernel, alpha=alpha),
        grid=(n_tok // tm,),
        in_specs=[row, resident(D_MODEL, ff), resident(D_MODEL, ff), resident(ff, D_MODEL),
                  pl.BlockSpec((1, D_MODEL), lambda i: (0, 0)), pl.BlockSpec((1, D_MODEL), lambda i: (0, 0))],
        out_specs=row,
        out_shape=jax.ShapeDtypeStruct((n_tok, D_MODEL), F32),
        scratch_shapes=[pltpu.VMEM((tm, ff), BF16)],
        compiler_params=_cparams(("arbitrary",)),
        name="swiglu_dense",
    )(x, _to_bf16(w_gate), _to_bf16(w_up), _to_bf16(w_down), g[None, :].astype(F32), b[None, :].astype(F32))


FFN_TM = 1024
FFN_TF = 512
FFN_SUB = 256


def _ffn_kernel(te_ref, nt_ref, rows_ref, x_ref, wg_ref, wu_ref, wd_ref, o_ref, xb_ref, acc_ref, *, n_chunks):
    i, j = pl.program_id(0), pl.program_id(1)
    tm = xb_ref.shape[0]

    def swiglu_rows(rs):
        xb = xb_ref[rs, :]
        gate = jnp.dot(xb, wg_ref[...].astype(BF16), preferred_element_type=F32)
        up = jnp.dot(xb, wu_ref[...].astype(BF16), preferred_element_type=F32)
        h = (gate * jax.nn.sigmoid(gate) * up).astype(BF16)
        acc_ref[rs, :] += jnp.dot(h, wd_ref[...].astype(BF16), preferred_element_type=F32)

    @pl.when(i < nt_ref[0])
    def _():
        @pl.when(j == 0)
        def _():
            xb_ref[...] = _load_token_tiles(x_ref, tm).astype(BF16)
            acc_ref[...] = jnp.zeros_like(acc_ref)

        sparse_tile = rows_ref[i] <= tm - FFN_SUB

        @pl.when(jnp.logical_not(sparse_tile))
        def _():
            swiglu_rows(slice(None))

        for sb in range(tm // FFN_SUB - 1):
            @pl.when(sparse_tile & (sb * FFN_SUB < rows_ref[i]))
            def _(sb=sb):
                swiglu_rows(slice(sb * FFN_SUB, (sb + 1) * FFN_SUB))

        @pl.when(j == n_chunks - 1)
        def _():
            _store_token_tiles(o_ref, acc_ref[...])

    @pl.when((i >= nt_ref[0]) & (j == n_chunks - 1))
    def _():
        o_ref[...] = jnp.zeros_like(o_ref)


def _grouped_ffn(x, w_gate, w_up, w_down, tile_expert, n_tiles, tile_rows):
    tm, tf = FFN_TM, FFN_TF
    n_chunks = w_gate.shape[-1] // tf
    max_tiles = tile_expert.shape[0]

    def tile_ix(i, nt):
        return jnp.minimum(i, nt[0] - 1)

    def chunk_ix(i, j, nt):
        return jnp.where(i < nt[0], j, n_chunks - 1)

    blk = (tm * SUBLANES, LANES)
    in_specs = [
        pl.BlockSpec(blk, lambda i, j, te, nt, tr: (tile_ix(i, nt), 0)),
        pl.BlockSpec((None, D_MODEL, tf), lambda i, j, te, nt, tr: (te[tile_ix(i, nt)], 0, chunk_ix(i, j, nt))),
        pl.BlockSpec((None, D_MODEL, tf), lambda i, j, te, nt, tr: (te[tile_ix(i, nt)], 0, chunk_ix(i, j, nt))),
        pl.BlockSpec((None, tf, D_MODEL), lambda i, j, te, nt, tr: (te[tile_ix(i, nt)], chunk_ix(i, j, nt), 0)),
    ]
    return pl.pallas_call(
        functools.partial(_ffn_kernel, n_chunks=n_chunks),
        grid_spec=pltpu.PrefetchScalarGridSpec(
            num_scalar_prefetch=3,
            grid=(max_tiles, n_chunks),
            in_specs=in_specs,
            out_specs=pl.BlockSpec(blk, lambda i, j, te, nt, tr: (i, 0)),
            scratch_shapes=[pltpu.VMEM((tm, D_MODEL), BF16), pltpu.VMEM((tm, D_MODEL), F32)],
        ),
        out_shape=jax.ShapeDtypeStruct(x.shape, F32),
        compiler_params=_cparams(("arbitrary", "arbitrary")),
        name="swiglu_grouped",
    )(tile_expert, n_tiles, tile_rows, x, w_gate, w_up, w_down)


ROUTE_TB = 512
META_E1, META_E2, META_R1, META_R2, META_G1, META_G2 = range(6)


def _router_kernel(x_ref, wr_ref, meta_ref, fields_ref, cnt_ref, carry_ref):
    tb = ROUTE_TB

    @pl.when(pl.program_id(0) == 0)
    def _():
        carry_ref[...] = jnp.zeros_like(carry_ref)

    x = _load_token_tiles(x_ref, tb)
    xh, xl = _split_bf16(x)
    wh, wl = _split_bf16(wr_ref[...])
    logits = (jnp.dot(xh, wh, preferred_element_type=F32) + jnp.dot(xh, wl, preferred_element_type=F32)
              + jnp.dot(xl, wh, preferred_element_type=F32))
    lane = lax.broadcasted_iota(jnp.int32, (tb, LANES), 1).astype(F32)
    logits = jnp.where(lane < N_EXPERTS, logits, -jnp.inf)
    m1 = jnp.max(logits, axis=-1, keepdims=True)
    i1 = jnp.min(jnp.where(logits == m1, lane, float(LANES)), axis=-1, keepdims=True)
    rest = jnp.where(lane == i1, -jnp.inf, logits)
    m2 = jnp.max(rest, axis=-1, keepdims=True)
    i2 = jnp.min(jnp.where(rest == m2, lane, float(LANES)), axis=-1, keepdims=True)
    e21 = jnp.exp(m2 - m1)
    g1 = 1.0 / (1.0 + e21)
    g2 = e21 * g1

    member = ((lane == i1) | (lane == i2)).astype(F32)
    r_i = lax.broadcasted_iota(jnp.int32, (tb, tb), 0)
    c_i = lax.broadcasted_iota(jnp.int32, (tb, tb), 1)
    lower = (c_i < r_i).astype(BF16)
    before = jnp.dot(lower, member.astype(BF16), preferred_element_type=F32)
    rank = carry_ref[...] + before
    r1 = jnp.sum(jnp.where(lane == i1, rank, 0.0), axis=-1, keepdims=True)
    r2 = jnp.sum(jnp.where(lane == i2, rank, 0.0), axis=-1, keepdims=True)
    carry_ref[...] += jnp.sum(member, axis=0, keepdims=True)
    cnt_ref[...] = carry_ref[...]

    meta = jnp.zeros((tb, LANES), F32)
    for slot, val in ((META_E1, i1.astype(F32)), (META_E2, i2.astype(F32)), (META_R1, r1), (META_R2, r2),
                      (META_G1, g1), (META_G2, g2)):
        meta = jnp.where(lane == slot, val, meta)
    meta_ref[...] = meta
    fields_ref[...] = meta.T[:SUBLANES]


def _router(xt, w_router):
    n_tok = xt.shape[0] // SUBLANES
    tb = ROUTE_TB
    wr = jnp.zeros((D_MODEL, LANES), F32).at[:, :N_EXPERTS].set(w_router.astype(F32))
    return pl.pallas_call(
        _router_kernel,
        grid=(n_tok // tb,),
        in_specs=[pl.BlockSpec((tb * SUBLANES, LANES), lambda i: (i, 0)),
                  pl.BlockSpec((D_MODEL, LANES), lambda i: (0, 0))],
        out_specs=[pl.BlockSpec((tb, LANES), lambda i: (i, 0)),
                   pl.BlockSpec((SUBLANES, tb), lambda i: (0, i)),
                   pl.BlockSpec((1, LANES), lambda i: (0, 0))],
        out_shape=[jax.ShapeDtypeStruct((n_tok, LANES), F32), jax.ShapeDtypeStruct((SUBLANES, n_tok), F32),
                   jax.ShapeDtypeStruct((1, LANES), F32)],
        scratch_shapes=[pltpu.VMEM((1, LANES), F32)],
        compiler_params=_cparams(("arbitrary",)),
        name="router_top2",
    )(xt, wr)


MOVE_TB = 1024
PAD_CHUNK = 64
DMA_ISSUE_UNROLL = 8


def _tile_rows(ref, idx):
    return ref.at[pl.ds(pl.multiple_of(idx * SUBLANES, SUBLANES), SUBLANES), :]


def _zero_fill(pad_ref, xs_hbm, zbuf, zsem):
    zbuf[...] = jnp.zeros_like(zbuf)
    one = _tile_rows(zbuf, 0)
    for wait in (False, True):
        for seg in range(pad_ref.shape[1]):
            start, n = pad_ref[0, seg], pad_ref[1, seg]
            n_single = n % PAD_CHUNK

            def single(t, c):
                cp = pltpu.make_async_copy(one, _tile_rows(xs_hbm, start + t), zsem)
                cp.wait() if wait else cp.start()
                return c

            def bulk(t, c):
                r0 = pl.multiple_of((start + n_single + t * PAD_CHUNK) * SUBLANES, SUBLANES)
                cp = pltpu.make_async_copy(zbuf, xs_hbm.at[pl.ds(r0, PAD_CHUNK * SUBLANES), :], zsem)
                cp.wait() if wait else cp.start()
                return c

            lax.fori_loop(0, n_single, single, 0)
            lax.fori_loop(0, n // PAD_CHUNK, bulk, 0)


def _dispatch_kernel(dest_ref, pad_ref, x_ref, xs_hbm, zbuf, sem, zsem):
    tb = MOVE_TB

    @pl.when(pl.program_id(0) == 0)
    def _():
        _zero_fill(pad_ref, xs_hbm, zbuf, zsem)

    def body(t, c):
        src = _tile_rows(x_ref, t)
        for k in range(TOP_K):
            pltpu.make_async_copy(src, _tile_rows(xs_hbm, dest_ref[0, 0, TOP_K * t + k]), sem).start(priority=k)
        return c

    lax.fori_loop(0, tb, body, 0, unroll=DMA_ISSUE_UNROLL)
    n_rows = TOP_K * tb * SUBLANES
    pltpu.make_async_copy(xs_hbm.at[pl.ds(0, n_rows), :], xs_hbm.at[pl.ds(0, n_rows), :], sem).wait()


def _dispatch(xt, dest, pads, n_slots):
    n_tok = xt.shape[0] // SUBLANES
    tb = MOVE_TB
    dest3 = dest.reshape(n_tok // tb, 1, TOP_K * tb)
    return pl.pallas_call(
        _dispatch_kernel,
        grid=(n_tok // tb,),
        in_specs=[pl.BlockSpec((1, 1, TOP_K * tb), lambda i: (i, 0, 0), memory_space=pltpu.SMEM),
                  pl.BlockSpec(memory_space=pltpu.SMEM),
                  pl.BlockSpec((tb * SUBLANES, LANES), lambda i: (i, 0))],
        out_specs=pl.BlockSpec(memory_space=pl.ANY),
        out_shape=jax.ShapeDtypeStruct((n_slots * SUBLANES, LANES), F32),
        scratch_shapes=[pltpu.VMEM((PAD_CHUNK * SUBLANES, LANES), F32),
                        pltpu.SemaphoreType.DMA(()), pltpu.SemaphoreType.DMA(())],
        compiler_params=_cparams(("arbitrary",)),
        name="moe_dispatch",
    )(dest3, pads, xt)


COMB_TB = 256


def _combine_kernel(dest_ref, next_ref, y_hbm, x_ref, meta_ref, g_ref, b_ref, o_ref, ybuf, sem, *, alpha, n_steps):
    tb = COMB_TB
    i = pl.program_id(0)
    slot = i % 2

    def gather(d_ref, s):
        def body(t, c):
            for k in range(TOP_K):
                pltpu.make_async_copy(_tile_rows(y_hbm, d_ref[0, 0, TOP_K * t + k]),
                                      _tile_rows(ybuf.at[s], k * tb + t), sem.at[s]).start(priority=k)
            return c
        lax.fori_loop(0, tb, body, 0, unroll=DMA_ISSUE_UNROLL)

    @pl.when(i == 0)
    def _():
        gather(dest_ref, 0)

    @pl.when(i + 1 < n_steps)
    def _():
        gather(next_ref, 1 - slot)

    cur = ybuf.at[slot]
    pltpu.make_async_copy(cur, cur, sem.at[slot]).wait()

    meta = meta_ref[...]
    lane = lax.broadcasted_iota(jnp.int32, meta.shape, 1)
    g1 = jnp.sum(jnp.where(lane == META_G1, meta, 0.0), axis=-1, keepdims=True)
    g2 = jnp.sum(jnp.where(lane == META_G2, meta, 0.0), axis=-1, keepdims=True)
    x = _load_token_tiles(x_ref, tb)
    f = g1 * _load_token_tiles(cur, tb) + g2 * _load_token_tiles(cur, tb, row0=tb)
    o_ref[...] = _layer_norm(alpha * x + f, g_ref[...], b_ref[...])


def _combine(yt, xt, dest, meta, g, b, alpha):
    n_tok = xt.shape[0] // SUBLANES
    tb = COMB_TB
    n_steps = n_tok // tb
    dest3 = dest.reshape(n_steps, 1, TOP_K * tb)
    return pl.pallas_call(
        functools.partial(_combine_kernel, alpha=alpha, n_steps=n_steps),
        grid=(n_steps,),
        in_specs=[pl.BlockSpec((1, 1, TOP_K * tb), lambda i: (i, 0, 0), memory_space=pltpu.SMEM),
                  pl.BlockSpec((1, 1, TOP_K * tb), lambda i: (jnp.minimum(i + 1, n_steps - 1), 0, 0),
                               memory_space=pltpu.SMEM),
                  pl.BlockSpec(memory_space=pl.ANY),
                  pl.BlockSpec((tb * SUBLANES, LANES), lambda i: (i, 0)),
                  pl.BlockSpec((tb, LANES), lambda i: (i, 0)),
                  pl.BlockSpec((1, D_MODEL), lambda i: (0, 0)),
                  pl.BlockSpec((1, D_MODEL), lambda i: (0, 0))],
        out_specs=pl.BlockSpec((tb, D_MODEL), lambda i: (i, 0)),
        out_shape=jax.ShapeDtypeStruct((n_tok, D_MODEL), F32),
        scratch_shapes=[pltpu.VMEM((2, TOP_K * tb * SUBLANES, LANES), F32), pltpu.SemaphoreType.DMA((2,))],
        compiler_params=_cparams(("arbitrary",)),
        name="moe_combine_ln",
    )(dest3, dest3, yt, xt, meta, g[None, :].astype(F32), b[None, :].astype(F32))


class MoePlan(NamedTuple):
    dest: jax.Array
    tile_expert: jax.Array
    tile_rows: jax.Array
    n_tiles: jax.Array
    pads: jax.Array
    n_slots: int


def _moe_plan(fields, counts, n_tok):
    tm = FFN_TM
    max_tiles = n_tok * TOP_K // tm + N_EXPERTS
    cnt = counts[0, :N_EXPERTS].astype(jnp.int32)
    tiles = (cnt + tm - 1) // tm
    tile_end = jnp.cumsum(tiles)
    start = (tile_end - tiles) * tm
    e = fields[META_E1:META_E2 + 1].astype(jnp.int32)
    r = fields[META_R1:META_R2 + 1].astype(jnp.int32)
    experts = jnp.arange(N_EXPERTS, dtype=jnp.int32)[:, None, None]
    group_start = jnp.sum(jnp.where(e[None] == experts, start[:, None, None], 0), axis=0)
    dest = (group_start + r).T.reshape(-1)
    tile_ids = jnp.arange(max_tiles, dtype=jnp.int32)
    tile_expert = jnp.minimum(jnp.sum((tile_ids[:, None] >= tile_end[None, :]).astype(jnp.int32), axis=1),
                              N_EXPERTS - 1)
    first_tile = (tile_end - tiles)[tile_expert]
    tile_rows = jnp.clip(cnt[tile_expert] - (tile_ids - first_tile) * tm, 0, tm).astype(jnp.int32)
    n_tiles = tile_end[-1:]
    pads = jnp.stack([jnp.concatenate([start + cnt, n_tiles * tm]),
                      jnp.concatenate([tiles * tm - cnt, (max_tiles - n_tiles) * tm])]).astype(jnp.int32)
    return MoePlan(dest, tile_expert, tile_rows, n_tiles.astype(jnp.int32), pads, max_tiles * tm)


def kernel(x, w_in, w_out, beta_mix, diff_lambda, diff_norm_g, na_rpb, ln1_g, ln1_b, ln2_g, ln2_b,
           ffn_w_gate, ffn_w_up, ffn_w_down, moe_w_router, moe_w_gate, moe_w_up, moe_w_down):
    B, S, D = x.shape
    assert D == D_MODEL and S % (max(DILATIONS) * BAND_TQ) == 0 and S % PROJ_TM == 0
    depth = w_in.shape[0]
    alpha = (2 * depth) ** 0.25
    n_tok = B * S
    tables = _rope_tables(S)
    x = x.astype(F32)
    for l in range(depth):
        lam_init = 0.8 - 0.6 * math.exp(-0.3 * l)
        moe = l % 2 == 1
        qkv, *res_major = _project(x, w_in, l, tables)
        oa = _attn_a(qkv, diff_lambda[l], diff_norm_g[l], lam_init).reshape(n_tok, WIDTH_A)
        band = [_attn_b(qkv[:, None], DILATIONS[0], BAND_LO // WIDTH_B)]
        band += [_attn_b(src, d, 0) for d, src in zip(DILATIONS[1:], res_major)]
        oc = _attn_c(qkv, na_rpb[l]).reshape(n_tok, WIDTH_C)
        x1 = _outproj(x.reshape(n_tok, D), oa, [o for o, _ in band], [s for _, s in band], oc,
                      beta_mix[l], w_out, l, ln1_g[l], ln1_b[l], alpha, tile_out=moe)
        j = l // 2
        if not moe:
            x = _dense_ffn(x1, ffn_w_gate[j], ffn_w_up[j], ffn_w_down[j], ln2_g[l], ln2_b[l], alpha)
        else:
            meta, fields, counts = _router(x1, moe_w_router[j])
            plan = _moe_plan(fields, counts, n_tok)
            xs = _dispatch(x1, plan.dest, plan.pads, plan.n_slots)
            ys = _grouped_ffn(xs, moe_w_gate[j], moe_w_up[j], moe_w_down[j],
                              plan.tile_expert, plan.n_tiles, plan.tile_rows)
            x = _combine(ys, x1, plan.dest, meta, ln2_g[l], ln2_b[l], alpha)
        x = x.reshape(B, S, D)
    return x
```

```python
import functools
import math
from typing import NamedTuple

import jax
import jax.numpy as jnp
import numpy as np
from jax import lax
from jax.experimental import pallas as pl
from jax.experimental.pallas import tpu as pltpu

F32 = jnp.float32
BF16 = jnp.bfloat16

D_MODEL = 1024
HEAD_DIM = 64
N_HEADS_A, N_HEADS_B, N_HEADS_C = 4, 6, 6
WIDTH_A, WIDTH_B, WIDTH_C = N_HEADS_A * HEAD_DIM, N_HEADS_B * HEAD_DIM, N_HEADS_C * HEAD_DIM
DIFF_DIM = HEAD_DIM // 2
ROPE_THETA = 500000.0
ROPE_FRACTION = 4
DILATIONS = (1, 4, 16)
assert DILATIONS[0] == 1 and DILATIONS[2] == DILATIONS[1] ** 2
BAND_HALF = 64
GRID_W = 64
NA_ROWS = 8
NA_COLS = 16
N_EXPERTS = 8
TOP_K = 2
LN_EPS = 1e-5
NEG = -1e30
LOG2E = math.log2(math.e)
LN2 = math.log(2.0)

LANES = 128
SUBLANES = 8
VMEM_LIMIT = 56 * 1024 * 1024

PROJ_W = 3 * (WIDTH_A + WIDTH_B + WIDTH_C)
ROPE_COLS = 3 * WIDTH_A + 2 * WIDTH_B
COL_CHUNK = 3 * WIDTH_A
BAND_LO = 3 * WIDTH_A
BAND_COLS = 3 * WIDTH_B


def _cparams(sem):
    return pltpu.CompilerParams(dimension_semantics=sem, vmem_limit_bytes=VMEM_LIMIT)


def _layer_norm(z, g, b):
    mu = jnp.mean(z, axis=-1, keepdims=True)
    zc = z - mu
    var = jnp.mean(zc * zc, axis=-1, keepdims=True)
    return zc * lax.rsqrt(var + LN_EPS) * g + b


def _rope_tables(seq):
    rot_a = DIFF_DIM // ROPE_FRACTION
    rot_b = HEAD_DIM // ROPE_FRACTION
    pos = jnp.arange(seq, dtype=F32)[:, None]
    inv_a = ROPE_THETA ** (-jnp.arange(0, rot_a, 2, dtype=F32) / rot_a)
    inv_b = ROPE_THETA ** (-jnp.arange(0, rot_b, 2, dtype=F32) / rot_b)
    ang_a, ang_b = pos * inv_a[None, :], pos * inv_b[None, :]

    cols = np.arange(ROPE_COLS)
    in_a = cols < 3 * WIDTH_A
    sub_a = cols // WIDTH_A
    within_a = cols % DIFF_DIM
    rot_mask_a = in_a & (sub_a < 2) & (within_a < rot_a)
    cb = cols - 3 * WIDTH_A
    sub_b = cb // WIDTH_B
    within_b = cb % HEAD_DIM
    rot_mask_b = (~in_a) & (within_b < rot_b)

    first_half = np.where(in_a, within_a < rot_a // 2, within_b < rot_b // 2)
    rot = rot_mask_a | rot_mask_b
    scale = np.ones(ROPE_COLS, np.float32)
    scale[in_a & (sub_a == 0)] = DIFF_DIM ** -0.5 * LOG2E
    scale[(~in_a) & (sub_b == 0)] = HEAD_DIM ** -0.5 * LOG2E

    def spread(tab_a, tab_b, fill):
        def groups(tab, width, reps):
            pad = jnp.full((seq, width - 2 * tab.shape[1]), fill, F32)
            return jnp.tile(jnp.concatenate([tab, tab, pad], axis=1), (1, reps))
        return jnp.concatenate([groups(tab_a, DIFF_DIM, 2 * WIDTH_A // DIFF_DIM),
                                jnp.full((seq, WIDTH_A), fill, F32),
                                groups(tab_b, HEAD_DIM, 2 * WIDTH_B // HEAD_DIM)], axis=1)

    cm = spread(jnp.cos(ang_a), jnp.cos(ang_b), 1.0) * jnp.asarray(scale)[None, :]
    sn = spread(jnp.sin(ang_a), jnp.sin(ang_b), 0.0) * jnp.asarray(scale)[None, :]
    ma = np.where(rot & first_half, -1.0, 0.0).astype(np.float32)[None, :]
    mb = np.where(rot & ~first_half, 1.0, 0.0).astype(np.float32)[None, :]
    return cm, sn, jnp.asarray(ma), jnp.asarray(mb)


PROJ_TM = 512
W_CAST_ROWS = 128


def _cast_rows_to_bf16(src_ref, dst_ref, rows):
    def body(i, c):
        r = pl.multiple_of(i * W_CAST_ROWS, W_CAST_ROWS)
        dst_ref[pl.ds(r, W_CAST_ROWS), :] = src_ref[pl.ds(r, W_CAST_ROWS), :].astype(BF16)
        return c
    lax.fori_loop(0, rows // W_CAST_ROWS, body, 0)


def _proj_kernel(x_ref, w_ref, cm_ref, sn_ref, ma_ref, mb_ref, o_ref, *rest):
    res_refs, (wbf_ref, band_ref, split_ref) = rest[:-3], rest[-3:]

    @pl.when((pl.program_id(0) == 0) & (pl.program_id(1) == 0))
    def _():
        _cast_rows_to_bf16(w_ref, wbf_ref, D_MODEL)

    xb = x_ref[0].astype(BF16)
    tm = xb.shape[0]
    n_chunks = PROJ_W // COL_CHUNK
    for c in range(n_chunks):
        lo, hi = c * COL_CHUNK, (c + 1) * COL_CHUNK
        acc = jnp.dot(xb, wbf_ref[:, lo:hi], preferred_element_type=F32)
        if hi <= ROPE_COLS:
            half = (DIFF_DIM if c == 0 else HEAD_DIM) // ROPE_FRACTION // 2
            up = pltpu.roll(acc, COL_CHUNK - half, 1)
            dn = pltpu.roll(acc, half, 1)
            acc = acc * cm_ref[:, lo:hi] + (up * ma_ref[:, lo:hi] + dn * mb_ref[:, lo:hi]) * sn_ref[:, lo:hi]
        elif c == 2:
            lane = lax.broadcasted_iota(jnp.int32, (1, COL_CHUNK), 1)
            acc = acc * jnp.where(lane >= WIDTH_B, HEAD_DIM ** -0.5 * LOG2E, 1.0)
        o_ref[0, :, lo:hi] = acc.astype(BF16)
        for col in range(max(lo, BAND_LO), min(hi, BAND_LO + BAND_COLS), LANES):
            band_ref[(col - BAND_LO) // LANES] = acc[:, col - lo:col - lo + LANES]

    (d1, d2), (r1_ref, r2_ref) = DILATIONS[1:], res_refs
    for page in range(BAND_COLS // LANES):
        cs = slice(page * LANES, (page + 1) * LANES)
        for r in range(d1):
            rows = band_ref[page, pl.ds(r, tm // d1, stride=d1), :]
            r1_ref[0, r, :, cs] = rows.astype(BF16)
            split_ref[page, r] = rows
        for r in range(d2):
            r2_ref[0, r, :, cs] = split_ref[page, r % d1, pl.ds(r // d1, tm // d2, stride=d1), :].astype(BF16)


def _project(x, w, layer, tables):
    B, S, _ = x.shape
    cm, sn, ma, mb = tables
    tm = PROJ_TM
    res_specs = [pl.BlockSpec((1, d, tm // d, BAND_COLS), lambda p, b: (b, 0, p, 0)) for d in DILATIONS[1:]]
    res_shapes = [jax.ShapeDtypeStruct((B, d, S // d, BAND_COLS), BF16) for d in DILATIONS[1:]]
    return pl.pallas_call(
        _proj_kernel,
        grid=(S // tm, B),
        in_specs=[
            pl.BlockSpec((1, tm, D_MODEL), lambda p, b: (b, p, 0)),
            pl.BlockSpec((None, D_MODEL, PROJ_W), lambda p, b: (layer, 0, 0), pipeline_mode=pl.Buffered(1)),
            pl.BlockSpec((tm, ROPE_COLS), lambda p, b: (p, 0)),
            pl.BlockSpec((tm, ROPE_COLS), lambda p, b: (p, 0)),
            pl.BlockSpec((1, ROPE_COLS), lambda p, b: (0, 0)),
            pl.BlockSpec((1, ROPE_COLS), lambda p, b: (0, 0)),
        ],
        out_specs=[pl.BlockSpec((1, tm, PROJ_W), lambda p, b: (b, p, 0))] + res_specs,
        out_shape=[jax.ShapeDtypeStruct((B, S, PROJ_W), BF16)] + res_shapes,
        scratch_shapes=[pltpu.VMEM((D_MODEL, PROJ_W), BF16),
                        pltpu.VMEM((BAND_COLS // LANES, tm, LANES), F32),
                        pltpu.VMEM((BAND_COLS // LANES, DILATIONS[1], tm // DILATIONS[1], LANES), F32)],
        compiler_params=_cparams(("arbitrary", "arbitrary")),
        name="proj_in",
    )(x, w, cm, sn, ma, mb)


ATT_A_TQ = 512
ATT_A_UNROLL = 2


def _nt_dot(a, b):
    return lax.dot_general(a, b, (((1,), (1,)), ((), ())), preferred_element_type=F32)


def _with_ones(v):
    return jnp.concatenate([v, jnp.ones_like(v)], axis=1)


def _softmax_times_v(e, v1):
    nd = jnp.dot(e, v1, preferred_element_type=F32)
    return nd[:, :LANES] * (1.0 / nd[:, LANES:])


def _attn_a_kernel(lam_ref, g_ref, q_ref, k_ref, v_ref, o_ref, *, lam_init, seq):
    lp = lam_ref[...]
    lam = (jnp.exp(jnp.sum(lp[0:1] * lp[1:2], axis=-1, keepdims=True))
           - jnp.exp(jnp.sum(lp[2:3] * lp[3:4], axis=-1, keepdims=True)) + lam_init)
    lane = lax.broadcasted_iota(jnp.int32, (1, LANES), 1)
    first = lane < HEAD_DIM
    gain = g_ref[...] * (1.0 - lam_init)
    tq = ATT_A_TQ

    def qblock(i, carry):
        r0 = pl.multiple_of(i * tq, tq)
        q = q_ref[0, pl.ds(r0, tq), :]
        k = k_ref[0]
        v1 = _with_ones(v_ref[0])
        outs = []
        for hh in range(2):
            es = []
            for m in range(2):
                lo = hh * HEAD_DIM + m * DIFF_DIM
                qm = jnp.where((lane >= lo) & (lane < lo + DIFF_DIM), q, jnp.zeros_like(q))
                s = _nt_dot(qm, k).astype(BF16)
                es.append(jnp.exp2(s - jnp.max(s, axis=-1, keepdims=True)))
            parts = _softmax_times_v(jnp.concatenate(es, axis=0), v1)
            outs.append(parts[:tq] - lam * parts[tq:])
        o = jnp.where(first, outs[0], outs[1])
        sq = o * o
        ss0 = jnp.sum(jnp.where(first, sq, 0.0), axis=-1, keepdims=True)
        ss1 = jnp.sum(jnp.where(first, 0.0, sq), axis=-1, keepdims=True)
        ms = jnp.where(first, ss0, ss1) * (1.0 / HEAD_DIM)
        o_ref[0, pl.ds(r0, tq), :] = (o * lax.rsqrt(ms + LN_EPS) * gain).astype(BF16)
        return carry

    lax.fori_loop(0, seq // tq, qblock, 0, unroll=ATT_A_UNROLL)


def _attn_a(qkv, diff_lambda, norm_g, lam_init):
    B, S, _ = qkv.shape
    n_pairs = WIDTH_A // LANES
    g2 = jnp.tile(norm_g.astype(F32), LANES // HEAD_DIM)[None, :]
    blk = lambda off: pl.BlockSpec((1, S, LANES), lambda b, h, off=off: (b, 0, off + h))
    return pl.pallas_call(
        functools.partial(_attn_a_kernel, lam_init=lam_init, seq=S),
        grid=(B, n_pairs),
        in_specs=[
            pl.BlockSpec((4, DIFF_DIM), lambda b, h: (0, 0)),
            pl.BlockSpec((1, LANES), lambda b, h: (0, 0)),
            blk(0), blk(n_pairs), blk(2 * n_pairs),
        ],
        out_specs=pl.BlockSpec((1, S, LANES), lambda b, h: (b, 0, h)),
        out_shape=jax.ShapeDtypeStruct((B, S, WIDTH_A), BF16),
        compiler_params=_cparams(("arbitrary", "arbitrary")),
        name="attn_diff",
    )(diff_lambda.astype(F32), g2, qkv, qkv, qkv)


BAND_TQ = 128
BAND_CHAINS = 12


def _stack_pair(q, lane):
    z = jnp.zeros_like(q)
    return jnp.concatenate([jnp.where(lane < HEAD_DIM, q, z), jnp.where(lane < HEAD_DIM, z, q)], axis=0)


def _attn_b_kernel(q_ref, k_ref, v_ref, o_ref, lse_ref, *, length, dil):
    tq = BAND_TQ
    win = min(tq + 2 * BAND_HALF, length)
    lane = lax.broadcasted_iota(jnp.int32, (1, LANES), 1)

    blocks = [(i * tq, min(max(i * tq - BAND_HALF, 0), length - win)) for i in range(length // tq)]
    row = lax.broadcasted_iota(jnp.int32, (2 * tq, win), 0)
    delta = lax.broadcasted_iota(jnp.int32, (2 * tq, win), 1) - jnp.where(row >= tq, row - tq, row)
    masks = {off: jnp.where(jnp.abs(delta - off) <= BAND_HALF, 0.0, NEG) for off in {q0 - k0 for q0, k0 in blocks}}

    def residue(r, carry):
        for q0, k0 in blocks:
            lse_tile = jnp.zeros((tq, LANES), F32)
            for pr in range(WIDTH_B // LANES):
                cs = slice(pr * LANES, (pr + 1) * LANES)
                q2 = _stack_pair(q_ref[0, r, q0:q0 + tq, cs], lane)
                s = _nt_dot(q2, k_ref[0, r, k0:k0 + win, cs]) + masks[q0 - k0]
                mx = jnp.max(s, axis=-1, keepdims=True)
                e = jnp.exp2(s - mx).astype(BF16)
                nd = jnp.dot(e, _with_ones(v_ref[0, r, k0:k0 + win, cs]), preferred_element_type=F32)
                l = nd[:, LANES:]
                o2 = nd[:, :LANES] * (1.0 / l)
                o_ref[0, r, q0:q0 + tq, cs] = jnp.where(lane < HEAD_DIM, o2[:tq], o2[tq:]).astype(BF16)
                lse2 = (mx + jnp.log2(l)) * LN2
                lse_tile = jnp.where(lane == 2 * pr, lse2[:tq], lse_tile)
                lse_tile = jnp.where(lane == 2 * pr + 1, lse2[tq:], lse_tile)
            lse_ref[0, r, q0:q0 + tq, :] = lse_tile
        return carry

    chains = len(blocks) * (WIDTH_B // LANES)
    lax.fori_loop(0, dil, residue, 0, unroll=min(dil, max(1, BAND_CHAINS // chains)))


def _attn_b(src, dil, first_blk):
    B, _, L, _ = src.shape
    blk = lambda off: pl.BlockSpec((1, dil, L, WIDTH_B), lambda b, off=off: (b, 0, 0, first_blk + off))
    return pl.pallas_call(
        functools.partial(_attn_b_kernel, length=L, dil=dil),
        grid=(B,),
        in_specs=[blk(0), blk(1), blk(2)],
        out_specs=[pl.BlockSpec((1, dil, L, WIDTH_B), lambda b: (b, 0, 0, 0)),
                   pl.BlockSpec((1, dil, L, LANES), lambda b: (b, 0, 0, 0))],
        out_shape=[jax.ShapeDtypeStruct((B, dil, L, WIDTH_B), BF16),
                   jax.ShapeDtypeStruct((B, dil, L, LANES), F32)],
        compiler_params=_cparams(("arbitrary",)),
        name=f"attn_band_d{dil}",
    )(src, src, src)


N_RPB_ROWS = 2 * NA_ROWS - 1
N_RPB_COLS = 2 * NA_COLS - 1
NA_KEYS = NA_ROWS * GRID_W
N_BIAS_VARIANTS = NA_ROWS
NA_ROWS_PER_STEP = 4


def _attn_c_kernel(u_ref, q_ref, k_ref, v_ref, o_ref, bias_ref, *, rows):
    lane = lax.broadcasted_iota(jnp.int32, (1, LANES), 1)
    n_pairs = WIDTH_C // LANES

    @pl.when(pl.program_id(0) == 0)
    def _():
        wide = N_RPB_ROWS * GRID_W
        c = lax.broadcasted_iota(jnp.int32, (GRID_W, wide), 0)
        kc = lax.broadcasted_iota(jnp.int32, (GRID_W, wide), 1) % GRID_W
        tidx = jnp.clip(kc - c + NA_COLS - 1, 0, N_RPB_COLS - 1)
        c5 = lax.broadcasted_iota(jnp.int32, (GRID_W, NA_KEYS), 0)
        kc5 = lax.broadcasted_iota(jnp.int32, (GRID_W, NA_KEYS), 1) % GRID_W
        cstart = jnp.clip(c5 - NA_COLS // 2, 0, GRID_W - NA_COLS)
        valid = (kc5 >= cstart) & (kc5 < cstart + NA_COLS)
        for h in range(N_HEADS_C):
            def tbody(t, acc, h=h):
                row = u_ref[h, pl.ds(t, 1), :]
                return jnp.where(tidx == t, row, acc)
            toep = lax.fori_loop(0, N_RPB_COLS, tbody, jnp.zeros((GRID_W, wide), F32))
            for var in range(N_BIAS_VARIANTS):
                tile = jnp.where(valid, toep[:, var * GRID_W:var * GRID_W + NA_KEYS] * LOG2E, NEG)
                bias_ref[var, h // 2, (h % 2) * GRID_W:(h % 2 + 1) * GRID_W, :] = tile

    def rbody(step, carry):
        for sub in range(NA_ROWS_PER_STEP):
            r = step * NA_ROWS_PER_STEP + sub
            rs = jnp.clip(r - NA_ROWS // 2, 0, rows - NA_ROWS)
            var = rs - r + NA_ROWS - 1
            q0 = pl.multiple_of(r * GRID_W, GRID_W)
            k0 = pl.multiple_of(rs * GRID_W, GRID_W)
            for pr in range(n_pairs):
                cs = slice(pr * LANES, (pr + 1) * LANES)
                q2 = _stack_pair(q_ref[0, pl.ds(q0, GRID_W), cs], lane)
                s = _nt_dot(q2, k_ref[0, pl.ds(k0, NA_KEYS), cs]) + bias_ref[var, pr]
                e = jnp.exp2(s - jnp.max(s, axis=-1, keepdims=True)).astype(BF16)
                o2 = _softmax_times_v(e, _with_ones(v_ref[0, pl.ds(k0, NA_KEYS), cs]))
                o_ref[0, pl.ds(q0, GRID_W), cs] = jnp.where(lane < HEAD_DIM, o2[:GRID_W], o2[GRID_W:]).astype(BF16)
        return carry

    lax.fori_loop(0, rows // NA_ROWS_PER_STEP, rbody, 0)


def _attn_c(qkv, rpb):
    B, S, _ = qkv.shape
    rows = S // GRID_W
    assert rows >= NA_ROWS
    u = jnp.repeat(jnp.transpose(rpb.astype(F32), (0, 2, 1)), GRID_W, axis=2)
    q_blk = (3 * WIDTH_A + 3 * WIDTH_B) // WIDTH_C
    blk = lambda off: pl.BlockSpec((1, S, WIDTH_C), lambda b, off=off: (b, 0, off))
    return pl.pallas_call(
        functools.partial(_attn_c_kernel, rows=rows),
        grid=(B,),
        in_specs=[pl.BlockSpec((N_HEADS_C, N_RPB_COLS, N_RPB_ROWS * GRID_W), lambda b: (0, 0, 0)),
                  blk(q_blk), blk(q_blk + 1), blk(q_blk + 2)],
        out_specs=pl.BlockSpec((1, S, WIDTH_C), lambda b: (b, 0, 0)),
        out_shape=jax.ShapeDtypeStruct((B, S, WIDTH_C), BF16),
        scratch_shapes=[pltpu.VMEM((N_BIAS_VARIANTS, WIDTH_C // LANES, LANES, NA_KEYS), F32)],
        compiler_params=_cparams(("arbitrary",)),
        name="attn_nbr",
    )(u, qkv, qkv, qkv)


OUT_TM = 512
OUT_SPLIT = 2


def _split_bf16(x):
    hi = x.astype(BF16)
    return hi, (x - hi.astype(F32)).astype(BF16)


def _outproj_kernel(x_ref, oa_ref, ob0_ref, ob1_ref, ob2_ref, l0_ref, l1_ref, l2_ref, oc_ref,
                    beta_ref, w_ref, g_ref, b_ref, e_ref, o_ref, wbf_ref, il_ref, *, alpha, tile_out):
    @pl.when(pl.program_id(0) == 0)
    def _():
        _cast_rows_to_bf16(w_ref, wbf_ref, D_MODEL)

    tm = x_ref.shape[0]
    n_pages = WIDTH_B // LANES
    for slot, (dil, ob_r, l_r) in enumerate(zip(DILATIONS[1:], (ob1_ref, ob2_ref), (l1_ref, l2_ref))):
        for r in range(dil):
            rows = pl.ds(r, tm // dil, stride=dil)
            for page in range(n_pages):
                il_ref[slot, page, rows, :] = ob_r[0, r, :, page * LANES:(page + 1) * LANES].astype(F32)
            il_ref[slot, n_pages, rows, :] = l_r[0, r]

    beta = beta_ref[...]
    half = tm // OUT_SPLIT
    for part in range(OUT_SPLIT):
        rs = slice(part * half, (part + 1) * half)
        l0, l1, l2 = l0_ref[0, 0, rs, :], il_ref[0, n_pages, rs, :], il_ref[1, n_pages, rs, :]
        lm = jnp.maximum(jnp.maximum(l0, l1), l2)
        e0, e1, e2 = jnp.exp(l0 - lm), jnp.exp(l1 - lm), jnp.exp(l2 - lm)
        inv = 1.0 / (e0 + e1 + e2)
        wexp = [jnp.dot(jnp.concatenate(_split_bf16(e * inv), axis=1), e_ref[...], preferred_element_type=F32)
                for e in (e0, e1, e2)]
        pages = []
        for page in range(n_pages):
            cs = slice(page * LANES, (page + 1) * LANES)
            branch_vals = (ob0_ref[0, 0, rs, cs].astype(F32), il_ref[0, page, rs, :], il_ref[1, page, rs, :])
            pages.append(sum(w[:, cs] * val for w, val in zip(wexp, branch_vals)))
        ob = jnp.concatenate(pages, axis=1)

        ma = (oa_ref[rs, :].astype(F32) * beta[:, :WIDTH_A]).astype(BF16)
        mb = (ob * beta[:, WIDTH_A:WIDTH_A + WIDTH_B]).astype(BF16)
        mc = (oc_ref[rs, :].astype(F32) * beta[:, WIDTH_A + WIDTH_B:]).astype(BF16)
        acc = jnp.dot(jnp.concatenate([ma, mb, mc], axis=1), wbf_ref[...], preferred_element_type=F32)
        y = _layer_norm(alpha * x_ref[rs, :] + acc, g_ref[...], b_ref[...])
        if tile_out:
            _store_token_tiles(o_ref, y, row0=part * half)
        else:
            o_ref[rs, :] = y


def _store_token_tiles(ref, y, row0=0):
    rows = y.shape[0]
    for j in range(D_MODEL // LANES):
        ref[pl.ds(row0 * SUBLANES + j, rows, stride=SUBLANES), :] = y[:, j * LANES:(j + 1) * LANES]


def _load_token_tiles(ref, rows, row0=0):
    return jnp.concatenate(
        [ref[pl.ds(row0 * SUBLANES + j, rows, stride=SUBLANES), :] for j in range(D_MODEL // LANES)], axis=1)


def _outproj(x2, oa, obs, lses, oc, beta, w, layer, g, b, alpha, tile_out):
    N = x2.shape[0]
    tm = OUT_TM
    steps_per_seq = obs[0].shape[2] // tm
    expand = np.zeros((2 * LANES, WIDTH_B), np.float32)
    for h in range(N_HEADS_B):
        expand[h, h * HEAD_DIM:(h + 1) * HEAD_DIM] = 1.0
        expand[LANES + h, h * HEAD_DIM:(h + 1) * HEAD_DIM] = 1.0
    row = lambda w_: pl.BlockSpec((tm, w_), lambda i: (i, 0))
    full = lambda r, c, **kw: pl.BlockSpec((r, c), lambda i: (0, 0), **kw)
    res = lambda d, w_: pl.BlockSpec((1, d, tm // d, w_),
                                     lambda i: (i // steps_per_seq, 0, i % steps_per_seq, 0))
    if tile_out:
        out_spec = pl.BlockSpec((tm * SUBLANES, LANES), lambda i: (i, 0))
        out_shape = jax.ShapeDtypeStruct((N * SUBLANES, LANES), F32)
    else:
        out_spec = row(D_MODEL)
        out_shape = jax.ShapeDtypeStruct((N, D_MODEL), F32)
    return pl.pallas_call(
        functools.partial(_outproj_kernel, alpha=alpha, tile_out=tile_out),
        grid=(N // tm,),
        in_specs=[row(D_MODEL), row(WIDTH_A)] + [res(d, WIDTH_B) for d in DILATIONS]
                 + [res(d, LANES) for d in DILATIONS] + [row(WIDTH_C),
                  full(1, D_MODEL),
                  pl.BlockSpec((None, D_MODEL, D_MODEL), lambda i: (layer, 0, 0), pipeline_mode=pl.Buffered(1)),
                  full(1, D_MODEL), full(1, D_MODEL), full(2 * LANES, WIDTH_B)],
        out_specs=out_spec,
        out_shape=out_shape,
        scratch_shapes=[pltpu.VMEM((D_MODEL, D_MODEL), BF16),
                        pltpu.VMEM((len(DILATIONS) - 1, WIDTH_B // LANES + 1, tm, LANES), F32)],
        compiler_params=_cparams(("arbitrary",)),
        name="merge_outproj_ln",
    )(x2, oa, obs[0], obs[1], obs[2], lses[0], lses[1], lses[2], oc,
      beta[None, :].astype(F32), w, g[None, :].astype(F32), b[None, :].astype(F32),
      jnp.asarray(expand, BF16))


CAST_ROWS = 256
DENSE_TM = 512
DENSE_TF = 256


def _cast_kernel(w_ref, o_ref):
    o_ref[...] = w_ref[...].astype(BF16)


def _to_bf16(w):
    rows, cols = w.shape
    return pl.pallas_call(
        _cast_kernel,
        grid=(rows // CAST_ROWS,),
        in_specs=[pl.BlockSpec((CAST_ROWS, cols), lambda i: (i, 0))],
        out_specs=pl.BlockSpec((CAST_ROWS, cols), lambda i: (i, 0)),
        out_shape=jax.ShapeDtypeStruct((rows, cols), BF16),
        compiler_params=_cparams(("arbitrary",)),
        name="cast_bf16",
    )(w)


def _dense_ffn_kernel(x_ref, wg_ref, wu_ref, wd_ref, g_ref, b_ref, o_ref, h_ref, *, alpha):
    x = x_ref[...]
    xb = x.astype(BF16)
    for c in range(h_ref.shape[1] // DENSE_TF):
        cs = slice(c * DENSE_TF, (c + 1) * DENSE_TF)
        gate = jnp.dot(xb, wg_ref[:, cs], preferred_element_type=F32)
        up = jnp.dot(xb, wu_ref[:, cs], preferred_element_type=F32)
        h_ref[:, cs] = (gate * jax.nn.sigmoid(gate) * up).astype(BF16)
    y = jnp.dot(h_ref[...], wd_ref[...], preferred_element_type=F32)
    o_ref[...] = _layer_norm(alpha * x + y, g_ref[...], b_ref[...])


def _dense_ffn(x, w_gate, w_up, w_down, g, b, alpha):
    n_tok = x.shape[0]
    ff = w_gate.shape[1]
    tm = DENSE_TM
    row = pl.BlockSpec((tm, D_MODEL), lambda i: (i, 0))
    resident = lambda r, c: pl.BlockSpec((r, c), lambda i: (0, 0), pipeline_mode=pl.Buffered(1))
    return pl.pallas_call(
        functools.partial(_dense_ffn_kernel, alpha=alpha),
        grid=(n_tok // tm,),
        in_specs=[row, resident(D_MODEL, ff), resident(D_MODEL, ff), resident(ff, D_MODEL),
                  pl.BlockSpec((1, D_MODEL), lambda i: (0, 0)), pl.BlockSpec((1, D_MODEL), lambda i: (0, 0))],
        out_specs=row,
        out_shape=jax.ShapeDtypeStruct((n_tok, D_MODEL), F32),
        scratch_shapes=[pltpu.VMEM((tm, ff), BF16)],
        compiler_params=_cparams(("arbitrary",)),
        name="swiglu_dense",
    )(x, _to_bf16(w_gate), _to_bf16(w_up), _to_bf16(w_down), g[None, :].astype(F32), b[None, :].astype(F32))


FFN_TM = 1024
FFN_TF = 512
FFN_SUB = 256


def _ffn_kernel(te_ref, nt_ref, rows_ref, x_ref, wg_ref, wu_ref, wd_ref, o_ref, xb_ref, acc_ref, *, n_chunks):
    i, j = pl.program_id(0), pl.program_id(1)
    tm = xb_ref.shape[0]

    def swiglu_rows(rs):
        xb = xb_ref[rs, :]
        gate = jnp.dot(xb, wg_ref[...].astype(BF16), preferred_element_type=F32)
        up = jnp.dot(xb, wu_ref[...].astype(BF16), preferred_element_type=F32)
        h = (gate * jax.nn.sigmoid(gate) * up).astype(BF16)
        acc_ref[rs, :] += jnp.dot(h, wd_ref[...].astype(BF16), preferred_element_type=F32)

    @pl.when(i < nt_ref[0])
    def _():
        @pl.when(j == 0)
        def _():
            xb_ref[...] = _load_token_tiles(x_ref, tm).astype(BF16)
            acc_ref[...] = jnp.zeros_like(acc_ref)

        sparse_tile = rows_ref[i] <= tm - FFN_SUB

        @pl.when(jnp.logical_not(sparse_tile))
        def _():
            swiglu_rows(slice(None))

        for sb in range(tm // FFN_SUB - 1):
            @pl.when(sparse_tile & (sb * FFN_SUB < rows_ref[i]))
            def _(sb=sb):
                swiglu_rows(slice(sb * FFN_SUB, (sb + 1) * FFN_SUB))

        @pl.when(j == n_chunks - 1)
        def _():
            _store_token_tiles(o_ref, acc_ref[...])

    @pl.when((i >= nt_ref[0]) & (j == n_chunks - 1))
    def _():
        o_ref[...] = jnp.zeros_like(o_ref)


def _grouped_ffn(x, w_gate, w_up, w_down, tile_expert, n_tiles, tile_rows):
    tm, tf = FFN_TM, FFN_TF
    n_chunks = w_gate.shape[-1] // tf
    max_tiles = tile_expert.shape[0]

    def tile_ix(i, nt):
        return jnp.minimum(i, nt[0] - 1)

    def chunk_ix(i, j, nt):
        return jnp.where(i < nt[0], j, n_chunks - 1)

    blk = (tm * SUBLANES, LANES)
    in_specs = [
        pl.BlockSpec(blk, lambda i, j, te, nt, tr: (tile_ix(i, nt), 0)),
        pl.BlockSpec((None, D_MODEL, tf), lambda i, j, te, nt, tr: (te[tile_ix(i, nt)], 0, chunk_ix(i, j, nt))),
        pl.BlockSpec((None, D_MODEL, tf), lambda i, j, te, nt, tr: (te[tile_ix(i, nt)], 0, chunk_ix(i, j, nt))),
        pl.BlockSpec((None, tf, D_MODEL), lambda i, j, te, nt, tr: (te[tile_ix(i, nt)], chunk_ix(i, j, nt), 0)),
    ]
    return pl.pallas_call(
        functools.partial(_ffn_kernel, n_chunks=n_chunks),
        grid_spec=pltpu.PrefetchScalarGridSpec(
            num_scalar_prefetch=3,
            grid=(max_tiles, n_chunks),
            in_specs=in_specs,
            out_specs=pl.BlockSpec(blk, lambda i, j, te, nt, tr: (i, 0)),
            scratch_shapes=[pltpu.VMEM((tm, D_MODEL), BF16), pltpu.VMEM((tm, D_MODEL), F32)],
        ),
        out_shape=jax.ShapeDtypeStruct(x.shape, F32),
        compiler_params=_cparams(("arbitrary", "arbitrary")),
        name="swiglu_grouped",
    )(tile_expert, n_tiles, tile_rows, x, w_gate, w_up, w_down)


ROUTE_TB = 512
META_E1, META_E2, META_R1, META_R2, META_G1, META_G2 = range(6)


def _router_kernel(x_ref, wr_ref, meta_ref, fields_ref, cnt_ref, carry_ref):
    tb = ROUTE_TB

    @pl.when(pl.program_id(0) == 0)
    def _():
        carry_ref[...] = jnp.zeros_like(carry_ref)

    x = _load_token_tiles(x_ref, tb)
    xh, xl = _split_bf16(x)
    wh, wl = _split_bf16(wr_ref[...])
    logits = (jnp.dot(xh, wh, preferred_element_type=F32) + jnp.dot(xh, wl, preferred_element_type=F32)
              + jnp.dot(xl, wh, preferred_element_type=F32))
    lane = lax.broadcasted_iota(jnp.int32, (tb, LANES), 1).astype(F32)
    logits = jnp.where(lane < N_EXPERTS, logits, -jnp.inf)
    m1 = jnp.max(logits, axis=-1, keepdims=True)
    i1 = jnp.min(jnp.where(logits == m1, lane, float(LANES)), axis=-1, keepdims=True)
    rest = jnp.where(lane == i1, -jnp.inf, logits)
    m2 = jnp.max(rest, axis=-1, keepdims=True)
    i2 = jnp.min(jnp.where(rest == m2, lane, float(LANES)), axis=-1, keepdims=True)
    e21 = jnp.exp(m2 - m1)
    g1 = 1.0 / (1.0 + e21)
    g2 = e21 * g1

    member = ((lane == i1) | (lane == i2)).astype(F32)
    r_i = lax.broadcasted_iota(jnp.int32, (tb, tb), 0)
    c_i = lax.broadcasted_iota(jnp.int32, (tb, tb), 1)
    lower = (c_i < r_i).astype(BF16)
    before = jnp.dot(lower, member.astype(BF16), preferred_element_type=F32)
    rank = carry_ref[...] + before
    r1 = jnp.sum(jnp.where(lane == i1, rank, 0.0), axis=-1, keepdims=True)
    r2 = jnp.sum(jnp.where(lane == i2, rank, 0.0), axis=-1, keepdims=True)
    carry_ref[...] += jnp.sum(member, axis=0, keepdims=True)
    cnt_ref[...] = carry_ref[...]

    meta = jnp.zeros((tb, LANES), F32)
    for slot, val in ((META_E1, i1.astype(F32)), (META_E2, i2.astype(F32)), (META_R1, r1), (META_R2, r2),
                      (META_G1, g1), (META_G2, g2)):
        meta = jnp.where(lane == slot, val, meta)
    meta_ref[...] = meta
    fields_ref[...] = meta.T[:SUBLANES]


def _router(xt, w_router):
    n_tok = xt.shape[0] // SUBLANES
    tb = ROUTE_TB
    wr = jnp.zeros((D_MODEL, LANES), F32).at[:, :N_EXPERTS].set(w_router.astype(F32))
    return pl.pallas_call(
        _router_kernel,
        grid=(n_tok // tb,),
        in_specs=[pl.BlockSpec((tb * SUBLANES, LANES), lambda i: (i, 0)),
                  pl.BlockSpec((D_MODEL, LANES), lambda i: (0, 0))],
        out_specs=[pl.BlockSpec((tb, LANES), lambda i: (i, 0)),
                   pl.BlockSpec((SUBLANES, tb), lambda i: (0, i)),
                   pl.BlockSpec((1, LANES), lambda i: (0, 0))],
        out_shape=[jax.ShapeDtypeStruct((n_tok, LANES), F32), jax.ShapeDtypeStruct((SUBLANES, n_tok), F32),
                   jax.ShapeDtypeStruct((1, LANES), F32)],
        scratch_shapes=[pltpu.VMEM((1, LANES), F32)],
        compiler_params=_cparams(("arbitrary",)),
        name="router_top2",
    )(xt, wr)


MOVE_TB = 1024
PAD_CHUNK = 64
DMA_ISSUE_UNROLL = 8


def _tile_rows(ref, idx):
    return ref.at[pl.ds(pl.multiple_of(idx * SUBLANES, SUBLANES), SUBLANES), :]


def _zero_fill(pad_ref, xs_hbm, zbuf, zsem):
    zbuf[...] = jnp.zeros_like(zbuf)
    one = _tile_rows(zbuf, 0)
    for wait in (False, True):
        for seg in range(pad_ref.shape[1]):
            start, n = pad_ref[0, seg], pad_ref[1, seg]
            n_single = n % PAD_CHUNK

            def single(t, c):
                cp = pltpu.make_async_copy(one, _tile_rows(xs_hbm, start + t), zsem)
                cp.wait() if wait else cp.start()
                return c

            def bulk(t, c):
                r0 = pl.multiple_of((start + n_single + t * PAD_CHUNK) * SUBLANES, SUBLANES)
                cp = pltpu.make_async_copy(zbuf, xs_hbm.at[pl.ds(r0, PAD_CHUNK * SUBLANES), :], zsem)
                cp.wait() if wait else cp.start()
                return c

            lax.fori_loop(0, n_single, single, 0)
            lax.fori_loop(0, n // PAD_CHUNK, bulk, 0)


def _dispatch_kernel(dest_ref, pad_ref, x_ref, xs_hbm, zbuf, sem, zsem):
    tb = MOVE_TB

    @pl.when(pl.program_id(0) == 0)
    def _():
        _zero_fill(pad_ref, xs_hbm, zbuf, zsem)

    def body(t, c):
        src = _tile_rows(x_ref, t)
        for k in range(TOP_K):
            pltpu.make_async_copy(src, _tile_rows(xs_hbm, dest_ref[0, 0, TOP_K * t + k]), sem).start(priority=k)
        return c

    lax.fori_loop(0, tb, body, 0, unroll=DMA_ISSUE_UNROLL)
    n_rows = TOP_K * tb * SUBLANES
    pltpu.make_async_copy(xs_hbm.at[pl.ds(0, n_rows), :], xs_hbm.at[pl.ds(0, n_rows), :], sem).wait()


def _dispatch(xt, dest, pads, n_slots):
    n_tok = xt.shape[0] // SUBLANES
    tb = MOVE_TB
    dest3 = dest.reshape(n_tok // tb, 1, TOP_K * tb)
    return pl.pallas_call(
        _dispatch_kernel,
        grid=(n_tok // tb,),
        in_specs=[pl.BlockSpec((1, 1, TOP_K * tb), lambda i: (i, 0, 0), memory_space=pltpu.SMEM),
                  pl.BlockSpec(memory_space=pltpu.SMEM),
                  pl.BlockSpec((tb * SUBLANES, LANES), lambda i: (i, 0))],
        out_specs=pl.BlockSpec(memory_space=pl.ANY),
        out_shape=jax.ShapeDtypeStruct((n_slots * SUBLANES, LANES), F32),
        scratch_shapes=[pltpu.VMEM((PAD_CHUNK * SUBLANES, LANES), F32),
                        pltpu.SemaphoreType.DMA(()), pltpu.SemaphoreType.DMA(())],
        compiler_params=_cparams(("arbitrary",)),
        name="moe_dispatch",
    )(dest3, pads, xt)


COMB_TB = 256


def _combine_kernel(dest_ref, next_ref, y_hbm, x_ref, meta_ref, g_ref, b_ref, o_ref, ybuf, sem, *, alpha, n_steps):
    tb = COMB_TB
    i = pl.program_id(0)
    slot = i % 2

    def gather(d_ref, s):
        def body(t, c):
            for k in range(TOP_K):
                pltpu.make_async_copy(_tile_rows(y_hbm, d_ref[0, 0, TOP_K * t + k]),
                                      _tile_rows(ybuf.at[s], k * tb + t), sem.at[s]).start(priority=k)
            return c
        lax.fori_loop(0, tb, body, 0, unroll=DMA_ISSUE_UNROLL)

    @pl.when(i == 0)
    def _():
        gather(dest_ref, 0)

    @pl.when(i + 1 < n_steps)
    def _():
        gather(next_ref, 1 - slot)

    cur = ybuf.at[slot]
    pltpu.make_async_copy(cur, cur, sem.at[slot]).wait()

    meta = meta_ref[...]
    lane = lax.broadcasted_iota(jnp.int32, meta.shape, 1)
    g1 = jnp.sum(jnp.where(lane == META_G1, meta, 0.0), axis=-1, keepdims=True)
    g2 = jnp.sum(jnp.where(lane == META_G2, meta, 0.0), axis=-1, keepdims=True)
    x = _load_token_tiles(x_ref, tb)
    f = g1 * _load_token_tiles(cur, tb) + g2 * _load_token_tiles(cur, tb, row0=tb)
    o_ref[...] = _layer_norm(alpha * x + f, g_ref[...], b_ref[...])


def _combine(yt, xt, dest, meta, g, b, alpha):
    n_tok = xt.shape[0] // SUBLANES
    tb = COMB_TB
    n_steps = n_tok // tb
    dest3 = dest.reshape(n_steps, 1, TOP_K * tb)
    return pl.pallas_call(
        functools.partial(_combine_kernel, alpha=alpha, n_steps=n_steps),
        grid=(n_steps,),
        in_specs=[pl.BlockSpec((1, 1, TOP_K * tb), lambda i: (i, 0, 0), memory_space=pltpu.SMEM),
                  pl.BlockSpec((1, 1, TOP_K * tb), lambda i: (jnp.minimum(i + 1, n_steps - 1), 0, 0),
                               memory_space=pltpu.SMEM),
                  pl.BlockSpec(memory_space=pl.ANY),
                  pl.BlockSpec((tb * SUBLANES, LANES), lambda i: (i, 0)),
                  pl.BlockSpec((tb, LANES), lambda i: (i, 0)),
                  pl.BlockSpec((1, D_MODEL), lambda i: (0, 0)),
                  pl.BlockSpec((1, D_MODEL), lambda i: (0, 0))],
        out_specs=pl.BlockSpec((tb, D_MODEL), lambda i: (i, 0)),
        out_shape=jax.ShapeDtypeStruct((n_tok, D_MODEL), F32),
        scratch_shapes=[pltpu.VMEM((2, TOP_K * tb * SUBLANES, LANES), F32), pltpu.SemaphoreType.DMA((2,))],
        compiler_params=_cparams(("arbitrary",)),
        name="moe_combine_ln",
    )(dest3, dest3, yt, xt, meta, g[None, :].astype(F32), b[None, :].astype(F32))


class MoePlan(NamedTuple):
    dest: jax.Array
    tile_expert: jax.Array
    tile_rows: jax.Array
    n_tiles: jax.Array
    pads: jax.Array
    n_slots: int


def _moe_plan(fields, counts, n_tok):
    tm = FFN_TM
    max_tiles = n_tok * TOP_K // tm + N_EXPERTS
    cnt = counts[0, :N_EXPERTS].astype(jnp.int32)
    tiles = (cnt + tm - 1) // tm
    tile_end = jnp.cumsum(tiles)
    start = (tile_end - tiles) * tm
    e = fields[META_E1:META_E2 + 1].astype(jnp.int32)
    r = fields[META_R1:META_R2 + 1].astype(jnp.int32)
    experts = jnp.arange(N_EXPERTS, dtype=jnp.int32)[:, None, None]
    group_start = jnp.sum(jnp.where(e[None] == experts, start[:, None, None], 0), axis=0)
    dest = (group_start + r).T.reshape(-1)
    tile_ids = jnp.arange(max_tiles, dtype=jnp.int32)
    tile_expert = jnp.minimum(jnp.sum((tile_ids[:, None] >= tile_end[None, :]).astype(jnp.int32), axis=1),
                              N_EXPERTS - 1)
    first_tile = (tile_end - tiles)[tile_expert]
    tile_rows = jnp.clip(cnt[tile_expert] - (tile_ids - first_tile) * tm, 0, tm).astype(jnp.int32)
    n_tiles = tile_end[-1:]
    pads = jnp.stack([jnp.concatenate([start + cnt, n_tiles * tm]),
                      jnp.concatenate([tiles * tm - cnt, (max_tiles - n_tiles) * tm])]).astype(jnp.int32)
    return MoePlan(dest, tile_expert, tile_rows, n_tiles.astype(jnp.int32), pads, max_tiles * tm)


def kernel(x, w_in, w_out, beta_mix, diff_lambda, diff_norm_g, na_rpb, ln1_g, ln1_b, ln2_g, ln2_b,
           ffn_w_gate, ffn_w_up, ffn_w_down, moe_w_router, moe_w_gate, moe_w_up, moe_w_down):
    B, S, D = x.shape
    assert D == D_MODEL and S % (max(DILATIONS) * BAND_TQ) == 0 and S % PROJ_TM == 0
    depth = w_in.shape[0]
    alpha = (2 * depth) ** 0.25
    n_tok = B * S
    tables = _rope_tables(S)
    x = x.astype(F32)
    for l in range(depth):
        lam_init = 0.8 - 0.6 * math.exp(-0.3 * l)
        moe = l % 2 == 1
        qkv, *res_major = _project(x, w_in, l, tables)
        oa = _attn_a(qkv, diff_lambda[l], diff_norm_g[l], lam_init).reshape(n_tok, WIDTH_A)
        band = [_attn_b(qkv[:, None], DILATIONS[0], BAND_LO // WIDTH_B)]
        band += [_attn_b(src, d, 0) for d, src in zip(DILATIONS[1:], res_major)]
        oc = _attn_c(qkv, na_rpb[l]).reshape(n_tok, WIDTH_C)
        x1 = _outproj(x.reshape(n_tok, D), oa, [o for o, _ in band], [s for _, s in band], oc,
                      beta_mix[l], w_out, l, ln1_g[l], ln1_b[l], alpha, tile_out=moe)
        j = l // 2
        if not moe:
            x = _dense_ffn(x1, ffn_w_gate[j], ffn_w_up[j], ffn_w_down[j], ln2_g[l], ln2_b[l], alpha)
        else:
            meta, fields, counts = _router(x1, moe_w_router[j])
            plan = _moe_plan(fields, counts, n_tok)
            xs = _dispatch(x1, plan.dest, plan.pads, plan.n_slots)
            ys = _grouped_ffn(xs, moe_w_gate[j], moe_w_up[j], moe_w_down[j],
                              plan.tile_expert, plan.n_tiles, plan.tile_rows)
            x = _combine(ys, x1, plan.dest, meta, ln2_g[l], ln2_b[l], alpha)
        x = x.reshape(B, S, D)
    return x
```

```python
import functools
import math
from typing import NamedTuple

import jax
import jax.numpy as jnp
import numpy as np
from jax import lax
from jax.experimental import pallas as pl
from jax.experimental.pallas import tpu as pltpu

F32 = jnp.float32
BF16 = jnp.bfloat16

D_MODEL = 1024
HEAD_DIM = 64
N_HEADS_A, N_HEADS_B, N_HEADS_C = 4, 6, 6
WIDTH_A, WIDTH_B, WIDTH_C = N_HEADS_A * HEAD_DIM, N_HEADS_B * HEAD_DIM, N_HEADS_C * HEAD_DIM
DIFF_DIM = HEAD_DIM // 2
ROPE_THETA = 500000.0
ROPE_FRACTION = 4
DILATIONS = (1, 4, 16)
assert DILATIONS[0] == 1 and DILATIONS[2] == DILATIONS[1] ** 2
BAND_HALF = 64
GRID_W = 64
NA_ROWS = 8
NA_COLS = 16
N_EXPERTS = 8
TOP_K = 2
LN_EPS = 1e-5
NEG = -1e30
LOG2E = math.log2(math.e)
LN2 = math.log(2.0)

LANES = 128
SUBLANES = 8
VMEM_LIMIT = 56 * 1024 * 1024

PROJ_W = 3 * (WIDTH_A + WIDTH_B + WIDTH_C)
ROPE_COLS = 3 * WIDTH_A + 2 * WIDTH_B
COL_CHUNK = 3 * WIDTH_A
BAND_LO = 3 * WIDTH_A
BAND_COLS = 3 * WIDTH_B


def _cparams(sem):
    return pltpu.CompilerParams(dimension_semantics=sem, vmem_limit_bytes=VMEM_LIMIT)


def _layer_norm(z, g, b):
    mu = jnp.mean(z, axis=-1, keepdims=True)
    zc = z - mu
    var = jnp.mean(zc * zc, axis=-1, keepdims=True)
    return zc * lax.rsqrt(var + LN_EPS) * g + b


def _rope_tables(seq):
    rot_a = DIFF_DIM // ROPE_FRACTION
    rot_b = HEAD_DIM // ROPE_FRACTION
    pos = jnp.arange(seq, dtype=F32)[:, None]
    inv_a = ROPE_THETA ** (-jnp.arange(0, rot_a, 2, dtype=F32) / rot_a)
    inv_b = ROPE_THETA ** (-jnp.arange(0, rot_b, 2, dtype=F32) / rot_b)
    ang_a, ang_b = pos * inv_a[None, :], pos * inv_b[None, :]

    cols = np.arange(ROPE_COLS)
    in_a = cols < 3 * WIDTH_A
    sub_a = cols // WIDTH_A
    within_a = cols % DIFF_DIM
    rot_mask_a = in_a & (sub_a < 2) & (within_a < rot_a)
    cb = cols - 3 * WIDTH_A
    sub_b = cb // WIDTH_B
    within_b = cb % HEAD_DIM
    rot_mask_b = (~in_a) & (within_b < rot_b)

    first_half = np.where(in_a, within_a < rot_a // 2, within_b < rot_b // 2)
    rot = rot_mask_a | rot_mask_b
    scale = np.ones(ROPE_COLS, np.float32)
    scale[in_a & (sub_a == 0)] = DIFF_DIM ** -0.5 * LOG2E
    scale[(~in_a) & (sub_b == 0)] = HEAD_DIM ** -0.5 * LOG2E

    def spread(tab_a, tab_b, fill):
        def groups(tab, width, reps):
            pad = jnp.full((seq, width - 2 * tab.shape[1]), fill, F32)
            return jnp.tile(jnp.concatenate([tab, tab, pad], axis=1), (1, reps))
        return jnp.concatenate([groups(tab_a, DIFF_DIM, 2 * WIDTH_A // DIFF_DIM),
                                jnp.full((seq, WIDTH_A), fill, F32),
                                groups(tab_b, HEAD_DIM, 2 * WIDTH_B // HEAD_DIM)], axis=1)

    cm = spread(jnp.cos(ang_a), jnp.cos(ang_b), 1.0) * jnp.asarray(scale)[None, :]
    sn = spread(jnp.sin(ang_a), jnp.sin(ang_b), 0.0) * jnp.asarray(scale)[None, :]
    ma = np.where(rot & first_half, -1.0, 0.0).astype(np.float32)[None, :]
    mb = np.where(rot & ~first_half, 1.0, 0.0).astype(np.float32)[None, :]
    return cm, sn, jnp.asarray(ma), jnp.asarray(mb)


PROJ_TM = 512
W_CAST_ROWS = 128


def _cast_rows_to_bf16(src_ref, dst_ref, rows):
    def body(i, c):
        r = pl.multiple_of(i * W_CAST_ROWS, W_CAST_ROWS)
        dst_ref[pl.ds(r, W_CAST_ROWS), :] = src_ref[pl.ds(r, W_CAST_ROWS), :].astype(BF16)
        return c
    lax.fori_loop(0, rows // W_CAST_ROWS, body, 0)


def _proj_kernel(x_ref, w_ref, cm_ref, sn_ref, ma_ref, mb_ref, o_ref, *rest):
    res_refs, (wbf_ref, band_ref, split_ref) = rest[:-3], rest[-3:]

    @pl.when((pl.program_id(0) == 0) & (pl.program_id(1) == 0))
    def _():
        _cast_rows_to_bf16(w_ref, wbf_ref, D_MODEL)

    xb = x_ref[0].astype(BF16)
    tm = xb.shape[0]
    n_chunks = PROJ_W // COL_CHUNK
    for c in range(n_chunks):
        lo, hi = c * COL_CHUNK, (c + 1) * COL_CHUNK
        acc = jnp.dot(xb, wbf_ref[:, lo:hi], preferred_element_type=F32)
        if hi <= ROPE_COLS:
            half = (DIFF_DIM if c == 0 else HEAD_DIM) // ROPE_FRACTION // 2
            up = pltpu.roll(acc, COL_CHUNK - half, 1)
            dn = pltpu.roll(acc, half, 1)
            acc = acc * cm_ref[:, lo:hi] + (up * ma_ref[:, lo:hi] + dn * mb_ref[:, lo:hi]) * sn_ref[:, lo:hi]
        elif c == 2:
            lane = lax.broadcasted_iota(jnp.int32, (1, COL_CHUNK), 1)
            acc = acc * jnp.where(lane >= WIDTH_B, HEAD_DIM ** -0.5 * LOG2E, 1.0)
        o_ref[0, :, lo:hi] = acc.astype(BF16)
        for col in range(max(lo, BAND_LO), min(hi, BAND_LO + BAND_COLS), LANES):
            band_ref[(col - BAND_LO) // LANES] = acc[:, col - lo:col - lo + LANES]

    (d1, d2), (r1_ref, r2_ref) = DILATIONS[1:], res_refs
    for page in range(BAND_COLS // LANES):
        cs = slice(page * LANES, (page + 1) * LANES)
        for r in range(d1):
            rows = band_ref[page, pl.ds(r, tm // d1, stride=d1), :]
            r1_ref[0, r, :, cs] = rows.astype(BF16)
            split_ref[page, r] = rows
        for r in range(d2):
            r2_ref[0, r, :, cs] = split_ref[page, r % d1, pl.ds(r // d1, tm // d2, stride=d1), :].astype(BF16)


def _project(x, w, layer, tables):
    B, S, _ = x.shape
    cm, sn, ma, mb = tables
    tm = PROJ_TM
    res_specs = [pl.BlockSpec((1, d, tm // d, BAND_COLS), lambda p, b: (b, 0, p, 0)) for d in DILATIONS[1:]]
    res_shapes = [jax.ShapeDtypeStruct((B, d, S // d, BAND_COLS), BF16) for d in DILATIONS[1:]]
    return pl.pallas_call(
        _proj_kernel,
        grid=(S // tm, B),
        in_specs=[
            pl.BlockSpec((1, tm, D_MODEL), lambda p, b: (b, p, 0)),
            pl.BlockSpec((None, D_MODEL, PROJ_W), lambda p, b: (layer, 0, 0), pipeline_mode=pl.Buffered(1)),
            pl.BlockSpec((tm, ROPE_COLS), lambda p, b: (p, 0)),
            pl.BlockSpec((tm, ROPE_COLS), lambda p, b: (p, 0)),
            pl.BlockSpec((1, ROPE_COLS), lambda p, b: (0, 0)),
            pl.BlockSpec((1, ROPE_COLS), lambda p, b: (0, 0)),
        ],
        out_specs=[pl.BlockSpec((1, tm, PROJ_W), lambda p, b: (b, p, 0))] + res_specs,
        out_shape=[jax.ShapeDtypeStruct((B, S, PROJ_W), BF16)] + res_shapes,
        scratch_shapes=[pltpu.VMEM((D_MODEL, PROJ_W), BF16),
                        pltpu.VMEM((BAND_COLS // LANES, tm, LANES), F32),
                        pltpu.VMEM((BAND_COLS // LANES, DILATIONS[1], tm // DILATIONS[1], LANES), F32)],
        compiler_params=_cparams(("arbitrary", "arbitrary")),
        name="proj_in",
    )(x, w, cm, sn, ma, mb)


ATT_A_TQ = 256
ATT_A_UNROLL = 4


def _nt_dot(a, b):
    return lax.dot_general(a, b, (((1,), (1,)), ((), ())), preferred_element_type=F32)


def _with_ones(v):
    return jnp.concatenate([v, jnp.ones_like(v)], axis=1)


def _softmax_times_v(e, v1):
    nd = jnp.dot(e, v1, preferred_element_type=F32)
    return nd[:, :LANES] * (1.0 / nd[:, LANES:])


def _attn_a_kernel(lam_ref, g_ref, q_ref, k_ref, v_ref, o_ref, *, lam_init, seq):
    lp = lam_ref[...]
    lam = (jnp.exp(jnp.sum(lp[0:1] * lp[1:2], axis=-1, keepdims=True))
           - jnp.exp(jnp.sum(lp[2:3] * lp[3:4], axis=-1, keepdims=True)) + lam_init)
    lane = lax.broadcasted_iota(jnp.int32, (1, LANES), 1)
    first = lane < HEAD_DIM
    gain = g_ref[...] * (1.0 - lam_init)
    tq = ATT_A_TQ

    def qblock(i, carry):
        r0 = pl.multiple_of(i * tq, tq)
        q = q_ref[0, pl.ds(r0, tq), :]
        k = k_ref[0]
        v1 = _with_ones(v_ref[0])
        outs = []
        for hh in range(2):
            maps = []
            for m in range(2):
                lo = hh * HEAD_DIM + m * DIFF_DIM
                qm = jnp.where((lane >= lo) & (lane < lo + DIFF_DIM), q, jnp.zeros_like(q))
                s = _nt_dot(qm, k)
                e = jnp.exp2(s - jnp.max(s, axis=-1, keepdims=True)).astype(BF16)
                maps.append(_softmax_times_v(e, v1))
            outs.append(maps[0] - lam * maps[1])
        o = jnp.where(first, outs[0], outs[1])
        sq = o * o
        ss0 = jnp.sum(jnp.where(first, sq, 0.0), axis=-1, keepdims=True)
        ss1 = jnp.sum(jnp.where(first, 0.0, sq), axis=-1, keepdims=True)
        ms = jnp.where(first, ss0, ss1) * (1.0 / HEAD_DIM)
        o_ref[0, pl.ds(r0, tq), :] = (o * lax.rsqrt(ms + LN_EPS) * gain).astype(BF16)
        return carry

    lax.fori_loop(0, seq // tq, qblock, 0, unroll=ATT_A_UNROLL)


def _attn_a(qkv, diff_lambda, norm_g, lam_init):
    B, S, _ = qkv.shape
    n_pairs = WIDTH_A // LANES
    g2 = jnp.tile(norm_g.astype(F32), LANES // HEAD_DIM)[None, :]
    blk = lambda off: pl.BlockSpec((1, S, LANES), lambda b, h, off=off: (b, 0, off + h))
    return pl.pallas_call(
        functools.partial(_attn_a_kernel, lam_init=lam_init, seq=S),
        grid=(B, n_pairs),
        in_specs=[
            pl.BlockSpec((4, DIFF_DIM), lambda b, h: (0, 0)),
            pl.BlockSpec((1, LANES), lambda b, h: (0, 0)),
            blk(0), blk(n_pairs), blk(2 * n_pairs),
        ],
        out_specs=pl.BlockSpec((1, S, LANES), lambda b, h: (b, 0, h)),
        out_shape=jax.ShapeDtypeStruct((B, S, WIDTH_A), BF16),
        compiler_params=_cparams(("arbitrary", "arbitrary")),
        name="attn_diff",
    )(diff_lambda.astype(F32), g2, qkv, qkv, qkv)


BAND_TQ = 128
BAND_CHAINS = 12


def _stack_pair(q, lane):
    z = jnp.zeros_like(q)
    return jnp.concatenate([jnp.where(lane < HEAD_DIM, q, z), jnp.where(lane < HEAD_DIM, z, q)], axis=0)


def _attn_b_kernel(q_ref, k_ref, v_ref, o_ref, lse_ref, *, length, dil):
    tq = BAND_TQ
    win = min(tq + 2 * BAND_HALF, length)
    lane = lax.broadcasted_iota(jnp.int32, (1, LANES), 1)

    blocks = [(i * tq, min(max(i * tq - BAND_HALF, 0), length - win)) for i in range(length // tq)]
    row = lax.broadcasted_iota(jnp.int32, (2 * tq, win), 0)
    delta = lax.broadcasted_iota(jnp.int32, (2 * tq, win), 1) - jnp.where(row >= tq, row - tq, row)
    masks = {off: jnp.where(jnp.abs(delta - off) <= BAND_HALF, 0.0, NEG) for off in {q0 - k0 for q0, k0 in blocks}}

    def residue(r, carry):
        for q0, k0 in blocks:
            lse_tile = jnp.zeros((tq, LANES), F32)
            for pr in range(WIDTH_B // LANES):
                cs = slice(pr * LANES, (pr + 1) * LANES)
                q2 = _stack_pair(q_ref[0, r, q0:q0 + tq, cs], lane)
                s = _nt_dot(q2, k_ref[0, r, k0:k0 + win, cs]) + masks[q0 - k0]
                mx = jnp.max(s, axis=-1, keepdims=True)
                e = jnp.exp2(s - mx).astype(BF16)
                nd = jnp.dot(e, _with_ones(v_ref[0, r, k0:k0 + win, cs]), preferred_element_type=F32)
                l = nd[:, LANES:]
                o2 = nd[:, :LANES] * (1.0 / l)
                o_ref[0, r, q0:q0 + tq, cs] = jnp.where(lane < HEAD_DIM, o2[:tq], o2[tq:]).astype(BF16)
                lse2 = (mx + jnp.log2(l)) * LN2
                lse_tile = jnp.where(lane == 2 * pr, lse2[:tq], lse_tile)
                lse_tile = jnp.where(lane == 2 * pr + 1, lse2[tq:], lse_tile)
            lse_ref[0, r, q0:q0 + tq, :] = lse_tile
        return carry

    chains = len(blocks) * (WIDTH_B // LANES)
    lax.fori_loop(0, dil, residue, 0, unroll=min(dil, max(1, BAND_CHAINS // chains)))


def _attn_b(src, dil, first_blk):
    B, _, L, _ = src.shape
    blk = lambda off: pl.BlockSpec((1, dil, L, WIDTH_B), lambda b, off=off: (b, 0, 0, first_blk + off))
    return pl.pallas_call(
        functools.partial(_attn_b_kernel, length=L, dil=dil),
        grid=(B,),
        in_specs=[blk(0), blk(1), blk(2)],
        out_specs=[pl.BlockSpec((1, dil, L, WIDTH_B), lambda b: (b, 0, 0, 0)),
                   pl.BlockSpec((1, dil, L, LANES), lambda b: (b, 0, 0, 0))],
        out_shape=[jax.ShapeDtypeStruct((B, dil, L, WIDTH_B), BF16),
                   jax.ShapeDtypeStruct((B, dil, L, LANES), F32)],
        compiler_params=_cparams(("arbitrary",)),
        name=f"attn_band_d{dil}",
    )(src, src, src)


N_RPB_ROWS = 2 * NA_ROWS - 1
N_RPB_COLS = 2 * NA_COLS - 1
NA_KEYS = NA_ROWS * GRID_W
N_BIAS_VARIANTS = NA_ROWS
NA_ROWS_PER_STEP = 4


def _attn_c_kernel(u_ref, q_ref, k_ref, v_ref, o_ref, bias_ref, *, rows):
    lane = lax.broadcasted_iota(jnp.int32, (1, LANES), 1)
    n_pairs = WIDTH_C // LANES

    @pl.when(pl.program_id(0) == 0)
    def _():
        wide = N_RPB_ROWS * GRID_W
        c = lax.broadcasted_iota(jnp.int32, (GRID_W, wide), 0)
        kc = lax.broadcasted_iota(jnp.int32, (GRID_W, wide), 1) % GRID_W
        tidx = jnp.clip(kc - c + NA_COLS - 1, 0, N_RPB_COLS - 1)
        c5 = lax.broadcasted_iota(jnp.int32, (GRID_W, NA_KEYS), 0)
        kc5 = lax.broadcasted_iota(jnp.int32, (GRID_W, NA_KEYS), 1) % GRID_W
        cstart = jnp.clip(c5 - NA_COLS // 2, 0, GRID_W - NA_COLS)
        valid = (kc5 >= cstart) & (kc5 < cstart + NA_COLS)
        for h in range(N_HEADS_C):
            def tbody(t, acc, h=h):
                row = u_ref[h, pl.ds(t, 1), :]
                return jnp.where(tidx == t, row, acc)
            toep = lax.fori_loop(0, N_RPB_COLS, tbody, jnp.zeros((GRID_W, wide), F32))
            for var in range(N_BIAS_VARIANTS):
                tile = jnp.where(valid, toep[:, var * GRID_W:var * GRID_W + NA_KEYS] * LOG2E, NEG)
                bias_ref[var, h // 2, (h % 2) * GRID_W:(h % 2 + 1) * GRID_W, :] = tile

    def rbody(step, carry):
        for sub in range(NA_ROWS_PER_STEP):
            r = step * NA_ROWS_PER_STEP + sub
            rs = jnp.clip(r - NA_ROWS // 2, 0, rows - NA_ROWS)
            var = rs - r + NA_ROWS - 1
            q0 = pl.multiple_of(r * GRID_W, GRID_W)
            k0 = pl.multiple_of(rs * GRID_W, GRID_W)
            for pr in range(n_pairs):
                cs = slice(pr * LANES, (pr + 1) * LANES)
                q2 = _stack_pair(q_ref[0, pl.ds(q0, GRID_W), cs], lane)
                s = _nt_dot(q2, k_ref[0, pl.ds(k0, NA_KEYS), cs]) + bias_ref[var, pr]
                e = jnp.exp2(s - jnp.max(s, axis=-1, keepdims=True)).astype(BF16)
                o2 = _softmax_times_v(e, _with_ones(v_ref[0, pl.ds(k0, NA_KEYS), cs]))
                o_ref[0, pl.ds(q0, GRID_W), cs] = jnp.where(lane < HEAD_DIM, o2[:GRID_W], o2[GRID_W:]).astype(BF16)
        return carry

    lax.fori_loop(0, rows // NA_ROWS_PER_STEP, rbody, 0)


def _attn_c(qkv, rpb):
    B, S, _ = qkv.shape
    rows = S // GRID_W
    assert rows >= NA_ROWS
    u = jnp.repeat(jnp.transpose(rpb.astype(F32), (0, 2, 1)), GRID_W, axis=2)
    q_blk = (3 * WIDTH_A + 3 * WIDTH_B) // WIDTH_C
    blk = lambda off: pl.BlockSpec((1, S, WIDTH_C), lambda b, off=off: (b, 0, off))
    return pl.pallas_call(
        functools.partial(_attn_c_kernel, rows=rows),
        grid=(B,),
        in_specs=[pl.BlockSpec((N_HEADS_C, N_RPB_COLS, N_RPB_ROWS * GRID_W), lambda b: (0, 0, 0)),
                  blk(q_blk), blk(q_blk + 1), blk(q_blk + 2)],
        out_specs=pl.BlockSpec((1, S, WIDTH_C), lambda b: (b, 0, 0)),
        out_shape=jax.ShapeDtypeStruct((B, S, WIDTH_C), BF16),
        scratch_shapes=[pltpu.VMEM((N_BIAS_VARIANTS, WIDTH_C // LANES, LANES, NA_KEYS), F32)],
        compiler_params=_cparams(("arbitrary",)),
        name="attn_nbr",
    )(u, qkv, qkv, qkv)


OUT_TM = 512
OUT_SPLIT = 2


def _split_bf16(x):
    hi = x.astype(BF16)
    return hi, (x - hi.astype(F32)).astype(BF16)


def _outproj_kernel(x_ref, oa_ref, ob0_ref, ob1_ref, ob2_ref, l0_ref, l1_ref, l2_ref, oc_ref,
                    beta_ref, w_ref, g_ref, b_ref, e_ref, *rest, alpha, route):
    if route:
        wr_ref, o_ref, meta_ref, fields_ref, cnt_ref, wbf_ref, il_ref, carry_ref = rest
    else:
        o_ref, wbf_ref, il_ref = rest

    @pl.when(pl.program_id(0) == 0)
    def _():
        _cast_rows_to_bf16(w_ref, wbf_ref, D_MODEL)
        if route:
            carry_ref[...] = jnp.zeros_like(carry_ref)

    tm = x_ref.shape[0]
    n_pages = WIDTH_B // LANES
    for slot, (dil, ob_r, l_r) in enumerate(zip(DILATIONS[1:], (ob1_ref, ob2_ref), (l1_ref, l2_ref))):
        for r in range(dil):
            rows = pl.ds(r, tm // dil, stride=dil)
            for page in range(n_pages):
                il_ref[slot, page, rows, :] = ob_r[0, r, :, page * LANES:(page + 1) * LANES].astype(F32)
            il_ref[slot, n_pages, rows, :] = l_r[0, r]

    beta = beta_ref[...]
    half = tm // OUT_SPLIT
    for part in range(OUT_SPLIT):
        rs = slice(part * half, (part + 1) * half)
        l0, l1, l2 = l0_ref[0, 0, rs, :], il_ref[0, n_pages, rs, :], il_ref[1, n_pages, rs, :]
        lm = jnp.maximum(jnp.maximum(l0, l1), l2)
        e0, e1, e2 = jnp.exp(l0 - lm), jnp.exp(l1 - lm), jnp.exp(l2 - lm)
        inv = 1.0 / (e0 + e1 + e2)
        wexp = [jnp.dot(jnp.concatenate(_split_bf16(e * inv), axis=1), e_ref[...], preferred_element_type=F32)
                for e in (e0, e1, e2)]
        pages = []
        for page in range(n_pages):
            cs = slice(page * LANES, (page + 1) * LANES)
            branch_vals = (ob0_ref[0, 0, rs, cs].astype(F32), il_ref[0, page, rs, :], il_ref[1, page, rs, :])
            pages.append(sum(w[:, cs] * val for w, val in zip(wexp, branch_vals)))
        ob = jnp.concatenate(pages, axis=1)

        ma = (oa_ref[rs, :].astype(F32) * beta[:, :WIDTH_A]).astype(BF16)
        mb = (ob * beta[:, WIDTH_A:WIDTH_A + WIDTH_B]).astype(BF16)
        mc = (oc_ref[rs, :].astype(F32) * beta[:, WIDTH_A + WIDTH_B:]).astype(BF16)
        acc = jnp.dot(jnp.concatenate([ma, mb, mc], axis=1), wbf_ref[...], preferred_element_type=F32)
        y = _layer_norm(alpha * x_ref[rs, :] + acc, g_ref[...], b_ref[...])
        if route:
            _store_token_tiles(o_ref, y, row0=part * half)
            fields = _route_rows(y, wr_ref, carry_ref)
            fields_ref[:, rs] = fields
            meta_ref[rs, :] = jnp.concatenate([fields, jnp.zeros((LANES - SUBLANES, half), F32)], axis=0).T
        else:
            o_ref[rs, :] = y
    if route:
        cnt_ref[...] = carry_ref[...]


def _store_token_tiles(ref, y, row0=0):
    rows = y.shape[0]
    for j in range(D_MODEL // LANES):
        ref[pl.ds(row0 * SUBLANES + j, rows, stride=SUBLANES), :] = y[:, j * LANES:(j + 1) * LANES]


def _load_token_tiles(ref, rows, row0=0):
    return jnp.concatenate(
        [ref[pl.ds(row0 * SUBLANES + j, rows, stride=SUBLANES), :] for j in range(D_MODEL // LANES)], axis=1)


def _outproj(x2, oa, obs, lses, oc, beta, w, layer, g, b, alpha, w_router=None):
    route = w_router is not None
    N = x2.shape[0]
    tm = OUT_TM
    steps_per_seq = obs[0].shape[2] // tm
    expand = np.zeros((2 * LANES, WIDTH_B), np.float32)
    for h in range(N_HEADS_B):
        expand[h, h * HEAD_DIM:(h + 1) * HEAD_DIM] = 1.0
        expand[LANES + h, h * HEAD_DIM:(h + 1) * HEAD_DIM] = 1.0
    row = lambda w_: pl.BlockSpec((tm, w_), lambda i: (i, 0))
    full = lambda r, c, **kw: pl.BlockSpec((r, c), lambda i: (0, 0), **kw)
    res = lambda d, w_: pl.BlockSpec((1, d, tm // d, w_),
                                     lambda i: (i // steps_per_seq, 0, i % steps_per_seq, 0))
    in_specs = ([row(D_MODEL), row(WIDTH_A)] + [res(d, WIDTH_B) for d in DILATIONS]
                + [res(d, LANES) for d in DILATIONS] + [row(WIDTH_C), full(1, D_MODEL),
                   pl.BlockSpec((None, D_MODEL, D_MODEL), lambda i: (layer, 0, 0), pipeline_mode=pl.Buffered(1)),
                   full(1, D_MODEL), full(1, D_MODEL), full(2 * LANES, WIDTH_B)])
    args = [x2, oa, obs[0], obs[1], obs[2], lses[0], lses[1], lses[2], oc,
            beta[None, :].astype(F32), w, g[None, :].astype(F32), b[None, :].astype(F32), jnp.asarray(expand, BF16)]
    scratch = [pltpu.VMEM((D_MODEL, D_MODEL), BF16),
               pltpu.VMEM((len(DILATIONS) - 1, WIDTH_B // LANES + 1, tm, LANES), F32)]
    if route:
        wr = jnp.zeros((D_MODEL, LANES), F32).at[:, :N_EXPERTS].set(w_router.astype(F32))
        wr_hi = wr.astype(BF16)
        in_specs.append(full(D_MODEL, 2 * LANES))
        args.append(jnp.concatenate([wr_hi, (wr - wr_hi.astype(F32)).astype(BF16)], axis=1))
        out_specs = [pl.BlockSpec((tm * SUBLANES, LANES), lambda i: (i, 0)), row(LANES),
                     pl.BlockSpec((SUBLANES, tm), lambda i: (0, i)), full(ROUTE_ROWS, LANES)]
        out_shape = [jax.ShapeDtypeStruct((N * SUBLANES, LANES), F32), jax.ShapeDtypeStruct((N, LANES), F32),
                     jax.ShapeDtypeStruct((SUBLANES, N), F32), jax.ShapeDtypeStruct((ROUTE_ROWS, LANES), F32)]
        scratch.append(pltpu.VMEM((ROUTE_ROWS, LANES), F32))
    else:
        out_specs = row(D_MODEL)
        out_shape = jax.ShapeDtypeStruct((N, D_MODEL), F32)
    return pl.pallas_call(
        functools.partial(_outproj_kernel, alpha=alpha, route=route),
        grid=(N // tm,),
        in_specs=in_specs,
        out_specs=out_specs,
        out_shape=out_shape,
        scratch_shapes=scratch,
        compiler_params=_cparams(("arbitrary",)),
        name="merge_outproj_ln_route" if route else "merge_outproj_ln",
    )(*args)


CAST_ROWS = 256
DENSE_TM = 512
DENSE_TF = 256


def _cast_kernel(w_ref, o_ref):
    o_ref[...] = w_ref[...].astype(BF16)


def _to_bf16(w):
    rows, cols = w.shape
    return pl.pallas_call(
        _cast_kernel,
        grid=(rows // CAST_ROWS,),
        in_specs=[pl.BlockSpec((CAST_ROWS, cols), lambda i: (i, 0))],
        out_specs=pl.BlockSpec((CAST_ROWS, cols), lambda i: (i, 0)),
        out_shape=jax.ShapeDtypeStruct((rows, cols), BF16),
        compiler_params=_cparams(("arbitrary",)),
        name="cast_bf16",
    )(w)


def _dense_ffn_kernel(x_ref, wg_ref, wu_ref, wd_ref, g_ref, b_ref, o_ref, h_ref, *, alpha):
    x = x_ref[...]
    xb = x.astype(BF16)
    for c in range(h_ref.shape[1] // DENSE_TF):
        cs = slice(c * DENSE_TF, (c + 1) * DENSE_TF)
        gate = jnp.dot(xb, wg_ref[:, cs], preferred_element_type=F32)
        up = jnp.dot(xb, wu_ref[:, cs], preferred_element_type=F32)
        h_ref[:, cs] = (gate * jax.nn.sigmoid(gate) * up).astype(BF16)
    y = jnp.dot(h_ref[...], wd_ref[...], preferred_element_type=F32)
    o_ref[...] = _layer_norm(alpha * x + y, g_ref[...], b_ref[...])


def _dense_ffn(x, w_gate, w_up, w_down, g, b, alpha):
    n_tok = x.shape[0]
    ff = w_gate.shape[1]
    tm = DENSE_TM
    row = pl.BlockSpec((tm, D_MODEL), lambda i: (i, 0))
    resident = lambda r, c: pl.BlockSpec((r, c), lambda i: (0, 0), pipeline_mode=pl.Buffered(1))
    return pl.pallas_call(
        functools.partial(_dense_ffn_kernel, alpha=alpha),
        grid=(n_tok // tm,),
        in_specs=[row, resident(D_MODEL, ff), resident(D_MODEL, ff), resident(ff, D_MODEL),
                  pl.BlockSpec((1, D_MODEL), lambda i: (0, 0)), pl.BlockSpec((1, D_MODEL), lambda i: (0, 0))],
        out_specs=row,
        out_shape=jax.ShapeDtypeStruct((n_tok, D_MODEL), F32),
        scratch_shapes=[pltpu.VMEM((tm, ff), BF16)],
        compiler_params=_cparams(("arbitrary",)),
        name="swiglu_dense",
    )(x, _to_bf16(w_gate), _to_bf16(w_up), _to_bf16(w_down), g[None, :].astype(F32), b[None, :].astype(F32))


FFN_TM = 1024
FFN_TF = 512
FFN_SUB = 256


def _ffn_kernel(te_ref, nt_ref, rows_ref, x_ref, wg_ref, wu_ref, wd_ref, o_ref, xb_ref, acc_ref, *, n_chunks):
    i, j = pl.program_id(0), pl.program_id(1)
    tm = xb_ref.shape[0]

    def swiglu_rows(rs):
        xb = xb_ref[rs, :]
        gate = jnp.dot(xb, wg_ref[...].astype(BF16), preferred_element_type=F32)
        up = jnp.dot(xb, wu_ref[...].astype(BF16), preferred_element_type=F32)
        h = (gate * jax.nn.sigmoid(gate) * up).astype(BF16)
        acc_ref[rs, :] += jnp.dot(h, wd_ref[...].astype(BF16), preferred_element_type=F32)

    @pl.when(i < nt_ref[0])
    def _():
        @pl.when(j == 0)
        def _():
            xb_ref[...] = _load_token_tiles(x_ref, tm).astype(BF16)
            acc_ref[...] = jnp.zeros_like(acc_ref)

        sparse_tile = rows_ref[i] <= tm - FFN_SUB

        @pl.when(jnp.logical_not(sparse_tile))
        def _():
            swiglu_rows(slice(None))

        for sb in range(tm // FFN_SUB - 1):
            @pl.when(sparse_tile & (sb * FFN_SUB < rows_ref[i]))
            def _(sb=sb):
                swiglu_rows(slice(sb * FFN_SUB, (sb + 1) * FFN_SUB))

        @pl.when(j == n_chunks - 1)
        def _():
            _store_token_tiles(o_ref, acc_ref[...])

    @pl.when((i >= nt_ref[0]) & (j == n_chunks - 1))
    def _():
        o_ref[...] = jnp.zeros_like(o_ref)


def _grouped_ffn(x, w_gate, w_up, w_down, tile_expert, n_tiles, tile_rows):
    tm, tf = FFN_TM, FFN_TF
    n_chunks = w_gate.shape[-1] // tf
    max_tiles = tile_expert.shape[0]

    def tile_ix(i, nt):
        return jnp.minimum(i, nt[0] - 1)

    def chunk_ix(i, j, nt):
        return jnp.where(i < nt[0], j, n_chunks - 1)

    blk = (tm * SUBLANES, LANES)
    in_specs = [
        pl.BlockSpec(blk, lambda i, j, te, nt, tr: (tile_ix(i, nt), 0)),
        pl.BlockSpec((None, D_MODEL, tf), lambda i, j, te, nt, tr: (te[tile_ix(i, nt)], 0, chunk_ix(i, j, nt))),
        pl.BlockSpec((None, D_MODEL, tf), lambda i, j, te, nt, tr: (te[tile_ix(i, nt)], 0, chunk_ix(i, j, nt))),
        pl.BlockSpec((None, tf, D_MODEL), lambda i, j, te, nt, tr: (te[tile_ix(i, nt)], chunk_ix(i, j, nt), 0)),
    ]
    return pl.pallas_call(
        functools.partial(_ffn_kernel, n_chunks=n_chunks),
        grid_spec=pltpu.PrefetchScalarGridSpec(
            num_scalar_prefetch=3,
            grid=(max_tiles, n_chunks),
            in_specs=in_specs,
            out_specs=pl.BlockSpec(blk, lambda i, j, te, nt, tr: (i, 0)),
            scratch_shapes=[pltpu.VMEM((tm, D_MODEL), BF16), pltpu.VMEM((tm, D_MODEL), F32)],
        ),
        out_shape=jax.ShapeDtypeStruct(x.shape, F32),
        compiler_params=_cparams(("arbitrary", "arbitrary")),
        name="swiglu_grouped",
    )(tile_expert, n_tiles, tile_rows, x, w_gate, w_up, w_down)


META_E1, META_E2, META_R1, META_R2, META_G1, META_G2 = range(6)


ROUTE_ROWS = 16


def _route_rows(x, wr_ref, carry_ref):
    tb = x.shape[0]
    xh, xl = _split_bf16(x)
    hi_lo = jnp.dot(xh, wr_ref[...], preferred_element_type=F32)
    logits = hi_lo[:, :LANES] + hi_lo[:, LANES:] + jnp.dot(xl, wr_ref[:, :LANES], preferred_element_type=F32)
    logits = logits.T[:ROUTE_ROWS]
    row = lax.broadcasted_iota(jnp.int32, (ROUTE_ROWS, tb), 0).astype(F32)
    logits = jnp.where(row < N_EXPERTS, logits, -jnp.inf)
    m1 = jnp.max(logits, axis=0, keepdims=True)
    i1 = jnp.min(jnp.where(logits == m1, row, float(ROUTE_ROWS)), axis=0, keepdims=True)
    rest = jnp.where(row == i1, -jnp.inf, logits)
    m2 = jnp.max(rest, axis=0, keepdims=True)
    i2 = jnp.min(jnp.where(rest == m2, row, float(ROUTE_ROWS)), axis=0, keepdims=True)
    e21 = jnp.exp(m2 - m1)
    g1 = 1.0 / (1.0 + e21)
    g2 = e21 * g1

    member = ((row == i1) | (row == i2)).astype(F32)
    r_i = lax.broadcasted_iota(jnp.int32, (tb, tb), 0)
    c_i = lax.broadcasted_iota(jnp.int32, (tb, tb), 1)
    earlier = (r_i < c_i).astype(BF16)
    before = jnp.dot(member.astype(BF16), earlier, preferred_element_type=F32)
    rank = carry_ref[:, 0:1] + before
    r1 = jnp.sum(jnp.where(row == i1, rank, 0.0), axis=0, keepdims=True)
    r2 = jnp.sum(jnp.where(row == i2, rank, 0.0), axis=0, keepdims=True)
    carry_ref[...] += jnp.sum(member, axis=1, keepdims=True)

    fields = [None] * SUBLANES
    for slot, val in ((META_E1, i1), (META_E2, i2), (META_R1, r1), (META_R2, r2), (META_G1, g1), (META_G2, g2)):
        fields[slot] = val
    return jnp.concatenate([f if f is not None else jnp.zeros((1, tb), F32) for f in fields], axis=0)


MOVE_TB = 1024
PAD_CHUNK = 64
DMA_ISSUE_UNROLL = 8


def _tile_rows(ref, idx):
    return ref.at[pl.ds(pl.multiple_of(idx * SUBLANES, SUBLANES), SUBLANES), :]


def _zero_fill(pad_ref, xs_hbm, zbuf, zsem):
    zbuf[...] = jnp.zeros_like(zbuf)
    one = _tile_rows(zbuf, 0)
    for wait in (False, True):
        for seg in range(pad_ref.shape[1]):
            start, n = pad_ref[0, seg], pad_ref[1, seg]
            n_single = n % PAD_CHUNK

            def single(t, c):
                cp = pltpu.make_async_copy(one, _tile_rows(xs_hbm, start + t), zsem)
                cp.wait() if wait else cp.start()
                return c

            def bulk(t, c):
                r0 = pl.multiple_of((start + n_single + t * PAD_CHUNK) * SUBLANES, SUBLANES)
                cp = pltpu.make_async_copy(zbuf, xs_hbm.at[pl.ds(r0, PAD_CHUNK * SUBLANES), :], zsem)
                cp.wait() if wait else cp.start()
                return c

            lax.fori_loop(0, n_single, single, 0)
            lax.fori_loop(0, n // PAD_CHUNK, bulk, 0)


def _dispatch_kernel(dest_ref, pad_ref, x_ref, xs_hbm, zbuf, sem, zsem):
    tb = MOVE_TB

    @pl.when(pl.program_id(0) == 0)
    def _():
        _zero_fill(pad_ref, xs_hbm, zbuf, zsem)

    def body(t, c):
        src = _tile_rows(x_ref, t)
        for k in range(TOP_K):
            pltpu.make_async_copy(src, _tile_rows(xs_hbm, dest_ref[0, 0, TOP_K * t + k]), sem).start(priority=k)
        return c

    lax.fori_loop(0, tb, body, 0, unroll=DMA_ISSUE_UNROLL)
    n_rows = TOP_K * tb * SUBLANES
    pltpu.make_async_copy(xs_hbm.at[pl.ds(0, n_rows), :], xs_hbm.at[pl.ds(0, n_rows), :], sem).wait()


def _dispatch(xt, dest, pads, n_slots):
    n_tok = xt.shape[0] // SUBLANES
    tb = MOVE_TB
    dest3 = dest.reshape(n_tok // tb, 1, TOP_K * tb)
    return pl.pallas_call(
        _dispatch_kernel,
        grid=(n_tok // tb,),
        in_specs=[pl.BlockSpec((1, 1, TOP_K * tb), lambda i: (i, 0, 0), memory_space=pltpu.SMEM),
                  pl.BlockSpec(memory_space=pltpu.SMEM),
                  pl.BlockSpec((tb * SUBLANES, LANES), lambda i: (i, 0))],
        out_specs=pl.BlockSpec(memory_space=pl.ANY),
        out_shape=jax.ShapeDtypeStruct((n_slots * SUBLANES, LANES), F32),
        scratch_shapes=[pltpu.VMEM((PAD_CHUNK * SUBLANES, LANES), F32),
                        pltpu.SemaphoreType.DMA(()), pltpu.SemaphoreType.DMA(())],
        compiler_params=_cparams(("arbitrary",)),
        name="moe_dispatch",
    )(dest3, pads, xt)


COMB_TB = 256


def _combine_kernel(dest_ref, next_ref, y_hbm, x_ref, meta_ref, g_ref, b_ref, o_ref, ybuf, sem, *, alpha, n_steps):
    tb = COMB_TB
    i = pl.program_id(0)
    slot = i % 2

    def gather(d_ref, s):
        def body(t, c):
            for k in range(TOP_K):
                pltpu.make_async_copy(_tile_rows(y_hbm, d_ref[0, 0, TOP_K * t + k]),
                                      _tile_rows(ybuf.at[s], k * tb + t), sem.at[s]).start(priority=k)
            return c
        lax.fori_loop(0, tb, body, 0, unroll=DMA_ISSUE_UNROLL)

    @pl.when(i == 0)
    def _():
        gather(dest_ref, 0)

    @pl.when(i + 1 < n_steps)
    def _():
        gather(next_ref, 1 - slot)

    cur = ybuf.at[slot]
    pltpu.make_async_copy(cur, cur, sem.at[slot]).wait()

    meta = meta_ref[...]
    lane = lax.broadcasted_iota(jnp.int32, meta.shape, 1)
    g1 = jnp.sum(jnp.where(lane == META_G1, meta, 0.0), axis=-1, keepdims=True)
    g2 = jnp.sum(jnp.where(lane == META_G2, meta, 0.0), axis=-1, keepdims=True)
    x = _load_token_tiles(x_ref, tb)
    f = g1 * _load_token_tiles(cur, tb) + g2 * _load_token_tiles(cur, tb, row0=tb)
    o_ref[...] = _layer_norm(alpha * x + f, g_ref[...], b_ref[...])


def _combine(yt, xt, dest, meta, g, b, alpha):
    n_tok = xt.shape[0] // SUBLANES
    tb = COMB_TB
    n_steps = n_tok // tb
    dest3 = dest.reshape(n_steps, 1, TOP_K * tb)
    return pl.pallas_call(
        functools.partial(_combine_kernel, alpha=alpha, n_steps=n_steps),
        grid=(n_steps,),
        in_specs=[pl.BlockSpec((1, 1, TOP_K * tb), lambda i: (i, 0, 0), memory_space=pltpu.SMEM),
                  pl.BlockSpec((1, 1, TOP_K * tb), lambda i: (jnp.minimum(i + 1, n_steps - 1), 0, 0),
                               memory_space=pltpu.SMEM),
                  pl.BlockSpec(memory_space=pl.ANY),
                  pl.BlockSpec((tb * SUBLANES, LANES), lambda i: (i, 0)),
                  pl.BlockSpec((tb, LANES), lambda i: (i, 0)),
                  pl.BlockSpec((1, D_MODEL), lambda i: (0, 0)),
                  pl.BlockSpec((1, D_MODEL), lambda i: (0, 0))],
        out_specs=pl.BlockSpec((tb, D_MODEL), lambda i: (i, 0)),
        out_shape=jax.ShapeDtypeStruct((n_tok, D_MODEL), F32),
        scratch_shapes=[pltpu.VMEM((2, TOP_K * tb * SUBLANES, LANES), F32), pltpu.SemaphoreType.DMA((2,))],
        compiler_params=_cparams(("arbitrary",)),
        name="moe_combine_ln",
    )(dest3, dest3, yt, xt, meta, g[None, :].astype(F32), b[None, :].astype(F32))


class MoePlan(NamedTuple):
    dest: jax.Array
    tile_expert: jax.Array
    tile_rows: jax.Array
    n_tiles: jax.Array
    pads: jax.Array
    n_slots: int


def _moe_plan(fields, counts, n_tok):
    tm = FFN_TM
    max_tiles = n_tok * TOP_K // tm + N_EXPERTS
    cnt = counts[:N_EXPERTS, 0].astype(jnp.int32)
    tiles = (cnt + tm - 1) // tm
    tile_end = jnp.cumsum(tiles)
    start = (tile_end - tiles) * tm
    e = fields[META_E1:META_E2 + 1].astype(jnp.int32)
    r = fields[META_R1:META_R2 + 1].astype(jnp.int32)
    experts = jnp.arange(N_EXPERTS, dtype=jnp.int32)[:, None, None]
    group_start = jnp.sum(jnp.where(e[None] == experts, start[:, None, None], 0), axis=0)
    dest = (group_start + r).T.reshape(-1)
    tile_ids = jnp.arange(max_tiles, dtype=jnp.int32)
    tile_expert = jnp.minimum(jnp.sum((tile_ids[:, None] >= tile_end[None, :]).astype(jnp.int32), axis=1),
                              N_EXPERTS - 1)
    first_tile = (tile_end - tiles)[tile_expert]
    tile_rows = jnp.clip(cnt[tile_expert] - (tile_ids - first_tile) * tm, 0, tm).astype(jnp.int32)
    n_tiles = tile_end[-1:]
    pads = jnp.stack([jnp.concatenate([start + cnt, n_tiles * tm]),
                      jnp.concatenate([tiles * tm - cnt, (max_tiles - n_tiles) * tm])]).astype(jnp.int32)
    return MoePlan(dest, tile_expert, tile_rows, n_tiles.astype(jnp.int32), pads, max_tiles * tm)


def kernel(x, w_in, w_out, beta_mix, diff_lambda, diff_norm_g, na_rpb, ln1_g, ln1_b, ln2_g, ln2_b,
           ffn_w_gate, ffn_w_up, ffn_w_down, moe_w_router, moe_w_gate, moe_w_up, moe_w_down):
    B, S, D = x.shape
    assert D == D_MODEL and S % (max(DILATIONS) * BAND_TQ) == 0 and S % PROJ_TM == 0
    depth = w_in.shape[0]
    alpha = (2 * depth) ** 0.25
    n_tok = B * S
    tables = _rope_tables(S)
    x = x.astype(F32)
    for l in range(depth):
        lam_init = 0.8 - 0.6 * math.exp(-0.3 * l)
        moe = l % 2 == 1
        qkv, *res_major = _project(x, w_in, l, tables)
        oa = _attn_a(qkv, diff_lambda[l], diff_norm_g[l], lam_init).reshape(n_tok, WIDTH_A)
        band = [_attn_b(qkv[:, None], DILATIONS[0], BAND_LO // WIDTH_B)]
        band += [_attn_b(src, d, 0) for d, src in zip(DILATIONS[1:], res_major)]
        oc = _attn_c(qkv, na_rpb[l]).reshape(n_tok, WIDTH_C)
        j = l // 2
        mixed = _outproj(x.reshape(n_tok, D), oa, [o for o, _ in band], [s for _, s in band], oc,
                         beta_mix[l], w_out, l, ln1_g[l], ln1_b[l], alpha,
                         w_router=moe_w_router[j] if moe else None)
        if not moe:
            x = _dense_ffn(mixed, ffn_w_gate[j], ffn_w_up[j], ffn_w_down[j], ln2_g[l], ln2_b[l], alpha)
        else:
            x1, meta, fields, counts = mixed
            plan = _moe_plan(fields, counts, n_tok)
            xs = _dispatch(x1, plan.dest, plan.pads, plan.n_slots)
            ys = _grouped_ffn(xs, moe_w_gate[j], moe_w_up[j], moe_w_down[j],
                              plan.tile_expert, plan.n_tiles, plan.tile_rows)
            x = _combine(ys, x1, plan.dest, meta, ln2_g[l], ln2_b[l], alpha)
        x = x.reshape(B, S, D)
    return x
```

```python
import functools
import math
from typing import NamedTuple

import jax
import jax.numpy as jnp
import numpy as np
from jax import lax
from jax.experimental import pallas as pl
from jax.experimental.pallas import tpu as pltpu

F32 = jnp.float32
BF16 = jnp.bfloat16

D_MODEL = 1024
HEAD_DIM = 64
N_HEADS_A, N_HEADS_B, N_HEADS_C = 4, 6, 6
WIDTH_A, WIDTH_B, WIDTH_C = N_HEADS_A * HEAD_DIM, N_HEADS_B * HEAD_DIM, N_HEADS_C * HEAD_DIM
DIFF_DIM = HEAD_DIM // 2
ROPE_THETA = 500000.0
ROPE_FRACTION = 4
DILATIONS = (1, 4, 16)
assert DILATIONS[0] == 1 and DILATIONS[2] == DILATIONS[1] ** 2
BAND_HALF = 64
GRID_W = 64
NA_ROWS = 8
NA_COLS = 16
N_EXPERTS = 8
TOP_K = 2
LN_EPS = 1e-5
NEG = -1e30
LOG2E = math.log2(math.e)
LN2 = math.log(2.0)

LANES = 128
SUBLANES = 8
VMEM_LIMIT = 56 * 1024 * 1024

PROJ_W = 3 * (WIDTH_A + WIDTH_B + WIDTH_C)
ROPE_COLS = 3 * WIDTH_A + 2 * WIDTH_B
COL_CHUNK = 3 * WIDTH_A
BAND_LO = 3 * WIDTH_A
BAND_COLS = 3 * WIDTH_B


def _cparams(sem):
    return pltpu.CompilerParams(dimension_semantics=sem, vmem_limit_bytes=VMEM_LIMIT)


def _layer_norm(z, g, b):
    mu = jnp.mean(z, axis=-1, keepdims=True)
    zc = z - mu
    var = jnp.mean(zc * zc, axis=-1, keepdims=True)
    return zc * lax.rsqrt(var + LN_EPS) * g + b


def _rope_tables(seq):
    rot_a = DIFF_DIM // ROPE_FRACTION
    rot_b = HEAD_DIM // ROPE_FRACTION
    pos = jnp.arange(seq, dtype=F32)[:, None]
    inv_a = ROPE_THETA ** (-jnp.arange(0, rot_a, 2, dtype=F32) / rot_a)
    inv_b = ROPE_THETA ** (-jnp.arange(0, rot_b, 2, dtype=F32) / rot_b)
    ang_a, ang_b = pos * inv_a[None, :], pos * inv_b[None, :]

    cols = np.arange(ROPE_COLS)
    in_a = cols < 3 * WIDTH_A
    sub_a = cols // WIDTH_A
    within_a = cols % DIFF_DIM
    rot_mask_a = in_a & (sub_a < 2) & (within_a < rot_a)
    cb = cols - 3 * WIDTH_A
    sub_b = cb // WIDTH_B
    within_b = cb % HEAD_DIM
    rot_mask_b = (~in_a) & (within_b < rot_b)

    first_half = np.where(in_a, within_a < rot_a // 2, within_b < rot_b // 2)
    rot = rot_mask_a | rot_mask_b
    scale = np.ones(ROPE_COLS, np.float32)
    scale[in_a & (sub_a == 0)] = DIFF_DIM ** -0.5 * LOG2E
    scale[(~in_a) & (sub_b == 0)] = HEAD_DIM ** -0.5 * LOG2E

    def spread(tab_a, tab_b, fill):
        def groups(tab, width, reps):
            pad = jnp.full((seq, width - 2 * tab.shape[1]), fill, F32)
            return jnp.tile(jnp.concatenate([tab, tab, pad], axis=1), (1, reps))
        return jnp.concatenate([groups(tab_a, DIFF_DIM, 2 * WIDTH_A // DIFF_DIM),
                                jnp.full((seq, WIDTH_A), fill, F32),
                                groups(tab_b, HEAD_DIM, 2 * WIDTH_B // HEAD_DIM)], axis=1)

    cm = spread(jnp.cos(ang_a), jnp.cos(ang_b), 1.0) * jnp.asarray(scale)[None, :]
    sn = spread(jnp.sin(ang_a), jnp.sin(ang_b), 0.0) * jnp.asarray(scale)[None, :]
    ma = np.where(rot & first_half, -1.0, 0.0).astype(np.float32)[None, :]
    mb = np.where(rot & ~first_half, 1.0, 0.0).astype(np.float32)[None, :]
    return cm, sn, jnp.asarray(ma), jnp.asarray(mb)


PROJ_TM = 512
W_CAST_ROWS = 128


def _cast_rows_to_bf16(src_ref, dst_ref, rows):
    def body(i, c):
        r = pl.multiple_of(i * W_CAST_ROWS, W_CAST_ROWS)
        dst_ref[pl.ds(r, W_CAST_ROWS), :] = src_ref[pl.ds(r, W_CAST_ROWS), :].astype(BF16)
        return c
    lax.fori_loop(0, rows // W_CAST_ROWS, body, 0)


def _proj_kernel(x_ref, w_ref, cm_ref, sn_ref, ma_ref, mb_ref, o_ref, *rest):
    res_refs, (wbf_ref, band_ref, split_ref) = rest[:-3], rest[-3:]

    @pl.when((pl.program_id(0) == 0) & (pl.program_id(1) == 0))
    def _():
        _cast_rows_to_bf16(w_ref, wbf_ref, D_MODEL)

    xb = x_ref[0].astype(BF16)
    tm = xb.shape[0]
    n_chunks = PROJ_W // COL_CHUNK
    for c in range(n_chunks):
        lo, hi = c * COL_CHUNK, (c + 1) * COL_CHUNK
        acc = jnp.dot(xb, wbf_ref[:, lo:hi], preferred_element_type=F32)
        if hi <= ROPE_COLS:
            half = (DIFF_DIM if c == 0 else HEAD_DIM) // ROPE_FRACTION // 2
            up = pltpu.roll(acc, COL_CHUNK - half, 1)
            dn = pltpu.roll(acc, half, 1)
            acc = acc * cm_ref[:, lo:hi] + (up * ma_ref[:, lo:hi] + dn * mb_ref[:, lo:hi]) * sn_ref[:, lo:hi]
        elif c == 2:
            lane = lax.broadcasted_iota(jnp.int32, (1, COL_CHUNK), 1)
            acc = acc * jnp.where(lane >= WIDTH_B, HEAD_DIM ** -0.5 * LOG2E, 1.0)
        o_ref[0, :, lo:hi] = acc.astype(BF16)
        for col in range(max(lo, BAND_LO), min(hi, BAND_LO + BAND_COLS), LANES):
            band_ref[(col - BAND_LO) // LANES] = acc[:, col - lo:col - lo + LANES]

    (d1, d2), (r1_ref, r2_ref) = DILATIONS[1:], res_refs
    for page in range(BAND_COLS // LANES):
        cs = slice(page * LANES, (page + 1) * LANES)
        for r in range(d1):
            rows = band_ref[page, pl.ds(r, tm // d1, stride=d1), :]
            r1_ref[0, r, :, cs] = rows.astype(BF16)
            split_ref[page, r] = rows
        for r in range(d2):
            r2_ref[0, r, :, cs] = split_ref[page, r % d1, pl.ds(r // d1, tm // d2, stride=d1), :].astype(BF16)


def _project(x, w, layer, tables):
    B, S, _ = x.shape
    cm, sn, ma, mb = tables
    tm = PROJ_TM
    res_specs = [pl.BlockSpec((1, d, tm // d, BAND_COLS), lambda p, b: (b, 0, p, 0)) for d in DILATIONS[1:]]
    res_shapes = [jax.ShapeDtypeStruct((B, d, S // d, BAND_COLS), BF16) for d in DILATIONS[1:]]
    return pl.pallas_call(
        _proj_kernel,
        grid=(S // tm, B),
        in_specs=[
            pl.BlockSpec((1, tm, D_MODEL), lambda p, b: (b, p, 0)),
            pl.BlockSpec((None, D_MODEL, PROJ_W), lambda p, b: (layer, 0, 0), pipeline_mode=pl.Buffered(1)),
            pl.BlockSpec((tm, ROPE_COLS), lambda p, b: (p, 0)),
            pl.BlockSpec((tm, ROPE_COLS), lambda p, b: (p, 0)),
            pl.BlockSpec((1, ROPE_COLS), lambda p, b: (0, 0)),
            pl.BlockSpec((1, ROPE_COLS), lambda p, b: (0, 0)),
        ],
        out_specs=[pl.BlockSpec((1, tm, PROJ_W), lambda p, b: (b, p, 0))] + res_specs,
        out_shape=[jax.ShapeDtypeStruct((B, S, PROJ_W), BF16)] + res_shapes,
        scratch_shapes=[pltpu.VMEM((D_MODEL, PROJ_W), BF16),
                        pltpu.VMEM((BAND_COLS // LANES, tm, LANES), F32),
                        pltpu.VMEM((BAND_COLS // LANES, DILATIONS[1], tm // DILATIONS[1], LANES), F32)],
        compiler_params=_cparams(("arbitrary", "arbitrary")),
        name="proj_in",
    )(x, w, cm, sn, ma, mb)


ATT_A_TQ = 256
ATT_A_UNROLL = 4


def _nt_dot(a, b):
    return lax.dot_general(a, b, (((1,), (1,)), ((), ())), preferred_element_type=F32)


def _with_ones(v):
    return jnp.concatenate([v, jnp.ones_like(v)], axis=1)


def _softmax_times_v(e, v1):
    nd = jnp.dot(e, v1, preferred_element_type=F32)
    return nd[:, :LANES] * (1.0 / nd[:, LANES:])


def _attn_a_kernel(lam_ref, g_ref, q_ref, k_ref, v_ref, o_ref, *, lam_init, seq):
    lp = lam_ref[...]
    lam = (jnp.exp(jnp.sum(lp[0:1] * lp[1:2], axis=-1, keepdims=True))
           - jnp.exp(jnp.sum(lp[2:3] * lp[3:4], axis=-1, keepdims=True)) + lam_init)
    lane = lax.broadcasted_iota(jnp.int32, (1, LANES), 1)
    first = lane < HEAD_DIM
    gain = g_ref[...] * (1.0 - lam_init)
    tq = ATT_A_TQ

    def qblock(i, carry):
        r0 = pl.multiple_of(i * tq, tq)
        q = q_ref[0, pl.ds(r0, tq), :]
        k = k_ref[0]
        v1 = _with_ones(v_ref[0])
        outs = []
        for hh in range(2):
            maps = []
            for m in range(2):
                lo = hh * HEAD_DIM + m * DIFF_DIM
                qm = jnp.where((lane >= lo) & (lane < lo + DIFF_DIM), q, jnp.zeros_like(q))
                s = _nt_dot(qm, k)
                e = jnp.exp2(s - jnp.max(s, axis=-1, keepdims=True)).astype(BF16)
                maps.append(_softmax_times_v(e, v1))
            outs.append(maps[0] - lam * maps[1])
        o = jnp.where(first, outs[0], outs[1])
        sq = o * o
        ss0 = jnp.sum(jnp.where(first, sq, 0.0), axis=-1, keepdims=True)
        ss1 = jnp.sum(jnp.where(first, 0.0, sq), axis=-1, keepdims=True)
        ms = jnp.where(first, ss0, ss1) * (1.0 / HEAD_DIM)
        o_ref[0, pl.ds(r0, tq), :] = (o * lax.rsqrt(ms + LN_EPS) * gain).astype(BF16)
        return carry

    lax.fori_loop(0, seq // tq, qblock, 0, unroll=ATT_A_UNROLL)


def _attn_a(qkv, diff_lambda, norm_g, lam_init):
    B, S, _ = qkv.shape
    n_pairs = WIDTH_A // LANES
    g2 = jnp.tile(norm_g.astype(F32), LANES // HEAD_DIM)[None, :]
    blk = lambda off: pl.BlockSpec((1, S, LANES), lambda b, h, off=off: (b, 0, off + h))
    return pl.pallas_call(
        functools.partial(_attn_a_kernel, lam_init=lam_init, seq=S),
        grid=(B, n_pairs),
        in_specs=[
            pl.BlockSpec((4, DIFF_DIM), lambda b, h: (0, 0)),
            pl.BlockSpec((1, LANES), lambda b, h: (0, 0)),
            blk(0), blk(n_pairs), blk(2 * n_pairs),
        ],
        out_specs=pl.BlockSpec((1, S, LANES), lambda b, h: (b, 0, h)),
        out_shape=jax.ShapeDtypeStruct((B, S, WIDTH_A), BF16),
        compiler_params=_cparams(("arbitrary", "arbitrary")),
        name="attn_diff",
    )(diff_lambda.astype(F32), g2, qkv, qkv, qkv)


BAND_TQ = 128
BAND_CHAINS = 24


def _stack_pair(q, lane):
    z = jnp.zeros_like(q)
    return jnp.concatenate([jnp.where(lane < HEAD_DIM, q, z), jnp.where(lane < HEAD_DIM, z, q)], axis=0)


def _attn_b_kernel(q_ref, k_ref, v_ref, o_ref, lse_ref, *, length, dil):
    tq = BAND_TQ
    win = min(tq + 2 * BAND_HALF, length)
    lane = lax.broadcasted_iota(jnp.int32, (1, LANES), 1)

    blocks = [(i * tq, min(max(i * tq - BAND_HALF, 0), length - win)) for i in range(length // tq)]
    row = lax.broadcasted_iota(jnp.int32, (2 * tq, win), 0)
    delta = lax.broadcasted_iota(jnp.int32, (2 * tq, win), 1) - jnp.where(row >= tq, row - tq, row)
    masks = {off: jnp.where(jnp.abs(delta - off) <= BAND_HALF, 0.0, NEG) for off in {q0 - k0 for q0, k0 in blocks}}

    def residue(r, carry):
        for q0, k0 in blocks:
            lse_tile = jnp.zeros((tq, LANES), F32)
            for pr in range(WIDTH_B // LANES):
                cs = slice(pr * LANES, (pr + 1) * LANES)
                q2 = _stack_pair(q_ref[0, r, q0:q0 + tq, cs], lane)
                s = _nt_dot(q2, k_ref[0, r, k0:k0 + win, cs]) + masks[q0 - k0]
                mx = jnp.max(s, axis=-1, keepdims=True)
                e = jnp.exp2(s - mx).astype(BF16)
                nd = jnp.dot(e, _with_ones(v_ref[0, r, k0:k0 + win, cs]), preferred_element_type=F32)
                l = nd[:, LANES:]
                o2 = nd[:, :LANES] * (1.0 / l)
                o_ref[0, r, q0:q0 + tq, cs] = jnp.where(lane < HEAD_DIM, o2[:tq], o2[tq:]).astype(BF16)
                lse2 = (mx + jnp.log2(l)) * LN2
                lse_tile = jnp.where(lane == 2 * pr, lse2[:tq], lse_tile)
                lse_tile = jnp.where(lane == 2 * pr + 1, lse2[tq:], lse_tile)
            lse_ref[0, r, q0:q0 + tq, :] = lse_tile
        return carry

    chains = len(blocks) * (WIDTH_B // LANES)
    lax.fori_loop(0, dil, residue, 0, unroll=min(dil, max(1, BAND_CHAINS // chains)))


def _attn_b(src, dil, first_blk):
    B, _, L, _ = src.shape
    blk = lambda off: pl.BlockSpec((1, dil, L, WIDTH_B), lambda b, off=off: (b, 0, 0, first_blk + off))
    return pl.pallas_call(
        functools.partial(_attn_b_kernel, length=L, dil=dil),
        grid=(B,),
        in_specs=[blk(0), blk(1), blk(2)],
        out_specs=[pl.BlockSpec((1, dil, L, WIDTH_B), lambda b: (b, 0, 0, 0)),
                   pl.BlockSpec((1, dil, L, LANES), lambda b: (b, 0, 0, 0))],
        out_shape=[jax.ShapeDtypeStruct((B, dil, L, WIDTH_B), BF16),
                   jax.ShapeDtypeStruct((B, dil, L, LANES), F32)],
        compiler_params=_cparams(("arbitrary",)),
        name=f"attn_band_d{dil}",
    )(src, src, src)


N_RPB_ROWS = 2 * NA_ROWS - 1
N_RPB_COLS = 2 * NA_COLS - 1
NA_KEYS = NA_ROWS * GRID_W
N_BIAS_VARIANTS = NA_ROWS
NA_ROWS_PER_STEP = 8


def _attn_c_kernel(u_ref, q_ref, k_ref, v_ref, o_ref, bias_ref, *, rows):
    lane = lax.broadcasted_iota(jnp.int32, (1, LANES), 1)
    n_pairs = WIDTH_C // LANES

    @pl.when(pl.program_id(0) == 0)
    def _():
        wide = N_RPB_ROWS * GRID_W
        c = lax.broadcasted_iota(jnp.int32, (GRID_W, wide), 0)
        kc = lax.broadcasted_iota(jnp.int32, (GRID_W, wide), 1) % GRID_W
        tidx = jnp.clip(kc - c + NA_COLS - 1, 0, N_RPB_COLS - 1)
        c5 = lax.broadcasted_iota(jnp.int32, (GRID_W, NA_KEYS), 0)
        kc5 = lax.broadcasted_iota(jnp.int32, (GRID_W, NA_KEYS), 1) % GRID_W
        cstart = jnp.clip(c5 - NA_COLS // 2, 0, GRID_W - NA_COLS)
        valid = (kc5 >= cstart) & (kc5 < cstart + NA_COLS)
        for h in range(N_HEADS_C):
            def tbody(t, acc, h=h):
                row = u_ref[h, pl.ds(t, 1), :]
                return jnp.where(tidx == t, row, acc)
            toep = lax.fori_loop(0, N_RPB_COLS, tbody, jnp.zeros((GRID_W, wide), F32))
            for var in range(N_BIAS_VARIANTS):
                tile = jnp.where(valid, toep[:, var * GRID_W:var * GRID_W + NA_KEYS] * LOG2E, NEG)
                bias_ref[var, h // 2, (h % 2) * GRID_W:(h % 2 + 1) * GRID_W, :] = tile

    def rbody(step, carry):
        for sub in range(NA_ROWS_PER_STEP):
            r = step * NA_ROWS_PER_STEP + sub
            rs = jnp.clip(r - NA_ROWS // 2, 0, rows - NA_ROWS)
            var = rs - r + NA_ROWS - 1
            q0 = pl.multiple_of(r * GRID_W, GRID_W)
            k0 = pl.multiple_of(rs * GRID_W, GRID_W)
            for pr in range(n_pairs):
                cs = slice(pr * LANES, (pr + 1) * LANES)
                q2 = _stack_pair(q_ref[0, pl.ds(q0, GRID_W), cs], lane)
                s = _nt_dot(q2, k_ref[0, pl.ds(k0, NA_KEYS), cs]) + bias_ref[var, pr]
                e = jnp.exp2(s - jnp.max(s, axis=-1, keepdims=True)).astype(BF16)
                o2 = _softmax_times_v(e, _with_ones(v_ref[0, pl.ds(k0, NA_KEYS), cs]))
                o_ref[0, pl.ds(q0, GRID_W), cs] = jnp.where(lane < HEAD_DIM, o2[:GRID_W], o2[GRID_W:]).astype(BF16)
        return carry

    lax.fori_loop(0, rows // NA_ROWS_PER_STEP, rbody, 0)


def _attn_c(qkv, rpb):
    B, S, _ = qkv.shape
    rows = S // GRID_W
    assert rows >= NA_ROWS
    u = jnp.repeat(jnp.transpose(rpb.astype(F32), (0, 2, 1)), GRID_W, axis=2)
    q_blk = (3 * WIDTH_A + 3 * WIDTH_B) // WIDTH_C
    blk = lambda off: pl.BlockSpec((1, S, WIDTH_C), lambda b, off=off: (b, 0, off))
    return pl.pallas_call(
        functools.partial(_attn_c_kernel, rows=rows),
        grid=(B,),
        in_specs=[pl.BlockSpec((N_HEADS_C, N_RPB_COLS, N_RPB_ROWS * GRID_W), lambda b: (0, 0, 0)),
                  blk(q_blk), blk(q_blk + 1), blk(q_blk + 2)],
        out_specs=pl.BlockSpec((1, S, WIDTH_C), lambda b: (b, 0, 0)),
        out_shape=jax.ShapeDtypeStruct((B, S, WIDTH_C), BF16),
        scratch_shapes=[pltpu.VMEM((N_BIAS_VARIANTS, WIDTH_C // LANES, LANES, NA_KEYS), F32)],
        compiler_params=_cparams(("arbitrary",)),
        name="attn_nbr",
    )(u, qkv, qkv, qkv)


OUT_TM = 512
OUT_SPLIT = 2


def _split_bf16(x):
    hi = x.astype(BF16)
    return hi, (x - hi.astype(F32)).astype(BF16)


def _outproj_kernel(x_ref, oa_ref, ob0_ref, ob1_ref, ob2_ref, l0_ref, l1_ref, l2_ref, oc_ref,
                    beta_ref, w_ref, g_ref, b_ref, e_ref, *rest, alpha, route):
    if route:
        wr_ref, o_ref, meta_ref, fields_ref, cnt_ref, wbf_ref, il_ref, carry_ref = rest
    else:
        o_ref, wbf_ref, il_ref = rest

    @pl.when(pl.program_id(0) == 0)
    def _():
        _cast_rows_to_bf16(w_ref, wbf_ref, D_MODEL)
        if route:
            carry_ref[...] = jnp.zeros_like(carry_ref)

    tm = x_ref.shape[0]
    n_pages = WIDTH_B // LANES
    for slot, (dil, ob_r, l_r) in enumerate(zip(DILATIONS[1:], (ob1_ref, ob2_ref), (l1_ref, l2_ref))):
        for r in range(dil):
            rows = pl.ds(r, tm // dil, stride=dil)
            for page in range(n_pages):
                il_ref[slot, page, rows, :] = ob_r[0, r, :, page * LANES:(page + 1) * LANES].astype(F32)
            il_ref[slot, n_pages, rows, :] = l_r[0, r]

    beta = beta_ref[...]
    half = tm // OUT_SPLIT
    for part in range(OUT_SPLIT):
        rs = slice(part * half, (part + 1) * half)
        l0, l1, l2 = l0_ref[0, 0, rs, :], il_ref[0, n_pages, rs, :], il_ref[1, n_pages, rs, :]
        lm = jnp.maximum(jnp.maximum(l0, l1), l2)
        e0, e1, e2 = jnp.exp(l0 - lm), jnp.exp(l1 - lm), jnp.exp(l2 - lm)
        inv = 1.0 / (e0 + e1 + e2)
        wexp = [jnp.dot(jnp.concatenate(_split_bf16(e * inv), axis=1), e_ref[...], preferred_element_type=F32)
                for e in (e0, e1, e2)]
        pages = []
        for page in range(n_pages):
            cs = slice(page * LANES, (page + 1) * LANES)
            branch_vals = (ob0_ref[0, 0, rs, cs].astype(F32), il_ref[0, page, rs, :], il_ref[1, page, rs, :])
            pages.append(sum(w[:, cs] * val for w, val in zip(wexp, branch_vals)))
        ob = jnp.concatenate(pages, axis=1)

        ma = (oa_ref[rs, :].astype(F32) * beta[:, :WIDTH_A]).astype(BF16)
        mb = (ob * beta[:, WIDTH_A:WIDTH_A + WIDTH_B]).astype(BF16)
        mc = (oc_ref[rs, :].astype(F32) * beta[:, WIDTH_A + WIDTH_B:]).astype(BF16)
        acc = jnp.dot(jnp.concatenate([ma, mb, mc], axis=1), wbf_ref[...], preferred_element_type=F32)
        y = _layer_norm(alpha * x_ref[rs, :] + acc, g_ref[...], b_ref[...])
        if route:
            _store_token_tiles(o_ref, y, row0=part * half)
            fields = _route_rows(y, wr_ref, carry_ref)
            fields_ref[:, rs] = fields
            meta_ref[rs, :] = jnp.concatenate([fields, jnp.zeros((LANES - SUBLANES, half), F32)], axis=0).T
        else:
            o_ref[rs, :] = y
    if route:
        cnt_ref[...] = carry_ref[...]


def _store_token_tiles(ref, y, row0=0):
    rows = y.shape[0]
    for j in range(D_MODEL // LANES):
        ref[pl.ds(row0 * SUBLANES + j, rows, stride=SUBLANES), :] = y[:, j * LANES:(j + 1) * LANES]


def _load_token_tiles(ref, rows, row0=0):
    return jnp.concatenate(
        [ref[pl.ds(row0 * SUBLANES + j, rows, stride=SUBLANES), :] for j in range(D_MODEL // LANES)], axis=1)


def _outproj(x2, oa, obs, lses, oc, beta, w, layer, g, b, alpha, w_router=None):
    route = w_router is not None
    N = x2.shape[0]
    tm = OUT_TM
    steps_per_seq = obs[0].shape[2] // tm
    expand = np.zeros((2 * LANES, WIDTH_B), np.float32)
    for h in range(N_HEADS_B):
        expand[h, h * HEAD_DIM:(h + 1) * HEAD_DIM] = 1.0
        expand[LANES + h, h * HEAD_DIM:(h + 1) * HEAD_DIM] = 1.0
    row = lambda w_: pl.BlockSpec((tm, w_), lambda i: (i, 0))
    full = lambda r, c, **kw: pl.BlockSpec((r, c), lambda i: (0, 0), **kw)
    res = lambda d, w_: pl.BlockSpec((1, d, tm // d, w_),
                                     lambda i: (i // steps_per_seq, 0, i % steps_per_seq, 0))
    in_specs = ([row(D_MODEL), row(WIDTH_A)] + [res(d, WIDTH_B) for d in DILATIONS]
                + [res(d, LANES) for d in DILATIONS] + [row(WIDTH_C), full(1, D_MODEL),
                   pl.BlockSpec((None, D_MODEL, D_MODEL), lambda i: (layer, 0, 0), pipeline_mode=pl.Buffered(1)),
                   full(1, D_MODEL), full(1, D_MODEL), full(2 * LANES, WIDTH_B)])
    args = [x2, oa, obs[0], obs[1], obs[2], lses[0], lses[1], lses[2], oc,
            beta[None, :].astype(F32), w, g[None, :].astype(F32), b[None, :].astype(F32), jnp.asarray(expand, BF16)]
    scratch = [pltpu.VMEM((D_MODEL, D_MODEL), BF16),
               pltpu.VMEM((len(DILATIONS) - 1, WIDTH_B // LANES + 1, tm, LANES), F32)]
    if route:
        wr = jnp.zeros((D_MODEL, LANES), F32).at[:, :N_EXPERTS].set(w_router.astype(F32))
        wr_hi = wr.astype(BF16)
        in_specs.append(full(D_MODEL, 2 * LANES))
        args.append(jnp.concatenate([wr_hi, (wr - wr_hi.astype(F32)).astype(BF16)], axis=1))
        out_specs = [pl.BlockSpec((tm * SUBLANES, LANES), lambda i: (i, 0)), row(LANES),
                     pl.BlockSpec((SUBLANES, tm), lambda i: (0, i)), full(ROUTE_ROWS, LANES)]
        out_shape = [jax.ShapeDtypeStruct((N * SUBLANES, LANES), F32), jax.ShapeDtypeStruct((N, LANES), F32),
                     jax.ShapeDtypeStruct((SUBLANES, N), F32), jax.ShapeDtypeStruct((ROUTE_ROWS, LANES), F32)]
        scratch.append(pltpu.VMEM((ROUTE_ROWS, LANES), F32))
    else:
        out_specs = row(D_MODEL)
        out_shape = jax.ShapeDtypeStruct((N, D_MODEL), F32)
    return pl.pallas_call(
        functools.partial(_outproj_kernel, alpha=alpha, route=route),
        grid=(N // tm,),
        in_specs=in_specs,
        out_specs=out_specs,
        out_shape=out_shape,
        scratch_shapes=scratch,
        compiler_params=_cparams(("arbitrary",)),
        name="merge_outproj_ln_route" if route else "merge_outproj_ln",
    )(*args)


CAST_ROWS = 256
DENSE_TM = 512
DENSE_TF = 256


def _cast_kernel(w_ref, o_ref):
    o_ref[...] = w_ref[...].astype(BF16)


def _to_bf16(w):
    rows, cols = w.shape
    return pl.pallas_call(
        _cast_kernel,
        grid=(rows // CAST_ROWS,),
        in_specs=[pl.BlockSpec((CAST_ROWS, cols), lambda i: (i, 0))],
        out_specs=pl.BlockSpec((CAST_ROWS, cols), lambda i: (i, 0)),
        out_shape=jax.ShapeDtypeStruct((rows, cols), BF16),
        compiler_params=_cparams(("arbitrary",)),
        name="cast_bf16",
    )(w)


def _dense_ffn_kernel(x_ref, wg_ref, wu_ref, wd_ref, g_ref, b_ref, o_ref, h_ref, *, alpha):
    x = x_ref[...]
    xb = x.astype(BF16)
    for c in range(h_ref.shape[1] // DENSE_TF):
        cs = slice(c * DENSE_TF, (c + 1) * DENSE_TF)
        gate = jnp.dot(xb, wg_ref[:, cs], preferred_element_type=F32)
        up = jnp.dot(xb, wu_ref[:, cs], preferred_element_type=F32)
        h_ref[:, cs] = (gate * jax.nn.sigmoid(gate) * up).astype(BF16)
    y = jnp.dot(h_ref[...], wd_ref[...], preferred_element_type=F32)
    o_ref[...] = _layer_norm(alpha * x + y, g_ref[...], b_ref[...])


def _dense_ffn(x, w_gate, w_up, w_down, g, b, alpha):
    n_tok = x.shape[0]
    ff = w_gate.shape[1]
    tm = DENSE_TM
    row = pl.BlockSpec((tm, D_MODEL), lambda i: (i, 0))
    resident = lambda r, c: pl.BlockSpec((r, c), lambda i: (0, 0), pipeline_mode=pl.Buffered(1))
    return pl.pallas_call(
        functools.partial(_dense_ffn_kernel, alpha=alpha),
        grid=(n_tok // tm,),
        in_specs=[row, resident(D_MODEL, ff), resident(D_MODEL, ff), resident(ff, D_MODEL),
                  pl.BlockSpec((1, D_MODEL), lambda i: (0, 0)), pl.BlockSpec((1, D_MODEL), lambda i: (0, 0))],
        out_specs=row,
        out_shape=jax.ShapeDtypeStruct((n_tok, D_MODEL), F32),
        scratch_shapes=[pltpu.VMEM((tm, ff), BF16)],
        compiler_params=_cparams(("arbitrary",)),
        name="swiglu_dense",
    )(x, _to_bf16(w_gate), _to_bf16(w_up), _to_bf16(w_down), g[None, :].astype(F32), b[None, :].astype(F32))


FFN_TM = 1024
FFN_TF = 512
FFN_SUB = 256


def _ffn_kernel(te_ref, nt_ref, rows_ref, x_ref, wg_ref, wu_ref, wd_ref, o_ref, xb_ref, acc_ref, *, n_chunks):
    i, j = pl.program_id(0), pl.program_id(1)
    tm = xb_ref.shape[0]

    def swiglu_rows(rs):
        xb = xb_ref[rs, :]
        gate = jnp.dot(xb, wg_ref[...].astype(BF16), preferred_element_type=F32)
        up = jnp.dot(xb, wu_ref[...].astype(BF16), preferred_element_type=F32)
        h = (gate * jax.nn.sigmoid(gate) * up).astype(BF16)
        acc_ref[rs, :] += jnp.dot(h, wd_ref[...].astype(BF16), preferred_element_type=F32)

    @pl.when(i < nt_ref[0])
    def _():
        @pl.when(j == 0)
        def _():
            xb_ref[...] = _load_token_tiles(x_ref, tm).astype(BF16)
            acc_ref[...] = jnp.zeros_like(acc_ref)

        sparse_tile = rows_ref[i] <= tm - FFN_SUB

        @pl.when(jnp.logical_not(sparse_tile))
        def _():
            swiglu_rows(slice(None))

        for sb in range(tm // FFN_SUB - 1):
            @pl.when(sparse_tile & (sb * FFN_SUB < rows_ref[i]))
            def _(sb=sb):
                swiglu_rows(slice(sb * FFN_SUB, (sb + 1) * FFN_SUB))

        @pl.when(j == n_chunks - 1)
        def _():
            _store_token_tiles(o_ref, acc_ref[...])

    @pl.when((i >= nt_ref[0]) & (j == n_chunks - 1))
    def _():
        o_ref[...] = jnp.zeros_like(o_ref)


def _grouped_ffn(x, w_gate, w_up, w_down, tile_expert, n_tiles, tile_rows):
    tm, tf = FFN_TM, FFN_TF
    n_chunks = w_gate.shape[-1] // tf
    max_tiles = tile_expert.shape[0]

    def tile_ix(i, nt):
        return jnp.minimum(i, nt[0] - 1)

    def chunk_ix(i, j, nt):
        return jnp.where(i < nt[0], j, n_chunks - 1)

    blk = (tm * SUBLANES, LANES)
    in_specs = [
        pl.BlockSpec(blk, lambda i, j, te, nt, tr: (tile_ix(i, nt), 0)),
        pl.BlockSpec((None, D_MODEL, tf), lambda i, j, te, nt, tr: (te[tile_ix(i, nt)], 0, chunk_ix(i, j, nt))),
        pl.BlockSpec((None, D_MODEL, tf), lambda i, j, te, nt, tr: (te[tile_ix(i, nt)], 0, chunk_ix(i, j, nt))),
        pl.BlockSpec((None, tf, D_MODEL), lambda i, j, te, nt, tr: (te[tile_ix(i, nt)], chunk_ix(i, j, nt), 0)),
    ]
    return pl.pallas_call(
        functools.partial(_ffn_kernel, n_chunks=n_chunks),
        grid_spec=pltpu.PrefetchScalarGridSpec(
            num_scalar_prefetch=3,
            grid=(max_tiles, n_chunks),
            in_specs=in_specs,
            out_specs=pl.BlockSpec(blk, lambda i, j, te, nt, tr: (i, 0)),
            scratch_shapes=[pltpu.VMEM((tm, D_MODEL), BF16), pltpu.VMEM((tm, D_MODEL), F32)],
        ),
        out_shape=jax.ShapeDtypeStruct(x.shape, F32),
        compiler_params=_cparams(("arbitrary", "arbitrary")),
        name="swiglu_grouped",
    )(tile_expert, n_tiles, tile_rows, x, w_gate, w_up, w_down)


META_E1, META_E2, META_R1, META_R2, META_G1, META_G2 = range(6)


ROUTE_ROWS = 16


def _route_rows(x, wr_ref, carry_ref):
    tb = x.shape[0]
    xh, xl = _split_bf16(x)
    hi_lo = jnp.dot(xh, wr_ref[...], preferred_element_type=F32)
    logits = hi_lo[:, :LANES] + hi_lo[:, LANES:] + jnp.dot(xl, wr_ref[:, :LANES], preferred_element_type=F32)
    logits = logits.T[:ROUTE_ROWS]
    row = lax.broadcasted_iota(jnp.int32, (ROUTE_ROWS, tb), 0).astype(F32)
    logits = jnp.where(row < N_EXPERTS, logits, -jnp.inf)
    m1 = jnp.max(logits, axis=0, keepdims=True)
    i1 = jnp.min(jnp.where(logits == m1, row, float(ROUTE_ROWS)), axis=0, keepdims=True)
    rest = jnp.where(row == i1, -jnp.inf, logits)
    m2 = jnp.max(rest, axis=0, keepdims=True)
    i2 = jnp.min(jnp.where(rest == m2, row, float(ROUTE_ROWS)), axis=0, keepdims=True)
    e21 = jnp.exp(m2 - m1)
    g1 = 1.0 / (1.0 + e21)
    g2 = e21 * g1

    member = ((row == i1) | (row == i2)).astype(F32)
    r_i = lax.broadcasted_iota(jnp.int32, (tb, tb), 0)
    c_i = lax.broadcasted_iota(jnp.int32, (tb, tb), 1)
    earlier = (r_i < c_i).astype(BF16)
    before = jnp.dot(member.astype(BF16), earlier, preferred_element_type=F32)
    rank = carry_ref[:, 0:1] + before
    r1 = jnp.sum(jnp.where(row == i1, rank, 0.0), axis=0, keepdims=True)
    r2 = jnp.sum(jnp.where(row == i2, rank, 0.0), axis=0, keepdims=True)
    carry_ref[...] += jnp.sum(member, axis=1, keepdims=True)

    fields = [None] * SUBLANES
    for slot, val in ((META_E1, i1), (META_E2, i2), (META_R1, r1), (META_R2, r2), (META_G1, g1), (META_G2, g2)):
        fields[slot] = val
    return jnp.concatenate([f if f is not None else jnp.zeros((1, tb), F32) for f in fields], axis=0)


MOVE_TB = 1024
PAD_CHUNK = 64
DMA_ISSUE_UNROLL = 8


def _tile_rows(ref, idx):
    return ref.at[pl.ds(pl.multiple_of(idx * SUBLANES, SUBLANES), SUBLANES), :]


def _zero_fill(pad_ref, xs_hbm, zbuf, zsem):
    zbuf[...] = jnp.zeros_like(zbuf)
    one = _tile_rows(zbuf, 0)
    for wait in (False, True):
        for seg in range(pad_ref.shape[1]):
            start, n = pad_ref[0, seg], pad_ref[1, seg]
            n_single = n % PAD_CHUNK

            def single(t, c):
                cp = pltpu.make_async_copy(one, _tile_rows(xs_hbm, start + t), zsem)
                cp.wait() if wait else cp.start()
                return c

            def bulk(t, c):
                r0 = pl.multiple_of((start + n_single + t * PAD_CHUNK) * SUBLANES, SUBLANES)
                cp = pltpu.make_async_copy(zbuf, xs_hbm.at[pl.ds(r0, PAD_CHUNK * SUBLANES), :], zsem)
                cp.wait() if wait else cp.start()
                return c

            lax.fori_loop(0, n_single, single, 0)
            lax.fori_loop(0, n // PAD_CHUNK, bulk, 0)


def _dispatch_kernel(dest_ref, pad_ref, x_ref, xs_hbm, zbuf, sem, zsem):
    tb = MOVE_TB

    @pl.when(pl.program_id(0) == 0)
    def _():
        _zero_fill(pad_ref, xs_hbm, zbuf, zsem)

    def body(t, c):
        src = _tile_rows(x_ref, t)
        for k in range(TOP_K):
            pltpu.make_async_copy(src, _tile_rows(xs_hbm, dest_ref[0, 0, TOP_K * t + k]), sem).start(priority=k)
        return c

    lax.fori_loop(0, tb, body, 0, unroll=DMA_ISSUE_UNROLL)
    n_rows = TOP_K * tb * SUBLANES
    pltpu.make_async_copy(xs_hbm.at[pl.ds(0, n_rows), :], xs_hbm.at[pl.ds(0, n_rows), :], sem).wait()


def _dispatch(xt, dest, pads, n_slots):
    n_tok = xt.shape[0] // SUBLANES
    tb = MOVE_TB
    dest3 = dest.reshape(n_tok // tb, 1, TOP_K * tb)
    return pl.pallas_call(
        _dispatch_kernel,
        grid=(n_tok // tb,),
        in_specs=[pl.BlockSpec((1, 1, TOP_K * tb), lambda i: (i, 0, 0), memory_space=pltpu.SMEM),
                  pl.BlockSpec(memory_space=pltpu.SMEM),
                  pl.BlockSpec((tb * SUBLANES, LANES), lambda i: (i, 0))],
        out_specs=pl.BlockSpec(memory_space=pl.ANY),
        out_shape=jax.ShapeDtypeStruct((n_slots * SUBLANES, LANES), F32),
        scratch_shapes=[pltpu.VMEM((PAD_CHUNK * SUBLANES, LANES), F32),
                        pltpu.SemaphoreType.DMA(()), pltpu.SemaphoreType.DMA(())],
        compiler_params=_cparams(("arbitrary",)),
        name="moe_dispatch",
    )(dest3, pads, xt)


COMB_TB = 256


def _combine_kernel(dest_ref, next_ref, y_hbm, x_ref, meta_ref, g_ref, b_ref, o_ref, ybuf, sem, *, alpha, n_steps):
    tb = COMB_TB
    i = pl.program_id(0)
    slot = i % 2

    def gather(d_ref, step, s):
        base = (step % (MOVE_TB // tb)) * (TOP_K * tb)

        def body(t, c):
            for k in range(TOP_K):
                pltpu.make_async_copy(_tile_rows(y_hbm, d_ref[0, 0, base + TOP_K * t + k]),
                                      _tile_rows(ybuf.at[s], k * tb + t), sem.at[s]).start(priority=k)
            return c
        lax.fori_loop(0, tb, body, 0, unroll=DMA_ISSUE_UNROLL)

    @pl.when(i == 0)
    def _():
        gather(dest_ref, i, 0)

    @pl.when(i + 1 < n_steps)
    def _():
        gather(next_ref, i + 1, 1 - slot)

    cur = ybuf.at[slot]
    pltpu.make_async_copy(cur, cur, sem.at[slot]).wait()

    meta = meta_ref[...]
    lane = lax.broadcasted_iota(jnp.int32, meta.shape, 1)
    g1 = jnp.sum(jnp.where(lane == META_G1, meta, 0.0), axis=-1, keepdims=True)
    g2 = jnp.sum(jnp.where(lane == META_G2, meta, 0.0), axis=-1, keepdims=True)
    x = _load_token_tiles(x_ref, tb)
    f = g1 * _load_token_tiles(cur, tb) + g2 * _load_token_tiles(cur, tb, row0=tb)
    o_ref[...] = _layer_norm(alpha * x + f, g_ref[...], b_ref[...])


def _combine(yt, xt, dest, meta, g, b, alpha):
    n_tok = xt.shape[0] // SUBLANES
    tb = COMB_TB
    n_steps = n_tok // tb
    per_blk = MOVE_TB // tb
    dest3 = dest.reshape(n_tok // MOVE_TB, 1, TOP_K * MOVE_TB)
    return pl.pallas_call(
        functools.partial(_combine_kernel, alpha=alpha, n_steps=n_steps),
        grid=(n_steps,),
        in_specs=[pl.BlockSpec((1, 1, TOP_K * MOVE_TB), lambda i: (i // per_blk, 0, 0), memory_space=pltpu.SMEM),
                  pl.BlockSpec((1, 1, TOP_K * MOVE_TB),
                               lambda i: (jnp.minimum(i + 1, n_steps - 1) // per_blk, 0, 0),
                               memory_space=pltpu.SMEM),
                  pl.BlockSpec(memory_space=pl.ANY),
                  pl.BlockSpec((tb * SUBLANES, LANES), lambda i: (i, 0)),
                  pl.BlockSpec((tb, LANES), lambda i: (i, 0)),
                  pl.BlockSpec((1, D_MODEL), lambda i: (0, 0)),
                  pl.BlockSpec((1, D_MODEL), lambda i: (0, 0))],
        out_specs=pl.BlockSpec((tb, D_MODEL), lambda i: (i, 0)),
        out_shape=jax.ShapeDtypeStruct((n_tok, D_MODEL), F32),
        scratch_shapes=[pltpu.VMEM((2, TOP_K * tb * SUBLANES, LANES), F32), pltpu.SemaphoreType.DMA((2,))],
        compiler_params=_cparams(("arbitrary",)),
        name="moe_combine_ln",
    )(dest3, dest3, yt, xt, meta, g[None, :].astype(F32), b[None, :].astype(F32))


class MoePlan(NamedTuple):
    dest: jax.Array
    tile_expert: jax.Array
    tile_rows: jax.Array
    n_tiles: jax.Array
    pads: jax.Array
    n_slots: int


def _moe_plan(fields, counts, n_tok):
    tm = FFN_TM
    max_tiles = n_tok * TOP_K // tm + N_EXPERTS
    cnt = counts[:N_EXPERTS, 0].astype(jnp.int32)
    tiles = (cnt + tm - 1) // tm
    tile_end = jnp.cumsum(tiles)
    start = (tile_end - tiles) * tm
    e = fields[META_E1:META_E2 + 1].astype(jnp.int32)
    r = fields[META_R1:META_R2 + 1].astype(jnp.int32)
    experts = jnp.arange(N_EXPERTS, dtype=jnp.int32)[:, None, None]
    group_start = jnp.sum(jnp.where(e[None] == experts, start[:, None, None], 0), axis=0)
    dest = (group_start + r).T.reshape(-1)
    tile_ids = jnp.arange(max_tiles, dtype=jnp.int32)
    tile_expert = jnp.minimum(jnp.sum((tile_ids[:, None] >= tile_end[None, :]).astype(jnp.int32), axis=1),
                              N_EXPERTS - 1)
    first_tile = (tile_end - tiles)[tile_expert]
    tile_rows = jnp.clip(cnt[tile_expert] - (tile_ids - first_tile) * tm, 0, tm).astype(jnp.int32)
    n_tiles = tile_end[-1:]
    pads = jnp.stack([jnp.concatenate([start + cnt, n_tiles * tm]),
                      jnp.concatenate([tiles * tm - cnt, (max_tiles - n_tiles) * tm])]).astype(jnp.int32)
    return MoePlan(dest, tile_expert, tile_rows, n_tiles.astype(jnp.int32), pads, max_tiles * tm)


def kernel(x, w_in, w_out, beta_mix, diff_lambda, diff_norm_g, na_rpb, ln1_g, ln1_b, ln2_g, ln2_b,
           ffn_w_gate, ffn_w_up, ffn_w_down, moe_w_router, moe_w_gate, moe_w_up, moe_w_down):
    B, S, D = x.shape
    assert D == D_MODEL and S % (max(DILATIONS) * BAND_TQ) == 0 and S % PROJ_TM == 0
    depth = w_in.shape[0]
    alpha = (2 * depth) ** 0.25
    n_tok = B * S
    tables = _rope_tables(S)
    x = x.astype(F32)
    for l in range(depth):
        lam_init = 0.8 - 0.6 * math.exp(-0.3 * l)
        moe = l % 2 == 1
        qkv, *res_major = _project(x, w_in, l, tables)
        oa = _attn_a(qkv, diff_lambda[l], diff_norm_g[l], lam_init).reshape(n_tok, WIDTH_A)
        band = [_attn_b(qkv[:, None], DILATIONS[0], BAND_LO // WIDTH_B)]
        band += [_attn_b(src, d, 0) for d, src in zip(DILATIONS[1:], res_major)]
        oc = _attn_c(qkv, na_rpb[l]).reshape(n_tok, WIDTH_C)
        j = l // 2
        mixed = _outproj(x.reshape(n_tok, D), oa, [o for o, _ in band], [s for _, s in band], oc,
                         beta_mix[l], w_out, l, ln1_g[l], ln1_b[l], alpha,
                         w_router=moe_w_router[j] if moe else None)
        if not moe:
            x = _dense_ffn(mixed, ffn_w_gate[j], ffn_w_up[j], ffn_w_down[j], ln2_g[l], ln2_b[l], alpha)
        else:
            x1, meta, fields, counts = mixed
            plan = _moe_plan(fields, counts, n_tok)
            xs = _dispatch(x1, plan.dest, plan.pads, plan.n_slots)
            ys = _grouped_ffn(xs, moe_w_gate[j], moe_w_up[j], moe_w_down[j],
                              plan.tile_expert, plan.n_tiles, plan.tile_rows)
            x = _combine(ys, x1, plan.dest, meta, ln2_g[l], ln2_b[l], alpha)
        x = x.reshape(B, S, D)
    return x
```

```python
import functools
import math
from typing import NamedTuple

import jax
import jax.numpy as jnp
import numpy as np
from jax import lax
from jax.experimental import pallas as pl
from jax.experimental.pallas import tpu as pltpu

F32 = jnp.float32
BF16 = jnp.bfloat16

D_MODEL = 1024
HEAD_DIM = 64
N_HEADS_A, N_HEADS_B, N_HEADS_C = 4, 6, 6
WIDTH_A, WIDTH_B, WIDTH_C = N_HEADS_A * HEAD_DIM, N_HEADS_B * HEAD_DIM, N_HEADS_C * HEAD_DIM
DIFF_DIM = HEAD_DIM // 2
ROPE_THETA = 500000.0
ROPE_FRACTION = 4
DILATIONS = (1, 4, 16)
assert DILATIONS[0] == 1 and DILATIONS[2] == DILATIONS[1] ** 2
BAND_HALF = 64
GRID_W = 64
NA_ROWS = 8
NA_COLS = 16
N_EXPERTS = 8
TOP_K = 2
LN_EPS = 1e-5
NEG = -1e30
LOG2E = math.log2(math.e)
LN2 = math.log(2.0)

LANES = 128
SUBLANES = 8
VMEM_LIMIT = 56 * 1024 * 1024

PROJ_W = 3 * (WIDTH_A + WIDTH_B + WIDTH_C)
ROPE_COLS = 3 * WIDTH_A + 2 * WIDTH_B
COL_CHUNK = 3 * WIDTH_A
BAND_LO = 3 * WIDTH_A
BAND_COLS = 3 * WIDTH_B


def _cparams(sem):
    return pltpu.CompilerParams(dimension_semantics=sem, vmem_limit_bytes=VMEM_LIMIT)


def _layer_norm(z, g, b):
    mu = jnp.mean(z, axis=-1, keepdims=True)
    zc = z - mu
    var = jnp.mean(zc * zc, axis=-1, keepdims=True)
    return zc * lax.rsqrt(var + LN_EPS) * g + b


def _rope_tables(seq):
    rot_a = DIFF_DIM // ROPE_FRACTION
    rot_b = HEAD_DIM // ROPE_FRACTION
    pos = jnp.arange(seq, dtype=F32)[:, None]
    inv_a = ROPE_THETA ** (-jnp.arange(0, rot_a, 2, dtype=F32) / rot_a)
    inv_b = ROPE_THETA ** (-jnp.arange(0, rot_b, 2, dtype=F32) / rot_b)
    ang_a, ang_b = pos * inv_a[None, :], pos * inv_b[None, :]

    cols = np.arange(ROPE_COLS)
    in_a = cols < 3 * WIDTH_A
    sub_a = cols // WIDTH_A
    within_a = cols % DIFF_DIM
    rot_mask_a = in_a & (sub_a < 2) & (within_a < rot_a)
    cb = cols - 3 * WIDTH_A
    sub_b = cb // WIDTH_B
    within_b = cb % HEAD_DIM
    rot_mask_b = (~in_a) & (within_b < rot_b)

    first_half = np.where(in_a, within_a < rot_a // 2, within_b < rot_b // 2)
    rot = rot_mask_a | rot_mask_b
    scale = np.ones(ROPE_COLS, np.float32)
    scale[in_a & (sub_a == 0)] = DIFF_DIM ** -0.5 * LOG2E
    scale[(~in_a) & (sub_b == 0)] = HEAD_DIM ** -0.5 * LOG2E

    def spread(tab_a, tab_b, fill):
        def groups(tab, width, reps):
            pad = jnp.full((seq, width - 2 * tab.shape[1]), fill, F32)
            return jnp.tile(jnp.concatenate([tab, tab, pad], axis=1), (1, reps))
        return jnp.concatenate([groups(tab_a, DIFF_DIM, 2 * WIDTH_A // DIFF_DIM),
                                jnp.full((seq, WIDTH_A), fill, F32),
                                groups(tab_b, HEAD_DIM, 2 * WIDTH_B // HEAD_DIM)], axis=1)

    cm = spread(jnp.cos(ang_a), jnp.cos(ang_b), 1.0) * jnp.asarray(scale)[None, :]
    sn = spread(jnp.sin(ang_a), jnp.sin(ang_b), 0.0) * jnp.asarray(scale)[None, :]
    ma = np.where(rot & first_half, -1.0, 0.0).astype(np.float32)[None, :]
    mb = np.where(rot & ~first_half, 1.0, 0.0).astype(np.float32)[None, :]
    return cm, sn, jnp.asarray(ma), jnp.asarray(mb)


PROJ_TM = 512
W_CAST_ROWS = 128


def _cast_rows_to_bf16(src_ref, dst_ref, rows):
    def body(i, c):
        r = pl.multiple_of(i * W_CAST_ROWS, W_CAST_ROWS)
        dst_ref[pl.ds(r, W_CAST_ROWS), :] = src_ref[pl.ds(r, W_CAST_ROWS), :].astype(BF16)
        return c
    lax.fori_loop(0, rows // W_CAST_ROWS, body, 0)


def _proj_kernel(x_ref, w_ref, cm_ref, sn_ref, ma_ref, mb_ref, o_ref, *rest):
    res_refs, (wbf_ref, band_ref, split_ref) = rest[:-3], rest[-3:]

    @pl.when((pl.program_id(0) == 0) & (pl.program_id(1) == 0))
    def _():
        _cast_rows_to_bf16(w_ref, wbf_ref, D_MODEL)

    xb = x_ref[0].astype(BF16)
    tm = xb.shape[0]
    n_chunks = PROJ_W // COL_CHUNK
    for c in range(n_chunks):
        lo, hi = c * COL_CHUNK, (c + 1) * COL_CHUNK
        acc = jnp.dot(xb, wbf_ref[:, lo:hi], preferred_element_type=F32)
        if hi <= ROPE_COLS:
            half = (DIFF_DIM if c == 0 else HEAD_DIM) // ROPE_FRACTION // 2
            up = pltpu.roll(acc, COL_CHUNK - half, 1)
            dn = pltpu.roll(acc, half, 1)
            acc = acc * cm_ref[:, lo:hi] + (up * ma_ref[:, lo:hi] + dn * mb_ref[:, lo:hi]) * sn_ref[:, lo:hi]
        elif c == 2:
            lane = lax.broadcasted_iota(jnp.int32, (1, COL_CHUNK), 1)
            acc = acc * jnp.where(lane >= WIDTH_B, HEAD_DIM ** -0.5 * LOG2E, 1.0)
        o_ref[0, :, lo:hi] = acc.astype(BF16)
        for col in range(max(lo, BAND_LO), min(hi, BAND_LO + BAND_COLS), LANES):
            band_ref[(col - BAND_LO) // LANES] = acc[:, col - lo:col - lo + LANES]

    (d1, d2), (r1_ref, r2_ref) = DILATIONS[1:], res_refs
    for page in range(BAND_COLS // LANES):
        cs = slice(page * LANES, (page + 1) * LANES)
        for r in range(d1):
            rows = band_ref[page, pl.ds(r, tm // d1, stride=d1), :]
            r1_ref[0, r, :, cs] = rows.astype(BF16)
            split_ref[page, r] = rows
        for r in range(d2):
            r2_ref[0, r, :, cs] = split_ref[page, r % d1, pl.ds(r // d1, tm // d2, stride=d1), :].astype(BF16)


def _project(x, w, layer, tables):
    B, S, _ = x.shape
    cm, sn, ma, mb = tables
    tm = PROJ_TM
    res_specs = [pl.BlockSpec((1, d, tm // d, BAND_COLS), lambda p, b: (b, 0, p, 0)) for d in DILATIONS[1:]]
    res_shapes = [jax.ShapeDtypeStruct((B, d, S // d, BAND_COLS), BF16) for d in DILATIONS[1:]]
    return pl.pallas_call(
        _proj_kernel,
        grid=(S // tm, B),
        in_specs=[
            pl.BlockSpec((1, tm, D_MODEL), lambda p, b: (b, p, 0)),
            pl.BlockSpec((None, D_MODEL, PROJ_W), lambda p, b: (layer, 0, 0), pipeline_mode=pl.Buffered(1)),
            pl.BlockSpec((tm, ROPE_COLS), lambda p, b: (p, 0)),
            pl.BlockSpec((tm, ROPE_COLS), lambda p, b: (p, 0)),
            pl.BlockSpec((1, ROPE_COLS), lambda p, b: (0, 0)),
            pl.BlockSpec((1, ROPE_COLS), lambda p, b: (0, 0)),
        ],
        out_specs=[pl.BlockSpec((1, tm, PROJ_W), lambda p, b: (b, p, 0))] + res_specs,
        out_shape=[jax.ShapeDtypeStruct((B, S, PROJ_W), BF16)] + res_shapes,
        scratch_shapes=[pltpu.VMEM((D_MODEL, PROJ_W), BF16),
                        pltpu.VMEM((BAND_COLS // LANES, tm, LANES), F32),
                        pltpu.VMEM((BAND_COLS // LANES, DILATIONS[1], tm // DILATIONS[1], LANES), F32)],
        compiler_params=_cparams(("arbitrary", "arbitrary")),
        name="proj_in",
    )(x, w, cm, sn, ma, mb)


ATT_A_TQ = 256
ATT_A_UNROLL = 8


def _nt_dot(a, b):
    return lax.dot_general(a, b, (((1,), (1,)), ((), ())), preferred_element_type=F32)


def _with_ones(v):
    return jnp.concatenate([v, jnp.ones_like(v)], axis=1)


def _softmax_times_v(e, v1):
    nd = jnp.dot(e, v1, preferred_element_type=F32)
    return nd[:, :LANES] * (1.0 / nd[:, LANES:])


def _attn_a_kernel(lam_ref, g_ref, q_ref, k_ref, v_ref, o_ref, *, lam_init, seq):
    lp = lam_ref[...]
    lam = (jnp.exp(jnp.sum(lp[0:1] * lp[1:2], axis=-1, keepdims=True))
           - jnp.exp(jnp.sum(lp[2:3] * lp[3:4], axis=-1, keepdims=True)) + lam_init)
    lane = lax.broadcasted_iota(jnp.int32, (1, LANES), 1)
    first = lane < HEAD_DIM
    gain = g_ref[...] * (1.0 - lam_init)
    tq = ATT_A_TQ

    def qblock(i, carry):
        r0 = pl.multiple_of(i * tq, tq)
        q = q_ref[0, pl.ds(r0, tq), :]
        k = k_ref[0]
        v1 = _with_ones(v_ref[0])
        outs = []
        for hh in range(2):
            maps = []
            for m in range(2):
                lo = hh * HEAD_DIM + m * DIFF_DIM
                qm = jnp.where((lane >= lo) & (lane < lo + DIFF_DIM), q, jnp.zeros_like(q))
                s = _nt_dot(qm, k)
                e = jnp.exp2(s - jnp.max(s, axis=-1, keepdims=True)).astype(BF16)
                maps.append(_softmax_times_v(e, v1))
            outs.append(maps[0] - lam * maps[1])
        o = jnp.where(first, outs[0], outs[1])
        sq = o * o
        ss0 = jnp.sum(jnp.where(first, sq, 0.0), axis=-1, keepdims=True)
        ss1 = jnp.sum(jnp.where(first, 0.0, sq), axis=-1, keepdims=True)
        ms = jnp.where(first, ss0, ss1) * (1.0 / HEAD_DIM)
        o_ref[0, pl.ds(r0, tq), :] = (o * lax.rsqrt(ms + LN_EPS) * gain).astype(BF16)
        return carry

    lax.fori_loop(0, seq // tq, qblock, 0, unroll=ATT_A_UNROLL)


def _attn_a(qkv, diff_lambda, norm_g, lam_init):
    B, S, _ = qkv.shape
    n_pairs = WIDTH_A // LANES
    g2 = jnp.tile(norm_g.astype(F32), LANES // HEAD_DIM)[None, :]
    blk = lambda off: pl.BlockSpec((1, S, LANES), lambda b, h, off=off: (b, 0, off + h))
    return pl.pallas_call(
        functools.partial(_attn_a_kernel, lam_init=lam_init, seq=S),
        grid=(B, n_pairs),
        in_specs=[
            pl.BlockSpec((4, DIFF_DIM), lambda b, h: (0, 0)),
            pl.BlockSpec((1, LANES), lambda b, h: (0, 0)),
            blk(0), blk(n_pairs), blk(2 * n_pairs),
        ],
        out_specs=pl.BlockSpec((1, S, LANES), lambda b, h: (b, 0, h)),
        out_shape=jax.ShapeDtypeStruct((B, S, WIDTH_A), BF16),
        compiler_params=_cparams(("arbitrary", "arbitrary")),
        name="attn_diff",
    )(diff_lambda.astype(F32), g2, qkv, qkv, qkv)


BAND_TQ = 128
BAND_CHAINS = 48


def _stack_pair(q, lane):
    z = jnp.zeros_like(q)
    return jnp.concatenate([jnp.where(lane < HEAD_DIM, q, z), jnp.where(lane < HEAD_DIM, z, q)], axis=0)


def _attn_b_kernel(q_ref, k_ref, v_ref, o_ref, lse_ref, *, length, dil):
    tq = BAND_TQ
    win = min(tq + 2 * BAND_HALF, length)
    lane = lax.broadcasted_iota(jnp.int32, (1, LANES), 1)

    blocks = [(i * tq, min(max(i * tq - BAND_HALF, 0), length - win)) for i in range(length // tq)]
    row = lax.broadcasted_iota(jnp.int32, (2 * tq, win), 0)
    delta = lax.broadcasted_iota(jnp.int32, (2 * tq, win), 1) - jnp.where(row >= tq, row - tq, row)
    masks = {off: jnp.where(jnp.abs(delta - off) <= BAND_HALF, 0.0, NEG) for off in {q0 - k0 for q0, k0 in blocks}}

    def residue(r, carry):
        for q0, k0 in blocks:
            lse_tile = jnp.zeros((tq, LANES), F32)
            for pr in range(WIDTH_B // LANES):
                cs = slice(pr * LANES, (pr + 1) * LANES)
                q2 = _stack_pair(q_ref[0, r, q0:q0 + tq, cs], lane)
                s = _nt_dot(q2, k_ref[0, r, k0:k0 + win, cs]) + masks[q0 - k0]
                mx = jnp.max(s, axis=-1, keepdims=True)
                e = jnp.exp2(s - mx).astype(BF16)
                nd = jnp.dot(e, _with_ones(v_ref[0, r, k0:k0 + win, cs]), preferred_element_type=F32)
                l = nd[:, LANES:]
                o2 = nd[:, :LANES] * (1.0 / l)
                o_ref[0, r, q0:q0 + tq, cs] = jnp.where(lane < HEAD_DIM, o2[:tq], o2[tq:]).astype(BF16)
                lse2 = (mx + jnp.log2(l)) * LN2
                lse_tile = jnp.where(lane == 2 * pr, lse2[:tq], lse_tile)
                lse_tile = jnp.where(lane == 2 * pr + 1, lse2[tq:], lse_tile)
            lse_ref[0, r, q0:q0 + tq, :] = lse_tile
        return carry

    chains = len(blocks) * (WIDTH_B // LANES)
    lax.fori_loop(0, dil, residue, 0, unroll=min(dil, max(1, BAND_CHAINS // chains)))


def _attn_b(src, dil, first_blk):
    B, _, L, _ = src.shape
    blk = lambda off: pl.BlockSpec((1, dil, L, WIDTH_B), lambda b, off=off: (b, 0, 0, first_blk + off))
    return pl.pallas_call(
        functools.partial(_attn_b_kernel, length=L, dil=dil),
        grid=(B,),
        in_specs=[blk(0), blk(1), blk(2)],
        out_specs=[pl.BlockSpec((1, dil, L, WIDTH_B), lambda b: (b, 0, 0, 0)),
                   pl.BlockSpec((1, dil, L, LANES), lambda b: (b, 0, 0, 0))],
        out_shape=[jax.ShapeDtypeStruct((B, dil, L, WIDTH_B), BF16),
                   jax.ShapeDtypeStruct((B, dil, L, LANES), F32)],
        compiler_params=_cparams(("arbitrary",)),
        name=f"attn_band_d{dil}",
    )(src, src, src)


N_RPB_ROWS = 2 * NA_ROWS - 1
N_RPB_COLS = 2 * NA_COLS - 1
NA_KEYS = NA_ROWS * GRID_W
N_BIAS_VARIANTS = NA_ROWS
NA_ROWS_PER_STEP = 16


def _attn_c_kernel(u_ref, q_ref, k_ref, v_ref, o_ref, bias_ref, *, rows):
    lane = lax.broadcasted_iota(jnp.int32, (1, LANES), 1)
    n_pairs = WIDTH_C // LANES

    @pl.when(pl.program_id(0) == 0)
    def _():
        wide = N_RPB_ROWS * GRID_W
        c = lax.broadcasted_iota(jnp.int32, (GRID_W, wide), 0)
        kc = lax.broadcasted_iota(jnp.int32, (GRID_W, wide), 1) % GRID_W
        tidx = jnp.clip(kc - c + NA_COLS - 1, 0, N_RPB_COLS - 1)
        c5 = lax.broadcasted_iota(jnp.int32, (GRID_W, NA_KEYS), 0)
        kc5 = lax.broadcasted_iota(jnp.int32, (GRID_W, NA_KEYS), 1) % GRID_W
        cstart = jnp.clip(c5 - NA_COLS // 2, 0, GRID_W - NA_COLS)
        valid = (kc5 >= cstart) & (kc5 < cstart + NA_COLS)
        for h in range(N_HEADS_C):
            def tbody(t, acc, h=h):
                row = u_ref[h, pl.ds(t, 1), :]
                return jnp.where(tidx == t, row, acc)
            toep = lax.fori_loop(0, N_RPB_COLS, tbody, jnp.zeros((GRID_W, wide), F32))
            for var in range(N_BIAS_VARIANTS):
                tile = jnp.where(valid, toep[:, var * GRID_W:var * GRID_W + NA_KEYS] * LOG2E, NEG)
                bias_ref[var, h // 2, (h % 2) * GRID_W:(h % 2 + 1) * GRID_W, :] = tile

    def rbody(step, carry):
        for sub in range(NA_ROWS_PER_STEP):
            r = step * NA_ROWS_PER_STEP + sub
            rs = jnp.clip(r - NA_ROWS // 2, 0, rows - NA_ROWS)
            var = rs - r + NA_ROWS - 1
            q0 = pl.multiple_of(r * GRID_W, GRID_W)
            k0 = pl.multiple_of(rs * GRID_W, GRID_W)
            for pr in range(n_pairs):
                cs = slice(pr * LANES, (pr + 1) * LANES)
                q2 = _stack_pair(q_ref[0, pl.ds(q0, GRID_W), cs], lane)
                s = _nt_dot(q2, k_ref[0, pl.ds(k0, NA_KEYS), cs]) + bias_ref[var, pr]
                e = jnp.exp2(s - jnp.max(s, axis=-1, keepdims=True)).astype(BF16)
                o2 = _softmax_times_v(e, _with_ones(v_ref[0, pl.ds(k0, NA_KEYS), cs]))
                o_ref[0, pl.ds(q0, GRID_W), cs] = jnp.where(lane < HEAD_DIM, o2[:GRID_W], o2[GRID_W:]).astype(BF16)
        return carry

    lax.fori_loop(0, rows // NA_ROWS_PER_STEP, rbody, 0)


def _attn_c(qkv, rpb):
    B, S, _ = qkv.shape
    rows = S // GRID_W
    assert rows >= NA_ROWS
    u = jnp.repeat(jnp.transpose(rpb.astype(F32), (0, 2, 1)), GRID_W, axis=2)
    q_blk = (3 * WIDTH_A + 3 * WIDTH_B) // WIDTH_C
    blk = lambda off: pl.BlockSpec((1, S, WIDTH_C), lambda b, off=off: (b, 0, off))
    return pl.pallas_call(
        functools.partial(_attn_c_kernel, rows=rows),
        grid=(B,),
        in_specs=[pl.BlockSpec((N_HEADS_C, N_RPB_COLS, N_RPB_ROWS * GRID_W), lambda b: (0, 0, 0)),
                  blk(q_blk), blk(q_blk + 1), blk(q_blk + 2)],
        out_specs=pl.BlockSpec((1, S, WIDTH_C), lambda b: (b, 0, 0)),
        out_shape=jax.ShapeDtypeStruct((B, S, WIDTH_C), BF16),
        scratch_shapes=[pltpu.VMEM((N_BIAS_VARIANTS, WIDTH_C // LANES, LANES, NA_KEYS), F32)],
        compiler_params=_cparams(("arbitrary",)),
        name="attn_nbr",
    )(u, qkv, qkv, qkv)


OUT_TM = 512
OUT_SPLIT = 2


def _split_bf16(x):
    hi = x.astype(BF16)
    return hi, (x - hi.astype(F32)).astype(BF16)


def _outproj_kernel(x_ref, oa_ref, ob0_ref, ob1_ref, ob2_ref, l0_ref, l1_ref, l2_ref, oc_ref,
                    beta_ref, w_ref, g_ref, b_ref, e_ref, *rest, alpha, route):
    if route:
        wr_ref, o_ref, meta_ref, fields_ref, cnt_ref, wbf_ref, il_ref, carry_ref = rest
    else:
        o_ref, wbf_ref, il_ref = rest

    @pl.when(pl.program_id(0) == 0)
    def _():
        _cast_rows_to_bf16(w_ref, wbf_ref, D_MODEL)
        if route:
            carry_ref[...] = jnp.zeros_like(carry_ref)

    tm = x_ref.shape[0]
    n_pages = WIDTH_B // LANES
    for slot, (dil, ob_r, l_r) in enumerate(zip(DILATIONS[1:], (ob1_ref, ob2_ref), (l1_ref, l2_ref))):
        for r in range(dil):
            rows = pl.ds(r, tm // dil, stride=dil)
            for page in range(n_pages):
                il_ref[slot, page, rows, :] = ob_r[0, r, :, page * LANES:(page + 1) * LANES].astype(F32)
            il_ref[slot, n_pages, rows, :] = l_r[0, r]

    beta = beta_ref[...]
    half = tm // OUT_SPLIT
    for part in range(OUT_SPLIT):
        rs = slice(part * half, (part + 1) * half)
        l0, l1, l2 = l0_ref[0, 0, rs, :], il_ref[0, n_pages, rs, :], il_ref[1, n_pages, rs, :]
        lm = jnp.maximum(jnp.maximum(l0, l1), l2)
        e0, e1, e2 = jnp.exp(l0 - lm), jnp.exp(l1 - lm), jnp.exp(l2 - lm)
        inv = 1.0 / (e0 + e1 + e2)
        wexp = [jnp.dot(jnp.concatenate(_split_bf16(e * inv), axis=1), e_ref[...], preferred_element_type=F32)
                for e in (e0, e1, e2)]
        pages = []
        for page in range(n_pages):
            cs = slice(page * LANES, (page + 1) * LANES)
            branch_vals = (ob0_ref[0, 0, rs, cs].astype(F32), il_ref[0, page, rs, :], il_ref[1, page, rs, :])
            pages.append(sum(w[:, cs] * val for w, val in zip(wexp, branch_vals)))
        ob = jnp.concatenate(pages, axis=1)

        ma = (oa_ref[rs, :].astype(F32) * beta[:, :WIDTH_A]).astype(BF16)
        mb = (ob * beta[:, WIDTH_A:WIDTH_A + WIDTH_B]).astype(BF16)
        mc = (oc_ref[rs, :].astype(F32) * beta[:, WIDTH_A + WIDTH_B:]).astype(BF16)
        acc = jnp.dot(jnp.concatenate([ma, mb, mc], axis=1), wbf_ref[...], preferred_element_type=F32)
        y = _layer_norm(alpha * x_ref[rs, :] + acc, g_ref[...], b_ref[...])
        if route:
            _store_token_tiles(o_ref, y, row0=part * half)
            fields = _route_rows(y, wr_ref, carry_ref)
            fields_ref[:, rs] = fields
            meta_ref[rs, :] = jnp.concatenate([fields, jnp.zeros((LANES - SUBLANES, half), F32)], axis=0).T
        else:
            o_ref[rs, :] = y
    if route:
        cnt_ref[...] = carry_ref[...]


def _store_token_tiles(ref, y, row0=0):
    rows = y.shape[0]
    for j in range(D_MODEL // LANES):
        ref[pl.ds(row0 * SUBLANES + j, rows, stride=SUBLANES), :] = y[:, j * LANES:(j + 1) * LANES]


def _load_token_tiles(ref, rows, row0=0):
    return jnp.concatenate(
        [ref[pl.ds(row0 * SUBLANES + j, rows, stride=SUBLANES), :] for j in range(D_MODEL // LANES)], axis=1)


def _outproj(x2, oa, obs, lses, oc, beta, w, layer, g, b, alpha, w_router=None):
    route = w_router is not None
    N = x2.shape[0]
    tm = OUT_TM
    steps_per_seq = obs[0].shape[2] // tm
    expand = np.zeros((2 * LANES, WIDTH_B), np.float32)
    for h in range(N_HEADS_B):
        expand[h, h * HEAD_DIM:(h + 1) * HEAD_DIM] = 1.0
        expand[LANES + h, h * HEAD_DIM:(h + 1) * HEAD_DIM] = 1.0
    row = lambda w_: pl.BlockSpec((tm, w_), lambda i: (i, 0))
    full = lambda r, c, **kw: pl.BlockSpec((r, c), lambda i: (0, 0), **kw)
    res = lambda d, w_: pl.BlockSpec((1, d, tm // d, w_),
                                     lambda i: (i // steps_per_seq, 0, i % steps_per_seq, 0))
    in_specs = ([row(D_MODEL), row(WIDTH_A)] + [res(d, WIDTH_B) for d in DILATIONS]
                + [res(d, LANES) for d in DILATIONS] + [row(WIDTH_C), full(1, D_MODEL),
                   pl.BlockSpec((None, D_MODEL, D_MODEL), lambda i: (layer, 0, 0), pipeline_mode=pl.Buffered(1)),
                   full(1, D_MODEL), full(1, D_MODEL), full(2 * LANES, WIDTH_B)])
    args = [x2, oa, obs[0], obs[1], obs[2], lses[0], lses[1], lses[2], oc,
            beta[None, :].astype(F32), w, g[None, :].astype(F32), b[None, :].astype(F32), jnp.asarray(expand, BF16)]
    scratch = [pltpu.VMEM((D_MODEL, D_MODEL), BF16),
               pltpu.VMEM((len(DILATIONS) - 1, WIDTH_B // LANES + 1, tm, LANES), F32)]
    if route:
        wr = jnp.zeros((D_MODEL, LANES), F32).at[:, :N_EXPERTS].set(w_router.astype(F32))
        wr_hi = wr.astype(BF16)
        in_specs.append(full(D_MODEL, 2 * LANES))
        args.append(jnp.concatenate([wr_hi, (wr - wr_hi.astype(F32)).astype(BF16)], axis=1))
        out_specs = [pl.BlockSpec((tm * SUBLANES, LANES), lambda i: (i, 0)), row(LANES),
                     pl.BlockSpec((SUBLANES, tm), lambda i: (0, i)), full(ROUTE_ROWS, LANES)]
        out_shape = [jax.ShapeDtypeStruct((N * SUBLANES, LANES), F32), jax.ShapeDtypeStruct((N, LANES), F32),
                     jax.ShapeDtypeStruct((SUBLANES, N), F32), jax.ShapeDtypeStruct((ROUTE_ROWS, LANES), F32)]
        scratch.append(pltpu.VMEM((ROUTE_ROWS, LANES), F32))
    else:
        out_specs = row(D_MODEL)
        out_shape = jax.ShapeDtypeStruct((N, D_MODEL), F32)
    return pl.pallas_call(
        functools.partial(_outproj_kernel, alpha=alpha, route=route),
        grid=(N // tm,),
        in_specs=in_specs,
        out_specs=out_specs,
        out_shape=out_shape,
        scratch_shapes=scratch,
        compiler_params=_cparams(("arbitrary",)),
        name="merge_outproj_ln_route" if route else "merge_outproj_ln",
    )(*args)


CAST_ROWS = 256
DENSE_TM = 512
DENSE_TF = 256


def _cast_kernel(w_ref, o_ref):
    o_ref[...] = w_ref[...].astype(BF16)


def _to_bf16(w):
    rows, cols = w.shape
    return pl.pallas_call(
        _cast_kernel,
        grid=(rows // CAST_ROWS,),
        in_specs=[pl.BlockSpec((CAST_ROWS, cols), lambda i: (i, 0))],
        out_specs=pl.BlockSpec((CAST_ROWS, cols), lambda i: (i, 0)),
        out_shape=jax.ShapeDtypeStruct((rows, cols), BF16),
        compiler_params=_cparams(("arbitrary",)),
        name="cast_bf16",
    )(w)


def _dense_ffn_kernel(x_ref, wg_ref, wu_ref, wd_ref, g_ref, b_ref, o_ref, h_ref, *, alpha):
    x = x_ref[...]
    xb = x.astype(BF16)
    for c in range(h_ref.shape[1] // DENSE_TF):
        cs = slice(c * DENSE_TF, (c + 1) * DENSE_TF)
        gate = jnp.dot(xb, wg_ref[:, cs], preferred_element_type=F32)
        up = jnp.dot(xb, wu_ref[:, cs], preferred_element_type=F32)
        h_ref[:, cs] = (gate * jax.nn.sigmoid(gate) * up).astype(BF16)
    y = jnp.dot(h_ref[...], wd_ref[...], preferred_element_type=F32)
    o_ref[...] = _layer_norm(alpha * x + y, g_ref[...], b_ref[...])


def _dense_ffn(x, w_gate, w_up, w_down, g, b, alpha):
    n_tok = x.shape[0]
    ff = w_gate.shape[1]
    tm = DENSE_TM
    row = pl.BlockSpec((tm, D_MODEL), lambda i: (i, 0))
    resident = lambda r, c: pl.BlockSpec((r, c), lambda i: (0, 0), pipeline_mode=pl.Buffered(1))
    return pl.pallas_call(
        functools.partial(_dense_ffn_kernel, alpha=alpha),
        grid=(n_tok // tm,),
        in_specs=[row, resident(D_MODEL, ff), resident(D_MODEL, ff), resident(ff, D_MODEL),
                  pl.BlockSpec((1, D_MODEL), lambda i: (0, 0)), pl.BlockSpec((1, D_MODEL), lambda i: (0, 0))],
        out_specs=row,
        out_shape=jax.ShapeDtypeStruct((n_tok, D_MODEL), F32),
        scratch_shapes=[pltpu.VMEM((tm, ff), BF16)],
        compiler_params=_cparams(("arbitrary",)),
        name="swiglu_dense",
    )(x, _to_bf16(w_gate), _to_bf16(w_up), _to_bf16(w_down), g[None, :].astype(F32), b[None, :].astype(F32))


FFN_TM = 1024
FFN_TF = 512
FFN_SUB = 256


def _ffn_kernel(te_ref, nt_ref, rows_ref, x_ref, wg_ref, wu_ref, wd_ref, o_ref, xb_ref, acc_ref, *, n_chunks):
    i, j = pl.program_id(0), pl.program_id(1)
    tm = xb_ref.shape[0]

    def swiglu_rows(rs):
        xb = xb_ref[rs, :]
        gate = jnp.dot(xb, wg_ref[...].astype(BF16), preferred_element_type=F32)
        up = jnp.dot(xb, wu_ref[...].astype(BF16), preferred_element_type=F32)
        h = (gate * jax.nn.sigmoid(gate) * up).astype(BF16)
        acc_ref[rs, :] += jnp.dot(h, wd_ref[...].astype(BF16), preferred_element_type=F32)

    @pl.when(i < nt_ref[0])
    def _():
        @pl.when(j == 0)
        def _():
            xb_ref[...] = _load_token_tiles(x_ref, tm).astype(BF16)
            acc_ref[...] = jnp.zeros_like(acc_ref)

        sparse_tile = rows_ref[i] <= tm - FFN_SUB

        @pl.when(jnp.logical_not(sparse_tile))
        def _():
            swiglu_rows(slice(None))

        for sb in range(tm // FFN_SUB - 1):
            @pl.when(sparse_tile & (sb * FFN_SUB < rows_ref[i]))
            def _(sb=sb):
                swiglu_rows(slice(sb * FFN_SUB, (sb + 1) * FFN_SUB))

        @pl.when(j == n_chunks - 1)
        def _():
            _store_token_tiles(o_ref, acc_ref[...])

    @pl.when((i >= nt_ref[0]) & (j == n_chunks - 1))
    def _():
        o_ref[...] = jnp.zeros_like(o_ref)


def _grouped_ffn(x, w_gate, w_up, w_down, tile_expert, n_tiles, tile_rows):
    tm, tf = FFN_TM, FFN_TF
    n_chunks = w_gate.shape[-1] // tf
    max_tiles = tile_expert.shape[0]

    def tile_ix(i, nt):
        return jnp.minimum(i, nt[0] - 1)

    def chunk_ix(i, j, nt):
        return jnp.where(i < nt[0], j, n_chunks - 1)

    blk = (tm * SUBLANES, LANES)
    in_specs = [
        pl.BlockSpec(blk, lambda i, j, te, nt, tr: (tile_ix(i, nt), 0)),
        pl.BlockSpec((None, D_MODEL, tf), lambda i, j, te, nt, tr: (te[tile_ix(i, nt)], 0, chunk_ix(i, j, nt))),
        pl.BlockSpec((None, D_MODEL, tf), lambda i, j, te, nt, tr: (te[tile_ix(i, nt)], 0, chunk_ix(i, j, nt))),
        pl.BlockSpec((None, tf, D_MODEL), lambda i, j, te, nt, tr: (te[tile_ix(i, nt)], chunk_ix(i, j, nt), 0)),
    ]
    return pl.pallas_call(
        functools.partial(_ffn_kernel, n_chunks=n_chunks),
        grid_spec=pltpu.PrefetchScalarGridSpec(
            num_scalar_prefetch=3,
            grid=(max_tiles, n_chunks),
            in_specs=in_specs,
            out_specs=pl.BlockSpec(blk, lambda i, j, te, nt, tr: (i, 0)),
            scratch_shapes=[pltpu.VMEM((tm, D_MODEL), BF16), pltpu.VMEM((tm, D_MODEL), F32)],
        ),
        out_shape=jax.ShapeDtypeStruct(x.shape, F32),
        compiler_params=_cparams(("arbitrary", "arbitrary")),
        name="swiglu_grouped",
    )(tile_expert, n_tiles, tile_rows, x, w_gate, w_up, w_down)


META_E1, META_E2, META_R1, META_R2, META_G1, META_G2 = range(6)


ROUTE_ROWS = 16


def _route_rows(x, wr_ref, carry_ref):
    tb = x.shape[0]
    xh, xl = _split_bf16(x)
    hi_lo = jnp.dot(xh, wr_ref[...], preferred_element_type=F32)
    logits = hi_lo[:, :LANES] + hi_lo[:, LANES:] + jnp.dot(xl, wr_ref[:, :LANES], preferred_element_type=F32)
    logits = logits.T[:ROUTE_ROWS]
    row = lax.broadcasted_iota(jnp.int32, (ROUTE_ROWS, tb), 0).astype(F32)
    logits = jnp.where(row < N_EXPERTS, logits, -jnp.inf)
    m1 = jnp.max(logits, axis=0, keepdims=True)
    i1 = jnp.min(jnp.where(logits == m1, row, float(ROUTE_ROWS)), axis=0, keepdims=True)
    rest = jnp.where(row == i1, -jnp.inf, logits)
    m2 = jnp.max(rest, axis=0, keepdims=True)
    i2 = jnp.min(jnp.where(rest == m2, row, float(ROUTE_ROWS)), axis=0, keepdims=True)
    e21 = jnp.exp(m2 - m1)
    g1 = 1.0 / (1.0 + e21)
    g2 = e21 * g1

    member = ((row == i1) | (row == i2)).astype(F32)
    r_i = lax.broadcasted_iota(jnp.int32, (tb, tb), 0)
    c_i = lax.broadcasted_iota(jnp.int32, (tb, tb), 1)
    earlier = (r_i < c_i).astype(BF16)
    before = jnp.dot(member.astype(BF16), earlier, preferred_element_type=F32)
    rank = carry_ref[:, 0:1] + before
    r1 = jnp.sum(jnp.where(row == i1, rank, 0.0), axis=0, keepdims=True)
    r2 = jnp.sum(jnp.where(row == i2, rank, 0.0), axis=0, keepdims=True)
    carry_ref[...] += jnp.sum(member, axis=1, keepdims=True)

    fields = [None] * SUBLANES
    for slot, val in ((META_E1, i1), (META_E2, i2), (META_R1, r1), (META_R2, r2), (META_G1, g1), (META_G2, g2)):
        fields[slot] = val
    return jnp.concatenate([f if f is not None else jnp.zeros((1, tb), F32) for f in fields], axis=0)


MOVE_TB = 1024
PAD_CHUNK = 64
DMA_ISSUE_UNROLL = 8


def _tile_rows(ref, idx):
    return ref.at[pl.ds(pl.multiple_of(idx * SUBLANES, SUBLANES), SUBLANES), :]


def _zero_fill(pad_ref, xs_hbm, zbuf, zsem):
    zbuf[...] = jnp.zeros_like(zbuf)
    one = _tile_rows(zbuf, 0)
    for wait in (False, True):
        for seg in range(pad_ref.shape[1]):
            start, n = pad_ref[0, seg], pad_ref[1, seg]
            n_single = n % PAD_CHUNK

            def single(t, c):
                cp = pltpu.make_async_copy(one, _tile_rows(xs_hbm, start + t), zsem)
                cp.wait() if wait else cp.start()
                return c

            def bulk(t, c):
                r0 = pl.multiple_of((start + n_single + t * PAD_CHUNK) * SUBLANES, SUBLANES)
                cp = pltpu.make_async_copy(zbuf, xs_hbm.at[pl.ds(r0, PAD_CHUNK * SUBLANES), :], zsem)
                cp.wait() if wait else cp.start()
                return c

            lax.fori_loop(0, n_single, single, 0)
            lax.fori_loop(0, n // PAD_CHUNK, bulk, 0)


def _dispatch_kernel(dest_ref, pad_ref, x_ref, xs_hbm, zbuf, sem, zsem):
    tb = MOVE_TB

    @pl.when(pl.program_id(0) == 0)
    def _():
        _zero_fill(pad_ref, xs_hbm, zbuf, zsem)

    def body(t, c):
        src = _tile_rows(x_ref, t)
        for k in range(TOP_K):
            pltpu.make_async_copy(src, _tile_rows(xs_hbm, dest_ref[0, 0, TOP_K * t + k]), sem).start(priority=k)
        return c

    lax.fori_loop(0, tb, body, 0, unroll=DMA_ISSUE_UNROLL)
    n_rows = TOP_K * tb * SUBLANES
    pltpu.make_async_copy(xs_hbm.at[pl.ds(0, n_rows), :], xs_hbm.at[pl.ds(0, n_rows), :], sem).wait()


def _dispatch(xt, dest, pads, n_slots):
    n_tok = xt.shape[0] // SUBLANES
    tb = MOVE_TB
    dest3 = dest.reshape(n_tok // tb, 1, TOP_K * tb)
    return pl.pallas_call(
        _dispatch_kernel,
        grid=(n_tok // tb,),
        in_specs=[pl.BlockSpec((1, 1, TOP_K * tb), lambda i: (i, 0, 0), memory_space=pltpu.SMEM),
                  pl.BlockSpec(memory_space=pltpu.SMEM),
                  pl.BlockSpec((tb * SUBLANES, LANES), lambda i: (i, 0))],
        out_specs=pl.BlockSpec(memory_space=pl.ANY),
        out_shape=jax.ShapeDtypeStruct((n_slots * SUBLANES, LANES), F32),
        scratch_shapes=[pltpu.VMEM((PAD_CHUNK * SUBLANES, LANES), F32),
                        pltpu.SemaphoreType.DMA(()), pltpu.SemaphoreType.DMA(())],
        compiler_params=_cparams(("arbitrary",)),
        name="moe_dispatch",
    )(dest3, pads, xt)


COMB_TB = 256


def _combine_kernel(dest_ref, next_ref, y_hbm, x_ref, meta_ref, g_ref, b_ref, o_ref, ybuf, sem, *, alpha, n_steps):
    tb = COMB_TB
    i = pl.program_id(0)
    slot = i % 2

    def gather(d_ref, step, s):
        base = (step % (MOVE_TB // tb)) * (TOP_K * tb)

        def body(t, c):
            for k in range(TOP_K):
                pltpu.make_async_copy(_tile_rows(y_hbm, d_ref[0, 0, base + TOP_K * t + k]),
                                      _tile_rows(ybuf.at[s], k * tb + t), sem.at[s]).start(priority=k)
            return c
        lax.fori_loop(0, tb, body, 0, unroll=DMA_ISSUE_UNROLL)

    @pl.when(i == 0)
    def _():
        gather(dest_ref, i, 0)

    @pl.when(i + 1 < n_steps)
    def _():
        gather(next_ref, i + 1, 1 - slot)

    cur = ybuf.at[slot]
    pltpu.make_async_copy(cur, cur, sem.at[slot]).wait()

    meta = meta_ref[...]
    lane = lax.broadcasted_iota(jnp.int32, meta.shape, 1)
    g1 = jnp.sum(jnp.where(lane == META_G1, meta, 0.0), axis=-1, keepdims=True)
    g2 = jnp.sum(jnp.where(lane == META_G2, meta, 0.0), axis=-1, keepdims=True)
    x = _load_token_tiles(x_ref, tb)
    f = g1 * _load_token_tiles(cur, tb) + g2 * _load_token_tiles(cur, tb, row0=tb)
    o_ref[...] = _layer_norm(alpha * x + f, g_ref[...], b_ref[...])


def _combine(yt, xt, dest, meta, g, b, alpha):
    n_tok = xt.shape[0] // SUBLANES
    tb = COMB_TB
    n_steps = n_tok // tb
    per_blk = MOVE_TB // tb
    dest3 = dest.reshape(n_tok // MOVE_TB, 1, TOP_K * MOVE_TB)
    return pl.pallas_call(
        functools.partial(_combine_kernel, alpha=alpha, n_steps=n_steps),
        grid=(n_steps,),
        in_specs=[pl.BlockSpec((1, 1, TOP_K * MOVE_TB), lambda i: (i // per_blk, 0, 0), memory_space=pltpu.SMEM),
                  pl.BlockSpec((1, 1, TOP_K * MOVE_TB),
                               lambda i: (jnp.minimum(i + 1, n_steps - 1) // per_blk, 0, 0),
                               memory_space=pltpu.SMEM),
                  pl.BlockSpec(memory_space=pl.ANY),
                  pl.BlockSpec((tb * SUBLANES, LANES), lambda i: (i, 0)),
                  pl.BlockSpec((tb, LANES), lambda i: (i, 0)),
                  pl.BlockSpec((1, D_MODEL), lambda i: (0, 0)),
                  pl.BlockSpec((1, D_MODEL), lambda i: (0, 0))],
        out_specs=pl.BlockSpec((tb, D_MODEL), lambda i: (i, 0)),
        out_shape=jax.ShapeDtypeStruct((n_tok, D_MODEL), F32),
        scratch_shapes=[pltpu.VMEM((2, TOP_K * tb * SUBLANES, LANES), F32), pltpu.SemaphoreType.DMA((2,))],
        compiler_params=_cparams(("arbitrary",)),
        name="moe_combine_ln",
    )(dest3, dest3, yt, xt, meta, g[None, :].astype(F32), b[None, :].astype(F32))


class MoePlan(NamedTuple):
    dest: jax.Array
    tile_expert: jax.Array
    tile_rows: jax.Array
    n_tiles: jax.Array
    pads: jax.Array
    n_slots: int


def _moe_plan(fields, counts, n_tok):
    tm = FFN_TM
    max_tiles = n_tok * TOP_K // tm + N_EXPERTS
    cnt = counts[:N_EXPERTS, 0].astype(jnp.int32)
    tiles = (cnt + tm - 1) // tm
    tile_end = jnp.cumsum(tiles)
    start = (tile_end - tiles) * tm
    e = fields[META_E1:META_E2 + 1].astype(jnp.int32)
    r = fields[META_R1:META_R2 + 1].astype(jnp.int32)
    experts = jnp.arange(N_EXPERTS, dtype=jnp.int32)[:, None, None]
    group_start = jnp.sum(jnp.where(e[None] == experts, start[:, None, None], 0), axis=0)
    dest = (group_start + r).T.reshape(-1)
    tile_ids = jnp.arange(max_tiles, dtype=jnp.int32)
    tile_expert = jnp.minimum(jnp.sum((tile_ids[:, None] >= tile_end[None, :]).astype(jnp.int32), axis=1),
                              N_EXPERTS - 1)
    first_tile = (tile_end - tiles)[tile_expert]
    tile_rows = jnp.clip(cnt[tile_expert] - (tile_ids - first_tile) * tm, 0, tm).astype(jnp.int32)
    n_tiles = tile_end[-1:]
    pads = jnp.stack([jnp.concatenate([start + cnt, n_tiles * tm]),
                      jnp.concatenate([tiles * tm - cnt, (max_tiles - n_tiles) * tm])]).astype(jnp.int32)
    return MoePlan(dest, tile_expert, tile_rows, n_tiles.astype(jnp.int32), pads, max_tiles * tm)


def kernel(x, w_in, w_out, beta_mix, diff_lambda, diff_norm_g, na_rpb, ln1_g, ln1_b, ln2_g, ln2_b,
           ffn_w_gate, ffn_w_up, ffn_w_down, moe_w_router, moe_w_gate, moe_w_up, moe_w_down):
    B, S, D = x.shape
    assert D == D_MODEL and S % (max(DILATIONS) * BAND_TQ) == 0 and S % PROJ_TM == 0
    depth = w_in.shape[0]
    alpha = (2 * depth) ** 0.25
    n_tok = B * S
    tables = _rope_tables(S)
    x = x.astype(F32)
    for l in range(depth):
        lam_init = 0.8 - 0.6 * math.exp(-0.3 * l)
        moe = l % 2 == 1
        qkv, *res_major = _project(x, w_in, l, tables)
        oa = _attn_a(qkv, diff_lambda[l], diff_norm_g[l], lam_init).reshape(n_tok, WIDTH_A)
        band = [_attn_b(qkv[:, None], DILATIONS[0], BAND_LO // WIDTH_B)]
        band += [_attn_b(src, d, 0) for d, src in zip(DILATIONS[1:], res_major)]
        oc = _attn_c(qkv, na_rpb[l]).reshape(n_tok, WIDTH_C)
        j = l // 2
        mixed = _outproj(x.reshape(n_tok, D), oa, [o for o, _ in band], [s for _, s in band], oc,
                         beta_mix[l], w_out, l, ln1_g[l], ln1_b[l], alpha,
                         w_router=moe_w_router[j] if moe else None)
        if not moe:
            x = _dense_ffn(mixed, ffn_w_gate[j], ffn_w_up[j], ffn_w_down[j], ln2_g[l], ln2_b[l], alpha)
        else:
            x1, meta, fields, counts = mixed
            plan = _moe_plan(fields, counts, n_tok)
            xs = _dispatch(x1, plan.dest, plan.pads, plan.n_slots)
            ys = _grouped_ffn(xs, moe_w_gate[j], moe_w_up[j], moe_w_down[j],
                              plan.tile_expert, plan.n_tiles, plan.tile_rows)
            x = _combine(ys, x1, plan.dest, meta, ln2_g[l], ln2_b[l], alpha)
        x = x.reshape(B, S, D)
    return x
```

```python
import functools
import math
from typing import NamedTuple

import jax
import jax.numpy as jnp
import numpy as np
from jax import lax
from jax.experimental import pallas as pl
from jax.experimental.pallas import tpu as pltpu

F32 = jnp.float32
BF16 = jnp.bfloat16

D_MODEL = 1024
HEAD_DIM = 64
N_HEADS_A, N_HEADS_B, N_HEADS_C = 4, 6, 6
WIDTH_A, WIDTH_B, WIDTH_C = N_HEADS_A * HEAD_DIM, N_HEADS_B * HEAD_DIM, N_HEADS_C * HEAD_DIM
DIFF_DIM = HEAD_DIM // 2
ROPE_THETA = 500000.0
ROPE_FRACTION = 4
DILATIONS = (1, 4, 16)
assert DILATIONS[0] == 1 and DILATIONS[2] == DILATIONS[1] ** 2
BAND_HALF = 64
GRID_W = 64
NA_ROWS = 8
NA_COLS = 16
N_EXPERTS = 8
TOP_K = 2
LN_EPS = 1e-5
NEG = -1e30
LOG2E = math.log2(math.e)
LN2 = math.log(2.0)

LANES = 128
SUBLANES = 8
VMEM_LIMIT = 56 * 1024 * 1024

PROJ_W = 3 * (WIDTH_A + WIDTH_B + WIDTH_C)
ROPE_COLS = 3 * WIDTH_A + 2 * WIDTH_B
COL_CHUNK = 3 * WIDTH_A
BAND_LO = 3 * WIDTH_A
BAND_COLS = 3 * WIDTH_B


def _cparams(sem):
    return pltpu.CompilerParams(dimension_semantics=sem, vmem_limit_bytes=VMEM_LIMIT)


def _layer_norm(z, g, b):
    mu = jnp.mean(z, axis=-1, keepdims=True)
    zc = z - mu
    var = jnp.mean(zc * zc, axis=-1, keepdims=True)
    return zc * lax.rsqrt(var + LN_EPS) * g + b


def _rope_tables(seq):
    rot_a = DIFF_DIM // ROPE_FRACTION
    rot_b = HEAD_DIM // ROPE_FRACTION
    pos = jnp.arange(seq, dtype=F32)[:, None]
    inv_a = ROPE_THETA ** (-jnp.arange(0, rot_a, 2, dtype=F32) / rot_a)
    inv_b = ROPE_THETA ** (-jnp.arange(0, rot_b, 2, dtype=F32) / rot_b)
    ang_a, ang_b = pos * inv_a[None, :], pos * inv_b[None, :]

    cols = np.arange(ROPE_COLS)
    in_a = cols < 3 * WIDTH_A
    sub_a = cols // WIDTH_A
    within_a = cols % DIFF_DIM
    rot_mask_a = in_a & (sub_a < 2) & (within_a < rot_a)
    cb = cols - 3 * WIDTH_A
    sub_b = cb // WIDTH_B
    within_b = cb % HEAD_DIM
    rot_mask_b = (~in_a) & (within_b < rot_b)

    first_half = np.where(in_a, within_a < rot_a // 2, within_b < rot_b // 2)
    rot = rot_mask_a | rot_mask_b
    scale = np.ones(ROPE_COLS, np.float32)
    scale[in_a & (sub_a == 0)] = DIFF_DIM ** -0.5 * LOG2E
    scale[(~in_a) & (sub_b == 0)] = HEAD_DIM ** -0.5 * LOG2E

    def spread(tab_a, tab_b, fill):
        def groups(tab, width, reps):
            pad = jnp.full((seq, width - 2 * tab.shape[1]), fill, F32)
            return jnp.tile(jnp.concatenate([tab, tab, pad], axis=1), (1, reps))
        return jnp.concatenate([groups(tab_a, DIFF_DIM, 2 * WIDTH_A // DIFF_DIM),
                                jnp.full((seq, WIDTH_A), fill, F32),
                                groups(tab_b, HEAD_DIM, 2 * WIDTH_B // HEAD_DIM)], axis=1)

    cm = spread(jnp.cos(ang_a), jnp.cos(ang_b), 1.0) * jnp.asarray(scale)[None, :]
    sn = spread(jnp.sin(ang_a), jnp.sin(ang_b), 0.0) * jnp.asarray(scale)[None, :]
    ma = np.where(rot & first_half, -1.0, 0.0).astype(np.float32)[None, :]
    mb = np.where(rot & ~first_half, 1.0, 0.0).astype(np.float32)[None, :]
    return cm, sn, jnp.asarray(ma), jnp.asarray(mb)


PROJ_TM = 512
W_CAST_ROWS = 128


def _cast_rows_to_bf16(src_ref, dst_ref, rows):
    def body(i, c):
        r = pl.multiple_of(i * W_CAST_ROWS, W_CAST_ROWS)
        dst_ref[pl.ds(r, W_CAST_ROWS), :] = src_ref[pl.ds(r, W_CAST_ROWS), :].astype(BF16)
        return c
    lax.fori_loop(0, rows // W_CAST_ROWS, body, 0)


def _proj_kernel(x_ref, w_ref, cm_ref, sn_ref, ma_ref, mb_ref, o_ref, *rest):
    res_refs, (wbf_ref, band_ref, split_ref) = rest[:-3], rest[-3:]

    @pl.when((pl.program_id(0) == 0) & (pl.program_id(1) == 0))
    def _():
        _cast_rows_to_bf16(w_ref, wbf_ref, D_MODEL)

    xb = x_ref[0].astype(BF16)
    tm = xb.shape[0]
    n_chunks = PROJ_W // COL_CHUNK
    for c in range(n_chunks):
        lo, hi = c * COL_CHUNK, (c + 1) * COL_CHUNK
        acc = jnp.dot(xb, wbf_ref[:, lo:hi], preferred_element_type=F32)
        if hi <= ROPE_COLS:
            half = (DIFF_DIM if c == 0 else HEAD_DIM) // ROPE_FRACTION // 2
            up = pltpu.roll(acc, COL_CHUNK - half, 1)
            dn = pltpu.roll(acc, half, 1)
            acc = acc * cm_ref[:, lo:hi] + (up * ma_ref[:, lo:hi] + dn * mb_ref[:, lo:hi]) * sn_ref[:, lo:hi]
        elif c == 2:
            lane = lax.broadcasted_iota(jnp.int32, (1, COL_CHUNK), 1)
            acc = acc * jnp.where(lane >= WIDTH_B, HEAD_DIM ** -0.5 * LOG2E, 1.0)
        o_ref[0, :, lo:hi] = acc.astype(BF16)
        for col in range(max(lo, BAND_LO), min(hi, BAND_LO + BAND_COLS), LANES):
            band_ref[(col - BAND_LO) // LANES] = acc[:, col - lo:col - lo + LANES]

    (d1, d2), (r1_ref, r2_ref) = DILATIONS[1:], res_refs
    for page in range(BAND_COLS // LANES):
        cs = slice(page * LANES, (page + 1) * LANES)
        for r in range(d1):
            rows = band_ref[page, pl.ds(r, tm // d1, stride=d1), :]
            r1_ref[0, r, :, cs] = rows.astype(BF16)
            split_ref[page, r] = rows
        for r in range(d2):
            r2_ref[0, r, :, cs] = split_ref[page, r % d1, pl.ds(r // d1, tm // d2, stride=d1), :].astype(BF16)


def _project(x, w, layer, tables):
    B, S, _ = x.shape
    cm, sn, ma, mb = tables
    tm = PROJ_TM
    res_specs = [pl.BlockSpec((1, d, tm // d, BAND_COLS), lambda p, b: (b, 0, p, 0)) for d in DILATIONS[1:]]
    res_shapes = [jax.ShapeDtypeStruct((B, d, S // d, BAND_COLS), BF16) for d in DILATIONS[1:]]
    return pl.pallas_call(
        _proj_kernel,
        grid=(S // tm, B),
        in_specs=[
            pl.BlockSpec((1, tm, D_MODEL), lambda p, b: (b, p, 0)),
            pl.BlockSpec((None, D_MODEL, PROJ_W), lambda p, b: (layer, 0, 0), pipeline_mode=pl.Buffered(1)),
            pl.BlockSpec((tm, ROPE_COLS), lambda p, b: (p, 0)),
            pl.BlockSpec((tm, ROPE_COLS), lambda p, b: (p, 0)),
            pl.BlockSpec((1, ROPE_COLS), lambda p, b: (0, 0)),
            pl.BlockSpec((1, ROPE_COLS), lambda p, b: (0, 0)),
        ],
        out_specs=[pl.BlockSpec((1, tm, PROJ_W), lambda p, b: (b, p, 0))] + res_specs,
        out_shape=[jax.ShapeDtypeStruct((B, S, PROJ_W), BF16)] + res_shapes,
        scratch_shapes=[pltpu.VMEM((D_MODEL, PROJ_W), BF16),
                        pltpu.VMEM((BAND_COLS // LANES, tm, LANES), F32),
                        pltpu.VMEM((BAND_COLS // LANES, DILATIONS[1], tm // DILATIONS[1], LANES), F32)],
        compiler_params=_cparams(("arbitrary", "arbitrary")),
        name="proj_in",
    )(x, w, cm, sn, ma, mb)


ATT_A_TQ = 256
ATT_A_UNROLL = 8


def _nt_dot(a, b):
    return lax.dot_general(a, b, (((1,), (1,)), ((), ())), preferred_element_type=F32)


def _with_ones(v):
    return jnp.concatenate([v, jnp.ones_like(v)], axis=1)


def _softmax_times_v(e, v1):
    nd = jnp.dot(e, v1, preferred_element_type=F32)
    return nd[:, :LANES] * (1.0 / nd[:, LANES:])


def _attn_a_kernel(lam_ref, g_ref, q_ref, k_ref, v_ref, o_ref, *, lam_init, seq):
    lp = lam_ref[...]
    lam = (jnp.exp(jnp.sum(lp[0:1] * lp[1:2], axis=-1, keepdims=True))
           - jnp.exp(jnp.sum(lp[2:3] * lp[3:4], axis=-1, keepdims=True)) + lam_init)
    lane = lax.broadcasted_iota(jnp.int32, (1, LANES), 1)
    first = lane < HEAD_DIM
    gain = g_ref[...] * (1.0 - lam_init)
    tq = ATT_A_TQ

    def qblock(i, carry):
        r0 = pl.multiple_of(i * tq, tq)
        q = q_ref[0, pl.ds(r0, tq), :]
        k = k_ref[0]
        v1 = _with_ones(v_ref[0])
        outs = []
        for hh in range(2):
            maps = []
            for m in range(2):
                lo = hh * HEAD_DIM + m * DIFF_DIM
                qm = jnp.where((lane >= lo) & (lane < lo + DIFF_DIM), q, jnp.zeros_like(q))
                s = _nt_dot(qm, k)
                e = jnp.exp2(s - jnp.max(s, axis=-1, keepdims=True)).astype(BF16)
                maps.append(_softmax_times_v(e, v1))
            outs.append(maps[0] - lam * maps[1])
        o = jnp.where(first, outs[0], outs[1])
        sq = o * o
        ss0 = jnp.sum(jnp.where(first, sq, 0.0), axis=-1, keepdims=True)
        ss1 = jnp.sum(jnp.where(first, 0.0, sq), axis=-1, keepdims=True)
        ms = jnp.where(first, ss0, ss1) * (1.0 / HEAD_DIM)
        o_ref[0, pl.ds(r0, tq), :] = (o * lax.rsqrt(ms + LN_EPS) * gain).astype(BF16)
        return carry

    lax.fori_loop(0, seq // tq, qblock, 0, unroll=ATT_A_UNROLL)


def _attn_a(qkv, diff_lambda, norm_g, lam_init):
    B, S, _ = qkv.shape
    n_pairs = WIDTH_A // LANES
    g2 = jnp.tile(norm_g.astype(F32), LANES // HEAD_DIM)[None, :]
    blk = lambda off: pl.BlockSpec((1, S, LANES), lambda b, h, off=off: (b, 0, off + h))
    return pl.pallas_call(
        functools.partial(_attn_a_kernel, lam_init=lam_init, seq=S),
        grid=(B, n_pairs),
        in_specs=[
            pl.BlockSpec((4, DIFF_DIM), lambda b, h: (0, 0)),
            pl.BlockSpec((1, LANES), lambda b, h: (0, 0)),
            blk(0), blk(n_pairs), blk(2 * n_pairs),
        ],
        out_specs=pl.BlockSpec((1, S, LANES), lambda b, h: (b, 0, h)),
        out_shape=jax.ShapeDtypeStruct((B, S, WIDTH_A), BF16),
        compiler_params=_cparams(("arbitrary", "arbitrary")),
        name="attn_diff",
    )(diff_lambda.astype(F32), g2, qkv, qkv, qkv)


BAND_TQ = 128
BAND_CHAINS = 48


def _stack_pair(q, lane):
    z = jnp.zeros_like(q)
    return jnp.concatenate([jnp.where(lane < HEAD_DIM, q, z), jnp.where(lane < HEAD_DIM, z, q)], axis=0)


def _attn_b_kernel(q_ref, k_ref, v_ref, o_ref, lse_ref, *, length, dil):
    tq = BAND_TQ
    win = min(tq + 2 * BAND_HALF, length)
    lane = lax.broadcasted_iota(jnp.int32, (1, LANES), 1)

    blocks = [(i * tq, min(max(i * tq - BAND_HALF, 0), length - win)) for i in range(length // tq)]
    row = lax.broadcasted_iota(jnp.int32, (2 * tq, win), 0)
    delta = lax.broadcasted_iota(jnp.int32, (2 * tq, win), 1) - jnp.where(row >= tq, row - tq, row)
    masks = {off: jnp.where(jnp.abs(delta - off) <= BAND_HALF, 0.0, NEG) for off in {q0 - k0 for q0, k0 in blocks}}

    def residue(r, carry):
        for q0, k0 in blocks:
            lse_tile = jnp.zeros((tq, LANES), F32)
            for pr in range(WIDTH_B // LANES):
                cs = slice(pr * LANES, (pr + 1) * LANES)
                q2 = _stack_pair(q_ref[0, r, q0:q0 + tq, cs], lane)
                s = _nt_dot(q2, k_ref[0, r, k0:k0 + win, cs]) + masks[q0 - k0]
                mx = jnp.max(s, axis=-1, keepdims=True)
                e = jnp.exp2(s - mx).astype(BF16)
                nd = jnp.dot(e, _with_ones(v_ref[0, r, k0:k0 + win, cs]), preferred_element_type=F32)
                l = nd[:, LANES:]
                o2 = nd[:, :LANES] * (1.0 / l)
                o_ref[0, r, q0:q0 + tq, cs] = jnp.where(lane < HEAD_DIM, o2[:tq], o2[tq:]).astype(BF16)
                lse2 = (mx + jnp.log2(l)) * LN2
                lse_tile = jnp.where(lane == 2 * pr, lse2[:tq], lse_tile)
                lse_tile = jnp.where(lane == 2 * pr + 1, lse2[tq:], lse_tile)
            lse_ref[0, r, q0:q0 + tq, :] = lse_tile
        return carry

    chains = len(blocks) * (WIDTH_B // LANES)
    lax.fori_loop(0, dil, residue, 0, unroll=min(dil, max(1, BAND_CHAINS // chains)))


def _attn_b(src, dil, first_blk):
    B, _, L, _ = src.shape
    blk = lambda off: pl.BlockSpec((1, dil, L, WIDTH_B), lambda b, off=off: (b, 0, 0, first_blk + off))
    return pl.pallas_call(
        functools.partial(_attn_b_kernel, length=L, dil=dil),
        grid=(B,),
        in_specs=[blk(0), blk(1), blk(2)],
        out_specs=[pl.BlockSpec((1, dil, L, WIDTH_B), lambda b: (b, 0, 0, 0)),
                   pl.BlockSpec((1, dil, L, LANES), lambda b: (b, 0, 0, 0))],
        out_shape=[jax.ShapeDtypeStruct((B, dil, L, WIDTH_B), BF16),
                   jax.ShapeDtypeStruct((B, dil, L, LANES), F32)],
        compiler_params=_cparams(("arbitrary",)),
        name=f"attn_band_d{dil}",
    )(src, src, src)


N_RPB_ROWS = 2 * NA_ROWS - 1
N_RPB_COLS = 2 * NA_COLS - 1
NA_KEYS = NA_ROWS * GRID_W
N_BIAS_VARIANTS = NA_ROWS
NA_ROWS_PER_STEP = 32


def _attn_c_kernel(u_ref, q_ref, k_ref, v_ref, o_ref, bias_ref, *, rows):
    lane = lax.broadcasted_iota(jnp.int32, (1, LANES), 1)
    n_pairs = WIDTH_C // LANES

    @pl.when(pl.program_id(0) == 0)
    def _():
        wide = N_RPB_ROWS * GRID_W
        c = lax.broadcasted_iota(jnp.int32, (GRID_W, wide), 0)
        kc = lax.broadcasted_iota(jnp.int32, (GRID_W, wide), 1) % GRID_W
        tidx = jnp.clip(kc - c + NA_COLS - 1, 0, N_RPB_COLS - 1)
        c5 = lax.broadcasted_iota(jnp.int32, (GRID_W, NA_KEYS), 0)
        kc5 = lax.broadcasted_iota(jnp.int32, (GRID_W, NA_KEYS), 1) % GRID_W
        cstart = jnp.clip(c5 - NA_COLS // 2, 0, GRID_W - NA_COLS)
        valid = (kc5 >= cstart) & (kc5 < cstart + NA_COLS)
        for h in range(N_HEADS_C):
            def tbody(t, acc, h=h):
                row = u_ref[h, pl.ds(t, 1), :]
                return jnp.where(tidx == t, row, acc)
            toep = lax.fori_loop(0, N_RPB_COLS, tbody, jnp.zeros((GRID_W, wide), F32))
            for var in range(N_BIAS_VARIANTS):
                tile = jnp.where(valid, toep[:, var * GRID_W:var * GRID_W + NA_KEYS] * LOG2E, NEG)
                bias_ref[var, h // 2, (h % 2) * GRID_W:(h % 2 + 1) * GRID_W, :] = tile

    def rbody(step, carry):
        for sub in range(NA_ROWS_PER_STEP):
            r = step * NA_ROWS_PER_STEP + sub
            rs = jnp.clip(r - NA_ROWS // 2, 0, rows - NA_ROWS)
            var = rs - r + NA_ROWS - 1
            q0 = pl.multiple_of(r * GRID_W, GRID_W)
            k0 = pl.multiple_of(rs * GRID_W, GRID_W)
            for pr in range(n_pairs):
                cs = slice(pr * LANES, (pr + 1) * LANES)
                q2 = _stack_pair(q_ref[0, pl.ds(q0, GRID_W), cs], lane)
                s = _nt_dot(q2, k_ref[0, pl.ds(k0, NA_KEYS), cs]) + bias_ref[var, pr]
                e = jnp.exp2(s - jnp.max(s, axis=-1, keepdims=True)).astype(BF16)
                o2 = _softmax_times_v(e, _with_ones(v_ref[0, pl.ds(k0, NA_KEYS), cs]))
                o_ref[0, pl.ds(q0, GRID_W), cs] = jnp.where(lane < HEAD_DIM, o2[:GRID_W], o2[GRID_W:]).astype(BF16)
        return carry

    lax.fori_loop(0, rows // NA_ROWS_PER_STEP, rbody, 0)


def _attn_c(qkv, rpb):
    B, S, _ = qkv.shape
    rows = S // GRID_W
    assert rows >= NA_ROWS
    u = jnp.repeat(jnp.transpose(rpb.astype(F32), (0, 2, 1)), GRID_W, axis=2)
    q_blk = (3 * WIDTH_A + 3 * WIDTH_B) // WIDTH_C
    blk = lambda off: pl.BlockSpec((1, S, WIDTH_C), lambda b, off=off: (b, 0, off))
    return pl.pallas_call(
        functools.partial(_attn_c_kernel, rows=rows),
        grid=(B,),
        in_specs=[pl.BlockSpec((N_HEADS_C, N_RPB_COLS, N_RPB_ROWS * GRID_W), lambda b: (0, 0, 0)),
                  blk(q_blk), blk(q_blk + 1), blk(q_blk + 2)],
        out_specs=pl.BlockSpec((1, S, WIDTH_C), lambda b: (b, 0, 0)),
        out_shape=jax.ShapeDtypeStruct((B, S, WIDTH_C), BF16),
        scratch_shapes=[pltpu.VMEM((N_BIAS_VARIANTS, WIDTH_C // LANES, LANES, NA_KEYS), F32)],
        compiler_params=_cparams(("arbitrary",)),
        name="attn_nbr",
    )(u, qkv, qkv, qkv)


OUT_TM = 512
OUT_SPLIT = 2


def _split_bf16(x):
    hi = x.astype(BF16)
    return hi, (x - hi.astype(F32)).astype(BF16)


def _outproj_kernel(x_ref, oa_ref, ob0_ref, ob1_ref, ob2_ref, l0_ref, l1_ref, l2_ref, oc_ref,
                    beta_ref, w_ref, g_ref, b_ref, e_ref, *rest, alpha, route):
    if route:
        wr_ref, o_ref, meta_ref, fields_ref, cnt_ref, wbf_ref, il_ref, carry_ref = rest
    else:
        o_ref, wbf_ref, il_ref = rest

    @pl.when(pl.program_id(0) == 0)
    def _():
        _cast_rows_to_bf16(w_ref, wbf_ref, D_MODEL)
        if route:
            carry_ref[...] = jnp.zeros_like(carry_ref)

    tm = x_ref.shape[0]
    n_pages = WIDTH_B // LANES
    for slot, (dil, ob_r, l_r) in enumerate(zip(DILATIONS[1:], (ob1_ref, ob2_ref), (l1_ref, l2_ref))):
        for r in range(dil):
            rows = pl.ds(r, tm // dil, stride=dil)
            for page in range(n_pages):
                il_ref[slot, page, rows, :] = ob_r[0, r, :, page * LANES:(page + 1) * LANES].astype(F32)
            il_ref[slot, n_pages, rows, :] = l_r[0, r]

    beta = beta_ref[...]
    half = tm // OUT_SPLIT
    for part in range(OUT_SPLIT):
        rs = slice(part * half, (part + 1) * half)
        l0, l1, l2 = l0_ref[0, 0, rs, :], il_ref[0, n_pages, rs, :], il_ref[1, n_pages, rs, :]
        lm = jnp.maximum(jnp.maximum(l0, l1), l2)
        e0, e1, e2 = jnp.exp(l0 - lm), jnp.exp(l1 - lm), jnp.exp(l2 - lm)
        inv = 1.0 / (e0 + e1 + e2)
        wexp = [jnp.dot(jnp.concatenate(_split_bf16(e * inv), axis=1), e_ref[...], preferred_element_type=F32)
                for e in (e0, e1, e2)]
        pages = []
        for page in range(n_pages):
            cs = slice(page * LANES, (page + 1) * LANES)
            branch_vals = (ob0_ref[0, 0, rs, cs].astype(F32), il_ref[0, page, rs, :], il_ref[1, page, rs, :])
            pages.append(sum(w[:, cs] * val for w, val in zip(wexp, branch_vals)))
        ob = jnp.concatenate(pages, axis=1)

        ma = (oa_ref[rs, :].astype(F32) * beta[:, :WIDTH_A]).astype(BF16)
        mb = (ob * beta[:, WIDTH_A:WIDTH_A + WIDTH_B]).astype(BF16)
        mc = (oc_ref[rs, :].astype(F32) * beta[:, WIDTH_A + WIDTH_B:]).astype(BF16)
        acc = jnp.dot(jnp.concatenate([ma, mb, mc], axis=1), wbf_ref[...], preferred_element_type=F32)
        y = _layer_norm(alpha * x_ref[rs, :] + acc, g_ref[...], b_ref[...])
        if route:
            _store_token_tiles(o_ref, y, row0=part * half)
            fields = _route_rows(y, wr_ref, carry_ref)
            fields_ref[:, rs] = fields
            meta_ref[rs, :] = jnp.concatenate([fields, jnp.zeros((LANES - SUBLANES, half), F32)], axis=0).T
        else:
            o_ref[rs, :] = y
    if route:
        cnt_ref[...] = carry_ref[...]


def _store_token_tiles(ref, y, row0=0):
    rows = y.shape[0]
    for j in range(D_MODEL // LANES):
        ref[pl.ds(row0 * SUBLANES + j, rows, stride=SUBLANES), :] = y[:, j * LANES:(j + 1) * LANES]


def _load_token_tiles(ref, rows, row0=0):
    return jnp.concatenate(
        [ref[pl.ds(row0 * SUBLANES + j, rows, stride=SUBLANES), :] for j in range(D_MODEL // LANES)], axis=1)


def _outproj(x2, oa, obs, lses, oc, beta, w, layer, g, b, alpha, w_router=None):
    route = w_router is not None
    N = x2.shape[0]
    tm = OUT_TM
    steps_per_seq = obs[0].shape[2] // tm
    expand = np.zeros((2 * LANES, WIDTH_B), np.float32)
    for h in range(N_HEADS_B):
        expand[h, h * HEAD_DIM:(h + 1) * HEAD_DIM] = 1.0
        expand[LANES + h, h * HEAD_DIM:(h + 1) * HEAD_DIM] = 1.0
    row = lambda w_: pl.BlockSpec((tm, w_), lambda i: (i, 0))
    full = lambda r, c, **kw: pl.BlockSpec((r, c), lambda i: (0, 0), **kw)
    res = lambda d, w_: pl.BlockSpec((1, d, tm // d, w_),
                                     lambda i: (i // steps_per_seq, 0, i % steps_per_seq, 0))
    in_specs = ([row(D_MODEL), row(WIDTH_A)] + [res(d, WIDTH_B) for d in DILATIONS]
                + [res(d, LANES) for d in DILATIONS] + [row(WIDTH_C), full(1, D_MODEL),
                   pl.BlockSpec((None, D_MODEL, D_MODEL), lambda i: (layer, 0, 0), pipeline_mode=pl.Buffered(1)),
                   full(1, D_MODEL), full(1, D_MODEL), full(2 * LANES, WIDTH_B)])
    args = [x2, oa, obs[0], obs[1], obs[2], lses[0], lses[1], lses[2], oc,
            beta[None, :].astype(F32), w, g[None, :].astype(F32), b[None, :].astype(F32), jnp.asarray(expand, BF16)]
    scratch = [pltpu.VMEM((D_MODEL, D_MODEL), BF16),
               pltpu.VMEM((len(DILATIONS) - 1, WIDTH_B // LANES + 1, tm, LANES), F32)]
    if route:
        wr = jnp.zeros((D_MODEL, LANES), F32).at[:, :N_EXPERTS].set(w_router.astype(F32))
        wr_hi = wr.astype(BF16)
        in_specs.append(full(D_MODEL, 2 * LANES))
        args.append(jnp.concatenate([wr_hi, (wr - wr_hi.astype(F32)).astype(BF16)], axis=1))
        out_specs = [pl.BlockSpec((tm * SUBLANES, LANES), lambda i: (i, 0)), row(LANES),
                     pl.BlockSpec((SUBLANES, tm), lambda i: (0, i)), full(ROUTE_ROWS, LANES)]
        out_shape = [jax.ShapeDtypeStruct((N * SUBLANES, LANES), F32), jax.ShapeDtypeStruct((N, LANES), F32),
                     jax.ShapeDtypeStruct((SUBLANES, N), F32), jax.ShapeDtypeStruct((ROUTE_ROWS, LANES), F32)]
        scratch.append(pltpu.VMEM((ROUTE_ROWS, LANES), F32))
    else:
        out_specs = row(D_MODEL)
        out_shape = jax.ShapeDtypeStruct((N, D_MODEL), F32)
    return pl.pallas_call(
        functools.partial(_outproj_kernel, alpha=alpha, route=route),
        grid=(N // tm,),
        in_specs=in_specs,
        out_specs=out_specs,
        out_shape=out_shape,
        scratch_shapes=scratch,
        compiler_params=_cparams(("arbitrary",)),
        name="merge_outproj_ln_route" if route else "merge_outproj_ln",
    )(*args)


CAST_ROWS = 256
DENSE_TM = 512
DENSE_TF = 256


def _load_as_bf16(w_hbm, dst_ref, stage_ref, sem):
    n = w_hbm.shape[0] // CAST_ROWS

    def chunk(c):
        return pltpu.make_async_copy(w_hbm.at[pl.ds(c * CAST_ROWS, CAST_ROWS), :], stage_ref.at[c % 2], sem.at[c % 2])

    chunk(0).start()
    for c in range(n):
        if c + 1 < n:
            chunk(c + 1).start()
        chunk(c).wait()
        dst_ref[c * CAST_ROWS:(c + 1) * CAST_ROWS, :] = stage_ref[c % 2].astype(BF16)


def _dense_ffn_kernel(x_ref, wg_hbm, wu_hbm, wd_hbm, g_ref, b_ref, o_ref,
                      wg_ref, wu_ref, wd_ref, h_ref, wide_stage, narrow_stage, sem, *, alpha):
    @pl.when(pl.program_id(0) == 0)
    def _():
        _load_as_bf16(wg_hbm, wg_ref, wide_stage, sem)
        _load_as_bf16(wu_hbm, wu_ref, wide_stage, sem)
        _load_as_bf16(wd_hbm, wd_ref, narrow_stage, sem)

    x = x_ref[...]
    xb = x.astype(BF16)
    for c in range(h_ref.shape[1] // DENSE_TF):
        cs = slice(c * DENSE_TF, (c + 1) * DENSE_TF)
        gate = jnp.dot(xb, wg_ref[:, cs], preferred_element_type=F32)
        up = jnp.dot(xb, wu_ref[:, cs], preferred_element_type=F32)
        h_ref[:, cs] = (gate * jax.nn.sigmoid(gate) * up).astype(BF16)
    y = jnp.dot(h_ref[...], wd_ref[...], preferred_element_type=F32)
    o_ref[...] = _layer_norm(alpha * x + y, g_ref[...], b_ref[...])


def _dense_ffn(x, w_gate, w_up, w_down, g, b, alpha):
    n_tok = x.shape[0]
    ff = w_gate.shape[1]
    tm = DENSE_TM
    row = pl.BlockSpec((tm, D_MODEL), lambda i: (i, 0))
    hbm = pl.BlockSpec(memory_space=pl.ANY)
    return pl.pallas_call(
        functools.partial(_dense_ffn_kernel, alpha=alpha),
        grid=(n_tok // tm,),
        in_specs=[row, hbm, hbm, hbm,
                  pl.BlockSpec((1, D_MODEL), lambda i: (0, 0)), pl.BlockSpec((1, D_MODEL), lambda i: (0, 0))],
        out_specs=row,
        out_shape=jax.ShapeDtypeStruct((n_tok, D_MODEL), F32),
        scratch_shapes=[pltpu.VMEM((D_MODEL, ff), BF16), pltpu.VMEM((D_MODEL, ff), BF16),
                        pltpu.VMEM((ff, D_MODEL), BF16), pltpu.VMEM((tm, ff), BF16),
                        pltpu.VMEM((2, CAST_ROWS, ff), F32), pltpu.VMEM((2, CAST_ROWS, D_MODEL), F32),
                        pltpu.SemaphoreType.DMA((2,))],
        compiler_params=_cparams(("arbitrary",)),
        name="swiglu_dense",
    )(x, w_gate.astype(F32), w_up.astype(F32), w_down.astype(F32), g[None, :].astype(F32), b[None, :].astype(F32))


FFN_TM = 1024
FFN_TF = 512
FFN_SUB = 256


def _ffn_kernel(te_ref, nt_ref, rows_ref, x_ref, wg_ref, wu_ref, wd_ref, o_ref, xb_ref, acc_ref, *, n_chunks):
    i, j = pl.program_id(0), pl.program_id(1)
    tm = xb_ref.shape[0]

    def swiglu_rows(rs):
        xb = xb_ref[rs, :]
        gate = jnp.dot(xb, wg_ref[...].astype(BF16), preferred_element_type=F32)
        up = jnp.dot(xb, wu_ref[...].astype(BF16), preferred_element_type=F32)
        h = (gate * jax.nn.sigmoid(gate) * up).astype(BF16)
        acc_ref[rs, :] += jnp.dot(h, wd_ref[...].astype(BF16), preferred_element_type=F32)

    @pl.when(i < nt_ref[0])
    def _():
        @pl.when(j == 0)
        def _():
            xb_ref[...] = _load_token_tiles(x_ref, tm).astype(BF16)
            acc_ref[...] = jnp.zeros_like(acc_ref)

        sparse_tile = rows_ref[i] <= tm - FFN_SUB

        @pl.when(jnp.logical_not(sparse_tile))
        def _():
            swiglu_rows(slice(None))

        for sb in range(tm // FFN_SUB - 1):
            @pl.when(sparse_tile & (sb * FFN_SUB < rows_ref[i]))
            def _(sb=sb):
                swiglu_rows(slice(sb * FFN_SUB, (sb + 1) * FFN_SUB))

        @pl.when(j == n_chunks - 1)
        def _():
            _store_token_tiles(o_ref, acc_ref[...])

    @pl.when((i >= nt_ref[0]) & (j == n_chunks - 1))
    def _():
        o_ref[...] = jnp.zeros_like(o_ref)


def _grouped_ffn(x, w_gate, w_up, w_down, tile_expert, n_tiles, tile_rows):
    tm, tf = FFN_TM, FFN_TF
    n_chunks = w_gate.shape[-1] // tf
    max_tiles = tile_expert.shape[0]

    def tile_ix(i, nt):
        return jnp.minimum(i, nt[0] - 1)

    def chunk_ix(i, j, nt):
        return jnp.where(i < nt[0], j, n_chunks - 1)

    blk = (tm * SUBLANES, LANES)
    in_specs = [
        pl.BlockSpec(blk, lambda i, j, te, nt, tr: (tile_ix(i, nt), 0)),
        pl.BlockSpec((None, D_MODEL, tf), lambda i, j, te, nt, tr: (te[tile_ix(i, nt)], 0, chunk_ix(i, j, nt))),
        pl.BlockSpec((None, D_MODEL, tf), lambda i, j, te, nt, tr: (te[tile_ix(i, nt)], 0, chunk_ix(i, j, nt))),
        pl.BlockSpec((None, tf, D_MODEL), lambda i, j, te, nt, tr: (te[tile_ix(i, nt)], chunk_ix(i, j, nt), 0)),
    ]
    return pl.pallas_call(
        functools.partial(_ffn_kernel, n_chunks=n_chunks),
        grid_spec=pltpu.PrefetchScalarGridSpec(
            num_scalar_prefetch=3,
            grid=(max_tiles, n_chunks),
            in_specs=in_specs,
            out_specs=pl.BlockSpec(blk, lambda i, j, te, nt, tr: (i, 0)),
            scratch_shapes=[pltpu.VMEM((tm, D_MODEL), BF16), pltpu.VMEM((tm, D_MODEL), F32)],
        ),
        out_shape=jax.ShapeDtypeStruct(x.shape, F32),
        compiler_params=_cparams(("arbitrary", "arbitrary")),
        name="swiglu_grouped",
    )(tile_expert, n_tiles, tile_rows, x, w_gate, w_up, w_down)


META_E1, META_E2, META_R1, META_R2, META_G1, META_G2 = range(6)


ROUTE_ROWS = 16


def _route_rows(x, wr_ref, carry_ref):
    tb = x.shape[0]
    xh, xl = _split_bf16(x)
    hi_lo = jnp.dot(xh, wr_ref[...], preferred_element_type=F32)
    logits = hi_lo[:, :LANES] + hi_lo[:, LANES:] + jnp.dot(xl, wr_ref[:, :LANES], preferred_element_type=F32)
    logits = logits.T[:ROUTE_ROWS]
    row = lax.broadcasted_iota(jnp.int32, (ROUTE_ROWS, tb), 0).astype(F32)
    logits = jnp.where(row < N_EXPERTS, logits, -jnp.inf)
    m1 = jnp.max(logits, axis=0, keepdims=True)
    i1 = jnp.min(jnp.where(logits == m1, row, float(ROUTE_ROWS)), axis=0, keepdims=True)
    rest = jnp.where(row == i1, -jnp.inf, logits)
    m2 = jnp.max(rest, axis=0, keepdims=True)
    i2 = jnp.min(jnp.where(rest == m2, row, float(ROUTE_ROWS)), axis=0, keepdims=True)
    e21 = jnp.exp(m2 - m1)
    g1 = 1.0 / (1.0 + e21)
    g2 = e21 * g1

    member = ((row == i1) | (row == i2)).astype(F32)
    r_i = lax.broadcasted_iota(jnp.int32, (tb, tb), 0)
    c_i = lax.broadcasted_iota(jnp.int32, (tb, tb), 1)
    earlier = (r_i < c_i).astype(BF16)
    before = jnp.dot(member.astype(BF16), earlier, preferred_element_type=F32)
    rank = carry_ref[:, 0:1] + before
    r1 = jnp.sum(jnp.where(row == i1, rank, 0.0), axis=0, keepdims=True)
    r2 = jnp.sum(jnp.where(row == i2, rank, 0.0), axis=0, keepdims=True)
    carry_ref[...] += jnp.sum(member, axis=1, keepdims=True)

    fields = [None] * SUBLANES
    for slot, val in ((META_E1, i1), (META_E2, i2), (META_R1, r1), (META_R2, r2), (META_G1, g1), (META_G2, g2)):
        fields[slot] = val
    return jnp.concatenate([f if f is not None else jnp.zeros((1, tb), F32) for f in fields], axis=0)


MOVE_TB = 1024
PAD_CHUNK = 64
DMA_ISSUE_UNROLL = 8


def _tile_rows(ref, idx):
    return ref.at[pl.ds(pl.multiple_of(idx * SUBLANES, SUBLANES), SUBLANES), :]


def _zero_fill(pad_ref, xs_hbm, zbuf, zsem):
    zbuf[...] = jnp.zeros_like(zbuf)
    one = _tile_rows(zbuf, 0)
    for wait in (False, True):
        for seg in range(pad_ref.shape[1]):
            start, n = pad_ref[0, seg], pad_ref[1, seg]
            n_single = n % PAD_CHUNK

            def single(t, c):
                cp = pltpu.make_async_copy(one, _tile_rows(xs_hbm, start + t), zsem)
                cp.wait() if wait else cp.start()
                return c

            def bulk(t, c):
                r0 = pl.multiple_of((start + n_single + t * PAD_CHUNK) * SUBLANES, SUBLANES)
                cp = pltpu.make_async_copy(zbuf, xs_hbm.at[pl.ds(r0, PAD_CHUNK * SUBLANES), :], zsem)
                cp.wait() if wait else cp.start()
                return c

            lax.fori_loop(0, n_single, single, 0)
            lax.fori_loop(0, n // PAD_CHUNK, bulk, 0)


def _dispatch_kernel(dest_ref, pad_ref, x_ref, xs_hbm, zbuf, sem, zsem):
    tb = MOVE_TB

    @pl.when(pl.program_id(0) == 0)
    def _():
        _zero_fill(pad_ref, xs_hbm, zbuf, zsem)

    def body(t, c):
        src = _tile_rows(x_ref, t)
        for k in range(TOP_K):
            pltpu.make_async_copy(src, _tile_rows(xs_hbm, dest_ref[0, 0, TOP_K * t + k]), sem).start(priority=k)
        return c

    lax.fori_loop(0, tb, body, 0, unroll=DMA_ISSUE_UNROLL)
    n_rows = TOP_K * tb * SUBLANES
    pltpu.make_async_copy(xs_hbm.at[pl.ds(0, n_rows), :], xs_hbm.at[pl.ds(0, n_rows), :], sem).wait()


def _dispatch(xt, dest, pads, n_slots):
    n_tok = xt.shape[0] // SUBLANES
    tb = MOVE_TB
    dest3 = dest.reshape(n_tok // tb, 1, TOP_K * tb)
    return pl.pallas_call(
        _dispatch_kernel,
        grid=(n_tok // tb,),
        in_specs=[pl.BlockSpec((1, 1, TOP_K * tb), lambda i: (i, 0, 0), memory_space=pltpu.SMEM),
                  pl.BlockSpec(memory_space=pltpu.SMEM),
                  pl.BlockSpec((tb * SUBLANES, LANES), lambda i: (i, 0))],
        out_specs=pl.BlockSpec(memory_space=pl.ANY),
        out_shape=jax.ShapeDtypeStruct((n_slots * SUBLANES, LANES), F32),
        scratch_shapes=[pltpu.VMEM((PAD_CHUNK * SUBLANES, LANES), F32),
                        pltpu.SemaphoreType.DMA(()), pltpu.SemaphoreType.DMA(())],
        compiler_params=_cparams(("arbitrary",)),
        name="moe_dispatch",
    )(dest3, pads, xt)


COMB_TB = 256


def _combine_kernel(dest_ref, next_ref, y_hbm, x_ref, meta_ref, g_ref, b_ref, o_ref, ybuf, sem, *, alpha, n_steps):
    tb = COMB_TB
    i = pl.program_id(0)
    slot = i % 2

    def gather(d_ref, step, s):
        base = (step % (MOVE_TB // tb)) * (TOP_K * tb)

        def body(t, c):
            for k in range(TOP_K):
                pltpu.make_async_copy(_tile_rows(y_hbm, d_ref[0, 0, base + TOP_K * t + k]),
                                      _tile_rows(ybuf.at[s], k * tb + t), sem.at[s]).start(priority=k)
            return c
        lax.fori_loop(0, tb, body, 0, unroll=DMA_ISSUE_UNROLL)

    @pl.when(i == 0)
    def _():
        gather(dest_ref, i, 0)

    @pl.when(i + 1 < n_steps)
    def _():
        gather(next_ref, i + 1, 1 - slot)

    cur = ybuf.at[slot]
    pltpu.make_async_copy(cur, cur, sem.at[slot]).wait()

    meta = meta_ref[...]
    lane = lax.broadcasted_iota(jnp.int32, meta.shape, 1)
    g1 = jnp.sum(jnp.where(lane == META_G1, meta, 0.0), axis=-1, keepdims=True)
    g2 = jnp.sum(jnp.where(lane == META_G2, meta, 0.0), axis=-1, keepdims=True)
    x = _load_token_tiles(x_ref, tb)
    f = g1 * _load_token_tiles(cur, tb) + g2 * _load_token_tiles(cur, tb, row0=tb)
    o_ref[...] = _layer_norm(alpha * x + f, g_ref[...], b_ref[...])


def _combine(yt, xt, dest, meta, g, b, alpha):
    n_tok = xt.shape[0] // SUBLANES
    tb = COMB_TB
    n_steps = n_tok // tb
    per_blk = MOVE_TB // tb
    dest3 = dest.reshape(n_tok // MOVE_TB, 1, TOP_K * MOVE_TB)
    return pl.pallas_call(
        functools.partial(_combine_kernel, alpha=alpha, n_steps=n_steps),
        grid=(n_steps,),
        in_specs=[pl.BlockSpec((1, 1, TOP_K * MOVE_TB), lambda i: (i // per_blk, 0, 0), memory_space=pltpu.SMEM),
                  pl.BlockSpec((1, 1, TOP_K * MOVE_TB),
                               lambda i: (jnp.minimum(i + 1, n_steps - 1) // per_blk, 0, 0),
                               memory_space=pltpu.SMEM),
                  pl.BlockSpec(memory_space=pl.ANY),
                  pl.BlockSpec((tb * SUBLANES, LANES), lambda i: (i, 0)),
                  pl.BlockSpec((tb, LANES), lambda i: (i, 0)),
                  pl.BlockSpec((1, D_MODEL), lambda i: (0, 0)),
                  pl.BlockSpec((1, D_MODEL), lambda i: (0, 0))],
        out_specs=pl.BlockSpec((tb, D_MODEL), lambda i: (i, 0)),
        out_shape=jax.ShapeDtypeStruct((n_tok, D_MODEL), F32),
        scratch_shapes=[pltpu.VMEM((2, TOP_K * tb * SUBLANES, LANES), F32), pltpu.SemaphoreType.DMA((2,))],
        compiler_params=_cparams(("arbitrary",)),
        name="moe_combine_ln",
    )(dest3, dest3, yt, xt, meta, g[None, :].astype(F32), b[None, :].astype(F32))


class MoePlan(NamedTuple):
    dest: jax.Array
    tile_expert: jax.Array
    tile_rows: jax.Array
    n_tiles: jax.Array
    pads: jax.Array
    n_slots: int


def _moe_plan(fields, counts, n_tok):
    tm = FFN_TM
    max_tiles = n_tok * TOP_K // tm + N_EXPERTS
    cnt = counts[:N_EXPERTS, 0].astype(jnp.int32)
    tiles = (cnt + tm - 1) // tm
    tile_end = jnp.cumsum(tiles)
    start = (tile_end - tiles) * tm
    e = fields[META_E1:META_E2 + 1].astype(jnp.int32)
    r = fields[META_R1:META_R2 + 1].astype(jnp.int32)
    experts = jnp.arange(N_EXPERTS, dtype=jnp.int32)[:, None, None]
    group_start = jnp.sum(jnp.where(e[None] == experts, start[:, None, None], 0), axis=0)
    dest = (group_start + r).T.reshape(-1)
    tile_ids = jnp.arange(max_tiles, dtype=jnp.int32)
    tile_expert = jnp.minimum(jnp.sum((tile_ids[:, None] >= tile_end[None, :]).astype(jnp.int32), axis=1),
                              N_EXPERTS - 1)
    first_tile = (tile_end - tiles)[tile_expert]
    tile_rows = jnp.clip(cnt[tile_expert] - (tile_ids - first_tile) * tm, 0, tm).astype(jnp.int32)
    n_tiles = tile_end[-1:]
    pads = jnp.stack([jnp.concatenate([start + cnt, n_tiles * tm]),
                      jnp.concatenate([tiles * tm - cnt, (max_tiles - n_tiles) * tm])]).astype(jnp.int32)
    return MoePlan(dest, tile_expert, tile_rows, n_tiles.astype(jnp.int32), pads, max_tiles * tm)


def kernel(x, w_in, w_out, beta_mix, diff_lambda, diff_norm_g, na_rpb, ln1_g, ln1_b, ln2_g, ln2_b,
           ffn_w_gate, ffn_w_up, ffn_w_down, moe_w_router, moe_w_gate, moe_w_up, moe_w_down):
    B, S, D = x.shape
    assert D == D_MODEL and S % (max(DILATIONS) * BAND_TQ) == 0 and S % PROJ_TM == 0
    depth = w_in.shape[0]
    alpha = (2 * depth) ** 0.25
    n_tok = B * S
    tables = _rope_tables(S)
    x = x.astype(F32)
    for l in range(depth):
        lam_init = 0.8 - 0.6 * math.exp(-0.3 * l)
        moe = l % 2 == 1
        qkv, *res_major = _project(x, w_in, l, tables)
        oa = _attn_a(qkv, diff_lambda[l], diff_norm_g[l], lam_init).reshape(n_tok, WIDTH_A)
        band = [_attn_b(qkv[:, None], DILATIONS[0], BAND_LO // WIDTH_B)]
        band += [_attn_b(src, d, 0) for d, src in zip(DILATIONS[1:], res_major)]
        oc = _attn_c(qkv, na_rpb[l]).reshape(n_tok, WIDTH_C)
        j = l // 2
        mixed = _outproj(x.reshape(n_tok, D), oa, [o for o, _ in band], [s for _, s in band], oc,
                         beta_mix[l], w_out, l, ln1_g[l], ln1_b[l], alpha,
                         w_router=moe_w_router[j] if moe else None)
        if not moe:
            x = _dense_ffn(mixed, ffn_w_gate[j], ffn_w_up[j], ffn_w_down[j], ln2_g[l], ln2_b[l], alpha)
        else:
            x1, meta, fields, counts = mixed
            plan = _moe_plan(fields, counts, n_tok)
            xs = _dispatch(x1, plan.dest, plan.pads, plan.n_slots)
            ys = _grouped_ffn(xs, moe_w_gate[j], moe_w_up[j], moe_w_down[j],
                              plan.tile_expert, plan.n_tiles, plan.tile_rows)
            x = _combine(ys, x1, plan.dest, meta, ln2_g[l], ln2_b[l], alpha)
        x = x.reshape(B, S, D)
    return x
```

```python
import functools
import math
from typing import NamedTuple

import jax
import jax.numpy as jnp
import numpy as np
from jax import lax
from jax.experimental import pallas as pl
from jax.experimental.pallas import tpu as pltpu

F32 = jnp.float32
BF16 = jnp.bfloat16

D_MODEL = 1024
HEAD_DIM = 64
N_HEADS_A, N_HEADS_B, N_HEADS_C = 4, 6, 6
WIDTH_A, WIDTH_B, WIDTH_C = N_HEADS_A * HEAD_DIM, N_HEADS_B * HEAD_DIM, N_HEADS_C * HEAD_DIM
DIFF_DIM = HEAD_DIM // 2
ROPE_THETA = 500000.0
ROPE_FRACTION = 4
DILATIONS = (1, 4, 16)
assert DILATIONS[0] == 1 and DILATIONS[2] == DILATIONS[1] ** 2
BAND_HALF = 64
GRID_W = 64
NA_ROWS = 8
NA_COLS = 16
N_EXPERTS = 8
TOP_K = 2
LN_EPS = 1e-5
NEG = -1e30
LOG2E = math.log2(math.e)
LN2 = math.log(2.0)

LANES = 128
SUBLANES = 8
VMEM_LIMIT = 56 * 1024 * 1024

PROJ_W = 3 * (WIDTH_A + WIDTH_B + WIDTH_C)
ROPE_COLS = 3 * WIDTH_A + 2 * WIDTH_B
COL_CHUNK = 3 * WIDTH_A
BAND_LO = 3 * WIDTH_A
BAND_COLS = 3 * WIDTH_B


def _cparams(sem):
    return pltpu.CompilerParams(dimension_semantics=sem, vmem_limit_bytes=VMEM_LIMIT)


def _layer_norm(z, g, b):
    mu = jnp.mean(z, axis=-1, keepdims=True)
    zc = z - mu
    var = jnp.mean(zc * zc, axis=-1, keepdims=True)
    return zc * lax.rsqrt(var + LN_EPS) * g + b


def _rope_tables(seq):
    rot_a = DIFF_DIM // ROPE_FRACTION
    rot_b = HEAD_DIM // ROPE_FRACTION
    pos = jnp.arange(seq, dtype=F32)[:, None]
    inv_a = ROPE_THETA ** (-jnp.arange(0, rot_a, 2, dtype=F32) / rot_a)
    inv_b = ROPE_THETA ** (-jnp.arange(0, rot_b, 2, dtype=F32) / rot_b)
    ang_a, ang_b = pos * inv_a[None, :], pos * inv_b[None, :]

    cols = np.arange(ROPE_COLS)
    in_a = cols < 3 * WIDTH_A
    sub_a = cols // WIDTH_A
    within_a = cols % DIFF_DIM
    rot_mask_a = in_a & (sub_a < 2) & (within_a < rot_a)
    cb = cols - 3 * WIDTH_A
    sub_b = cb // WIDTH_B
    within_b = cb % HEAD_DIM
    rot_mask_b = (~in_a) & (within_b < rot_b)

    first_half = np.where(in_a, within_a < rot_a // 2, within_b < rot_b // 2)
    rot = rot_mask_a | rot_mask_b
    scale = np.ones(ROPE_COLS, np.float32)
    scale[in_a & (sub_a == 0)] = DIFF_DIM ** -0.5 * LOG2E
    scale[(~in_a) & (sub_b == 0)] = HEAD_DIM ** -0.5 * LOG2E

    def spread(tab_a, tab_b, fill):
        def groups(tab, width, reps):
            pad = jnp.full((seq, width - 2 * tab.shape[1]), fill, F32)
            return jnp.tile(jnp.concatenate([tab, tab, pad], axis=1), (1, reps))
        return jnp.concatenate([groups(tab_a, DIFF_DIM, 2 * WIDTH_A // DIFF_DIM),
                                jnp.full((seq, WIDTH_A), fill, F32),
                                groups(tab_b, HEAD_DIM, 2 * WIDTH_B // HEAD_DIM)], axis=1)

    cm = spread(jnp.cos(ang_a), jnp.cos(ang_b), 1.0) * jnp.asarray(scale)[None, :]
    sn = spread(jnp.sin(ang_a), jnp.sin(ang_b), 0.0) * jnp.asarray(scale)[None, :]
    ma = np.where(rot & first_half, -1.0, 0.0).astype(np.float32)[None, :]
    mb = np.where(rot & ~first_half, 1.0, 0.0).astype(np.float32)[None, :]
    return cm, sn, jnp.asarray(ma), jnp.asarray(mb)


PROJ_TM = 512
W_CAST_ROWS = 128


def _cast_rows_to_bf16(src_ref, dst_ref, rows):
    def body(i, c):
        r = pl.multiple_of(i * W_CAST_ROWS, W_CAST_ROWS)
        dst_ref[pl.ds(r, W_CAST_ROWS), :] = src_ref[pl.ds(r, W_CAST_ROWS), :].astype(BF16)
        return c
    lax.fori_loop(0, rows // W_CAST_ROWS, body, 0)


def _proj_kernel(x_ref, w_ref, cm_ref, sn_ref, ma_ref, mb_ref, o_ref, *rest):
    res_refs, (wbf_ref, band_ref, split_ref) = rest[:-3], rest[-3:]

    @pl.when((pl.program_id(0) == 0) & (pl.program_id(1) == 0))
    def _():
        _cast_rows_to_bf16(w_ref, wbf_ref, D_MODEL)

    xb = x_ref[0].astype(BF16)
    tm = xb.shape[0]
    n_chunks = PROJ_W // COL_CHUNK
    for c in range(n_chunks):
        lo, hi = c * COL_CHUNK, (c + 1) * COL_CHUNK
        acc = jnp.dot(xb, wbf_ref[:, lo:hi], preferred_element_type=F32)
        if hi <= ROPE_COLS:
            half = (DIFF_DIM if c == 0 else HEAD_DIM) // ROPE_FRACTION // 2
            up = pltpu.roll(acc, COL_CHUNK - half, 1)
            dn = pltpu.roll(acc, half, 1)
            acc = acc * cm_ref[:, lo:hi] + (up * ma_ref[:, lo:hi] + dn * mb_ref[:, lo:hi]) * sn_ref[:, lo:hi]
        elif c == 2:
            lane = lax.broadcasted_iota(jnp.int32, (1, COL_CHUNK), 1)
            acc = acc * jnp.where(lane >= WIDTH_B, HEAD_DIM ** -0.5 * LOG2E, 1.0)
        o_ref[0, :, lo:hi] = acc.astype(BF16)
        for col in range(max(lo, BAND_LO), min(hi, BAND_LO + BAND_COLS), LANES):
            band_ref[(col - BAND_LO) // LANES] = acc[:, col - lo:col - lo + LANES]

    (d1, d2), (r1_ref, r2_ref) = DILATIONS[1:], res_refs
    for page in range(BAND_COLS // LANES):
        cs = slice(page * LANES, (page + 1) * LANES)
        for r in range(d1):
            rows = band_ref[page, pl.ds(r, tm // d1, stride=d1), :]
            r1_ref[0, r, :, cs] = rows.astype(BF16)
            split_ref[page, r] = rows
        for r in range(d2):
            r2_ref[0, r, :, cs] = split_ref[page, r % d1, pl.ds(r // d1, tm // d2, stride=d1), :].astype(BF16)


def _project(x, w, layer, tables):
    B, S, _ = x.shape
    cm, sn, ma, mb = tables
    tm = PROJ_TM
    res_specs = [pl.BlockSpec((1, d, tm // d, BAND_COLS), lambda p, b: (b, 0, p, 0)) for d in DILATIONS[1:]]
    res_shapes = [jax.ShapeDtypeStruct((B, d, S // d, BAND_COLS), BF16) for d in DILATIONS[1:]]
    return pl.pallas_call(
        _proj_kernel,
        grid=(S // tm, B),
        in_specs=[
            pl.BlockSpec((1, tm, D_MODEL), lambda p, b: (b, p, 0)),
            pl.BlockSpec((None, D_MODEL, PROJ_W), lambda p, b: (layer, 0, 0), pipeline_mode=pl.Buffered(1)),
            pl.BlockSpec((tm, ROPE_COLS), lambda p, b: (p, 0)),
            pl.BlockSpec((tm, ROPE_COLS), lambda p, b: (p, 0)),
            pl.BlockSpec((1, ROPE_COLS), lambda p, b: (0, 0)),
            pl.BlockSpec((1, ROPE_COLS), lambda p, b: (0, 0)),
        ],
        out_specs=[pl.BlockSpec((1, tm, PROJ_W), lambda p, b: (b, p, 0))] + res_specs,
        out_shape=[jax.ShapeDtypeStruct((B, S, PROJ_W), BF16)] + res_shapes,
        scratch_shapes=[pltpu.VMEM((D_MODEL, PROJ_W), BF16),
                        pltpu.VMEM((BAND_COLS // LANES, tm, LANES), F32),
                        pltpu.VMEM((BAND_COLS // LANES, DILATIONS[1], tm // DILATIONS[1], LANES), F32)],
        compiler_params=_cparams(("arbitrary", "arbitrary")),
        name="proj_in",
    )(x, w, cm, sn, ma, mb)


ATT_A_TQ = 256
ATT_A_UNROLL = 8


def _nt_dot(a, b):
    return lax.dot_general(a, b, (((1,), (1,)), ((), ())), preferred_element_type=F32)


def _with_ones(v):
    return jnp.concatenate([v, jnp.ones_like(v)], axis=1)


def _softmax_times_v(e, v1):
    nd = jnp.dot(e, v1, preferred_element_type=F32)
    return nd[:, :LANES] * (1.0 / nd[:, LANES:])


def _attn_a_kernel(lam_ref, g_ref, q_ref, k_ref, v_ref, o_ref, *, lam_init, seq):
    lp = lam_ref[...]
    lam = (jnp.exp(jnp.sum(lp[0:1] * lp[1:2], axis=-1, keepdims=True))
           - jnp.exp(jnp.sum(lp[2:3] * lp[3:4], axis=-1, keepdims=True)) + lam_init)
    lane = lax.broadcasted_iota(jnp.int32, (1, LANES), 1)
    first = lane < HEAD_DIM
    gain = g_ref[...] * (1.0 - lam_init)
    tq = ATT_A_TQ

    def qblock(i, carry):
        r0 = pl.multiple_of(i * tq, tq)
        q = q_ref[0, pl.ds(r0, tq), :]
        k = k_ref[0]
        v1 = _with_ones(v_ref[0])
        outs = []
        for hh in range(2):
            maps = []
            for m in range(2):
                lo = hh * HEAD_DIM + m * DIFF_DIM
                qm = jnp.where((lane >= lo) & (lane < lo + DIFF_DIM), q, jnp.zeros_like(q))
                s = _nt_dot(qm, k)
                e = jnp.exp2(s - jnp.max(s, axis=-1, keepdims=True)).astype(BF16)
                maps.append(_softmax_times_v(e, v1))
            outs.append(maps[0] - lam * maps[1])
        o = jnp.where(first, outs[0], outs[1])
        sq = o * o
        ss0 = jnp.sum(jnp.where(first, sq, 0.0), axis=-1, keepdims=True)
        ss1 = jnp.sum(jnp.where(first, 0.0, sq), axis=-1, keepdims=True)
        ms = jnp.where(first, ss0, ss1) * (1.0 / HEAD_DIM)
        o_ref[0, pl.ds(r0, tq), :] = (o * lax.rsqrt(ms + LN_EPS) * gain).astype(BF16)
        return carry

    lax.fori_loop(0, seq // tq, qblock, 0, unroll=ATT_A_UNROLL)


def _attn_a(qkv, diff_lambda, norm_g, lam_init):
    B, S, _ = qkv.shape
    n_pairs = WIDTH_A // LANES
    g2 = jnp.tile(norm_g.astype(F32), LANES // HEAD_DIM)[None, :]
    blk = lambda off: pl.BlockSpec((1, S, LANES), lambda b, h, off=off: (b, 0, off + h))
    return pl.pallas_call(
        functools.partial(_attn_a_kernel, lam_init=lam_init, seq=S),
        grid=(B, n_pairs),
        in_specs=[
            pl.BlockSpec((4, DIFF_DIM), lambda b, h: (0, 0)),
            pl.BlockSpec((1, LANES), lambda b, h: (0, 0)),
            blk(0), blk(n_pairs), blk(2 * n_pairs),
        ],
        out_specs=pl.BlockSpec((1, S, LANES), lambda b, h: (b, 0, h)),
        out_shape=jax.ShapeDtypeStruct((B, S, WIDTH_A), BF16),
        compiler_params=_cparams(("arbitrary", "arbitrary")),
        name="attn_diff",
    )(diff_lambda.astype(F32), g2, qkv, qkv, qkv)


BAND_TQ = 128
BAND_CHAINS = 48


def _stack_pair(q, lane):
    z = jnp.zeros_like(q)
    return jnp.concatenate([jnp.where(lane < HEAD_DIM, q, z), jnp.where(lane < HEAD_DIM, z, q)], axis=0)


def _attn_b_kernel(q_ref, k_ref, v_ref, o_ref, lse_ref, *, length, dil):
    tq = BAND_TQ
    win = min(tq + 2 * BAND_HALF, length)
    lane = lax.broadcasted_iota(jnp.int32, (1, LANES), 1)

    blocks = [(i * tq, min(max(i * tq - BAND_HALF, 0), length - win)) for i in range(length // tq)]
    row = lax.broadcasted_iota(jnp.int32, (2 * tq, win), 0)
    delta = lax.broadcasted_iota(jnp.int32, (2 * tq, win), 1) - jnp.where(row >= tq, row - tq, row)
    masks = {off: jnp.where(jnp.abs(delta - off) <= BAND_HALF, 0.0, NEG) for off in {q0 - k0 for q0, k0 in blocks}}

    def residue(r, carry):
        for q0, k0 in blocks:
            lse_tile = jnp.zeros((tq, LANES), F32)
            for pr in range(WIDTH_B // LANES):
                cs = slice(pr * LANES, (pr + 1) * LANES)
                q2 = _stack_pair(q_ref[0, r, q0:q0 + tq, cs], lane)
                s = _nt_dot(q2, k_ref[0, r, k0:k0 + win, cs]) + masks[q0 - k0]
                mx = jnp.max(s, axis=-1, keepdims=True)
                e = jnp.exp2(s - mx).astype(BF16)
                nd = jnp.dot(e, _with_ones(v_ref[0, r, k0:k0 + win, cs]), preferred_element_type=F32)
                l = nd[:, LANES:]
                o2 = nd[:, :LANES] * (1.0 / l)
                o_ref[0, r, q0:q0 + tq, cs] = jnp.where(lane < HEAD_DIM, o2[:tq], o2[tq:]).astype(BF16)
                lse2 = (mx + jnp.log2(l)) * LN2
                lse_tile = jnp.where(lane == 2 * pr, lse2[:tq], lse_tile)
                lse_tile = jnp.where(lane == 2 * pr + 1, lse2[tq:], lse_tile)
            lse_ref[0, r, q0:q0 + tq, :] = lse_tile
        return carry

    chains = len(blocks) * (WIDTH_B // LANES)
    lax.fori_loop(0, dil, residue, 0, unroll=min(dil, max(1, BAND_CHAINS // chains)))


def _attn_b(src, dil, first_blk):
    B, _, L, _ = src.shape
    blk = lambda off: pl.BlockSpec((1, dil, L, WIDTH_B), lambda b, off=off: (b, 0, 0, first_blk + off))
    return pl.pallas_call(
        functools.partial(_attn_b_kernel, length=L, dil=dil),
        grid=(B,),
        in_specs=[blk(0), blk(1), blk(2)],
        out_specs=[pl.BlockSpec((1, dil, L, WIDTH_B), lambda b: (b, 0, 0, 0)),
                   pl.BlockSpec((1, dil, L, LANES), lambda b: (b, 0, 0, 0))],
        out_shape=[jax.ShapeDtypeStruct((B, dil, L, WIDTH_B), BF16),
                   jax.ShapeDtypeStruct((B, dil, L, LANES), F32)],
        compiler_params=_cparams(("arbitrary",)),
        name=f"attn_band_d{dil}",
    )(src, src, src)


N_RPB_ROWS = 2 * NA_ROWS - 1
N_RPB_COLS = 2 * NA_COLS - 1
NA_KEYS = NA_ROWS * GRID_W
N_BIAS_VARIANTS = NA_ROWS
NA_ROWS_PER_STEP = 32


def _attn_c_kernel(u_ref, q_ref, k_ref, v_ref, o_ref, bias_ref, *, rows):
    lane = lax.broadcasted_iota(jnp.int32, (1, LANES), 1)
    n_pairs = WIDTH_C // LANES

    @pl.when(pl.program_id(0) == 0)
    def _():
        wide = N_RPB_ROWS * GRID_W
        c = lax.broadcasted_iota(jnp.int32, (GRID_W, wide), 0)
        kc = lax.broadcasted_iota(jnp.int32, (GRID_W, wide), 1) % GRID_W
        tidx = jnp.clip(kc - c + NA_COLS - 1, 0, N_RPB_COLS - 1)
        c5 = lax.broadcasted_iota(jnp.int32, (GRID_W, NA_KEYS), 0)
        kc5 = lax.broadcasted_iota(jnp.int32, (GRID_W, NA_KEYS), 1) % GRID_W
        cstart = jnp.clip(c5 - NA_COLS // 2, 0, GRID_W - NA_COLS)
        valid = (kc5 >= cstart) & (kc5 < cstart + NA_COLS)
        for h in range(N_HEADS_C):
            def tbody(t, acc, h=h):
                row = u_ref[h, pl.ds(t, 1), :]
                return jnp.where(tidx == t, row, acc)
            toep = lax.fori_loop(0, N_RPB_COLS, tbody, jnp.zeros((GRID_W, wide), F32))
            for var in range(N_BIAS_VARIANTS):
                tile = jnp.where(valid, toep[:, var * GRID_W:var * GRID_W + NA_KEYS] * LOG2E, NEG)
                bias_ref[var, h // 2, (h % 2) * GRID_W:(h % 2 + 1) * GRID_W, :] = tile

    def rbody(step, carry):
        for sub in range(NA_ROWS_PER_STEP):
            r = step * NA_ROWS_PER_STEP + sub
            rs = jnp.clip(r - NA_ROWS // 2, 0, rows - NA_ROWS)
            var = rs - r + NA_ROWS - 1
            q0 = pl.multiple_of(r * GRID_W, GRID_W)
            k0 = pl.multiple_of(rs * GRID_W, GRID_W)
            for pr in range(n_pairs):
                cs = slice(pr * LANES, (pr + 1) * LANES)
                q2 = _stack_pair(q_ref[0, pl.ds(q0, GRID_W), cs], lane)
                s = _nt_dot(q2, k_ref[0, pl.ds(k0, NA_KEYS), cs]) + bias_ref[var, pr]
                e = jnp.exp2(s - jnp.max(s, axis=-1, keepdims=True)).astype(BF16)
                o2 = _softmax_times_v(e, _with_ones(v_ref[0, pl.ds(k0, NA_KEYS), cs]))
                o_ref[0, pl.ds(q0, GRID_W), cs] = jnp.where(lane < HEAD_DIM, o2[:GRID_W], o2[GRID_W:]).astype(BF16)
        return carry

    lax.fori_loop(0, rows // NA_ROWS_PER_STEP, rbody, 0)


def _attn_c(qkv, rpb):
    B, S, _ = qkv.shape
    rows = S // GRID_W
    assert rows >= NA_ROWS
    u = jnp.repeat(jnp.transpose(rpb.astype(F32), (0, 2, 1)), GRID_W, axis=2)
    q_blk = (3 * WIDTH_A + 3 * WIDTH_B) // WIDTH_C
    blk = lambda off: pl.BlockSpec((1, S, WIDTH_C), lambda b, off=off: (b, 0, off))
    return pl.pallas_call(
        functools.partial(_attn_c_kernel, rows=rows),
        grid=(B,),
        in_specs=[pl.BlockSpec((N_HEADS_C, N_RPB_COLS, N_RPB_ROWS * GRID_W), lambda b: (0, 0, 0)),
                  blk(q_blk), blk(q_blk + 1), blk(q_blk + 2)],
        out_specs=pl.BlockSpec((1, S, WIDTH_C), lambda b: (b, 0, 0)),
        out_shape=jax.ShapeDtypeStruct((B, S, WIDTH_C), BF16),
        scratch_shapes=[pltpu.VMEM((N_BIAS_VARIANTS, WIDTH_C // LANES, LANES, NA_KEYS), F32)],
        compiler_params=_cparams(("arbitrary",)),
        name="attn_nbr",
    )(u, qkv, qkv, qkv)


OUT_TM = 512
OUT_SPLIT = 2


def _split_bf16(x):
    hi = x.astype(BF16)
    return hi, (x - hi.astype(F32)).astype(BF16)


def _outproj_kernel(x_ref, oa_ref, ob0_ref, ob1_ref, ob2_ref, l0_ref, l1_ref, l2_ref, oc_ref,
                    beta_ref, w_ref, g_ref, b_ref, e_ref, *rest, alpha, route):
    if route:
        wr_ref, o_ref, meta_ref, fields_ref, cnt_ref, wbf_ref, il_ref, carry_ref = rest
    else:
        o_ref, wbf_ref, il_ref = rest

    @pl.when(pl.program_id(0) == 0)
    def _():
        _cast_rows_to_bf16(w_ref, wbf_ref, D_MODEL)
        if route:
            carry_ref[...] = jnp.zeros_like(carry_ref)

    tm = x_ref.shape[0]
    n_pages = WIDTH_B // LANES
    for slot, (dil, ob_r, l_r) in enumerate(zip(DILATIONS[1:], (ob1_ref, ob2_ref), (l1_ref, l2_ref))):
        for r in range(dil):
            rows = pl.ds(r, tm // dil, stride=dil)
            for page in range(n_pages):
                il_ref[slot, page, rows, :] = ob_r[0, r, :, page * LANES:(page + 1) * LANES].astype(F32)
            il_ref[slot, n_pages, rows, :] = l_r[0, r]

    beta = beta_ref[...]
    half = tm // OUT_SPLIT
    for part in range(OUT_SPLIT):
        rs = slice(part * half, (part + 1) * half)
        l0, l1, l2 = l0_ref[0, 0, rs, :], il_ref[0, n_pages, rs, :], il_ref[1, n_pages, rs, :]
        lm = jnp.maximum(jnp.maximum(l0, l1), l2)
        e0, e1, e2 = jnp.exp(l0 - lm), jnp.exp(l1 - lm), jnp.exp(l2 - lm)
        inv = 1.0 / (e0 + e1 + e2)
        wexp = [jnp.dot(jnp.concatenate(_split_bf16(e * inv), axis=1), e_ref[...], preferred_element_type=F32)
                for e in (e0, e1, e2)]
        pages = []
        for page in range(n_pages):
            cs = slice(page * LANES, (page + 1) * LANES)
            branch_vals = (ob0_ref[0, 0, rs, cs].astype(F32), il_ref[0, page, rs, :], il_ref[1, page, rs, :])
            pages.append(sum(w[:, cs] * val for w, val in zip(wexp, branch_vals)))
        ob = jnp.concatenate(pages, axis=1)

        ma = (oa_ref[rs, :].astype(F32) * beta[:, :WIDTH_A]).astype(BF16)
        mb = (ob * beta[:, WIDTH_A:WIDTH_A + WIDTH_B]).astype(BF16)
        mc = (oc_ref[rs, :].astype(F32) * beta[:, WIDTH_A + WIDTH_B:]).astype(BF16)
        acc = jnp.dot(jnp.concatenate([ma, mb, mc], axis=1), wbf_ref[...], preferred_element_type=F32)
        y = _layer_norm(alpha * x_ref[rs, :] + acc, g_ref[...], b_ref[...])
        if route:
            _store_token_tiles(o_ref, y, row0=part * half)
            fields = _route_rows(y, wr_ref, carry_ref)
            fields_ref[:, rs] = fields
            meta_ref[rs, :] = jnp.concatenate([fields, jnp.zeros((LANES - SUBLANES, half), F32)], axis=0).T
        else:
            o_ref[rs, :] = y
    if route:
        cnt_ref[...] = carry_ref[...]


def _store_token_tiles(ref, y, row0=0):
    rows = y.shape[0]
    for j in range(D_MODEL // LANES):
        ref[pl.ds(row0 * SUBLANES + j, rows, stride=SUBLANES), :] = y[:, j * LANES:(j + 1) * LANES]


def _load_token_tiles(ref, rows, row0=0):
    return jnp.concatenate(
        [ref[pl.ds(row0 * SUBLANES + j, rows, stride=SUBLANES), :] for j in range(D_MODEL // LANES)], axis=1)


def _outproj(x2, oa, obs, lses, oc, beta, w, layer, g, b, alpha, w_router=None):
    route = w_router is not None
    N = x2.shape[0]
    tm = OUT_TM
    steps_per_seq = obs[0].shape[2] // tm
    expand = np.zeros((2 * LANES, WIDTH_B), np.float32)
    for h in range(N_HEADS_B):
        expand[h, h * HEAD_DIM:(h + 1) * HEAD_DIM] = 1.0
        expand[LANES + h, h * HEAD_DIM:(h + 1) * HEAD_DIM] = 1.0
    row = lambda w_: pl.BlockSpec((tm, w_), lambda i: (i, 0))
    full = lambda r, c, **kw: pl.BlockSpec((r, c), lambda i: (0, 0), **kw)
    res = lambda d, w_: pl.BlockSpec((1, d, tm // d, w_),
                                     lambda i: (i // steps_per_seq, 0, i % steps_per_seq, 0))
    in_specs = ([row(D_MODEL), row(WIDTH_A)] + [res(d, WIDTH_B) for d in DILATIONS]
                + [res(d, LANES) for d in DILATIONS] + [row(WIDTH_C), full(1, D_MODEL),
                   pl.BlockSpec((None, D_MODEL, D_MODEL), lambda i: (layer, 0, 0), pipeline_mode=pl.Buffered(1)),
                   full(1, D_MODEL), full(1, D_MODEL), full(2 * LANES, WIDTH_B)])
    args = [x2, oa, obs[0], obs[1], obs[2], lses[0], lses[1], lses[2], oc,
            beta[None, :].astype(F32), w, g[None, :].astype(F32), b[None, :].astype(F32), jnp.asarray(expand, BF16)]
    scratch = [pltpu.VMEM((D_MODEL, D_MODEL), BF16),
               pltpu.VMEM((len(DILATIONS) - 1, WIDTH_B // LANES + 1, tm, LANES), F32)]
    if route:
        wr = jnp.zeros((D_MODEL, LANES), F32).at[:, :N_EXPERTS].set(w_router.astype(F32))
        wr_hi = wr.astype(BF16)
        in_specs.append(full(D_MODEL, 2 * LANES))
        args.append(jnp.concatenate([wr_hi, (wr - wr_hi.astype(F32)).astype(BF16)], axis=1))
        out_specs = [pl.BlockSpec((tm * SUBLANES, LANES), lambda i: (i, 0)), row(LANES),
                     pl.BlockSpec((SUBLANES, tm), lambda i: (0, i)), full(ROUTE_ROWS, LANES)]
        out_shape = [jax.ShapeDtypeStruct((N * SUBLANES, LANES), F32), jax.ShapeDtypeStruct((N, LANES), F32),
                     jax.ShapeDtypeStruct((SUBLANES, N), F32), jax.ShapeDtypeStruct((ROUTE_ROWS, LANES), F32)]
        scratch.append(pltpu.VMEM((ROUTE_ROWS, LANES), F32))
    else:
        out_specs = row(D_MODEL)
        out_shape = jax.ShapeDtypeStruct((N, D_MODEL), F32)
    return pl.pallas_call(
        functools.partial(_outproj_kernel, alpha=alpha, route=route),
        grid=(N // tm,),
        in_specs=in_specs,
        out_specs=out_specs,
        out_shape=out_shape,
        scratch_shapes=scratch,
        compiler_params=_cparams(("arbitrary",)),
        name="merge_outproj_ln_route" if route else "merge_outproj_ln",
    )(*args)


CAST_ROWS = 256
DENSE_TM = 512
DENSE_TF = 256


def _load_as_bf16(w_hbm, dst_ref, stage_ref, sem):
    n = w_hbm.shape[0] // CAST_ROWS

    def chunk(c):
        return pltpu.make_async_copy(w_hbm.at[pl.ds(c * CAST_ROWS, CAST_ROWS), :], stage_ref.at[c % 2], sem.at[c % 2])

    chunk(0).start()
    for c in range(n):
        if c + 1 < n:
            chunk(c + 1).start()
        chunk(c).wait()
        dst_ref[c * CAST_ROWS:(c + 1) * CAST_ROWS, :] = stage_ref[c % 2].astype(BF16)


def _dense_ffn_kernel(x_ref, wg_hbm, wu_hbm, wd_hbm, g_ref, b_ref, o_ref,
                      wg_ref, wu_ref, wd_ref, h_ref, wide_stage, narrow_stage, sem, *, alpha):
    @pl.when(pl.program_id(0) == 0)
    def _():
        _load_as_bf16(wg_hbm, wg_ref, wide_stage, sem)
        _load_as_bf16(wu_hbm, wu_ref, wide_stage, sem)
        _load_as_bf16(wd_hbm, wd_ref, narrow_stage, sem)

    x = x_ref[...]
    xb = x.astype(BF16)
    for c in range(h_ref.shape[1] // DENSE_TF):
        cs = slice(c * DENSE_TF, (c + 1) * DENSE_TF)
        gate = jnp.dot(xb, wg_ref[:, cs], preferred_element_type=F32)
        up = jnp.dot(xb, wu_ref[:, cs], preferred_element_type=F32)
        h_ref[:, cs] = (gate * jax.nn.sigmoid(gate) * up).astype(BF16)
    y = jnp.dot(h_ref[...], wd_ref[...], preferred_element_type=F32)
    o_ref[...] = _layer_norm(alpha * x + y, g_ref[...], b_ref[...])


def _dense_ffn(x, w_gate, w_up, w_down, g, b, alpha):
    n_tok = x.shape[0]
    ff = w_gate.shape[1]
    tm = DENSE_TM
    row = pl.BlockSpec((tm, D_MODEL), lambda i: (i, 0))
    hbm = pl.BlockSpec(memory_space=pl.ANY)
    return pl.pallas_call(
        functools.partial(_dense_ffn_kernel, alpha=alpha),
        grid=(n_tok // tm,),
        in_specs=[row, hbm, hbm, hbm,
                  pl.BlockSpec((1, D_MODEL), lambda i: (0, 0)), pl.BlockSpec((1, D_MODEL), lambda i: (0, 0))],
        out_specs=row,
        out_shape=jax.ShapeDtypeStruct((n_tok, D_MODEL), F32),
        scratch_shapes=[pltpu.VMEM((D_MODEL, ff), BF16), pltpu.VMEM((D_MODEL, ff), BF16),
                        pltpu.VMEM((ff, D_MODEL), BF16), pltpu.VMEM((tm, ff), BF16),
                        pltpu.VMEM((2, CAST_ROWS, ff), F32), pltpu.VMEM((2, CAST_ROWS, D_MODEL), F32),
                        pltpu.SemaphoreType.DMA((2,))],
        compiler_params=_cparams(("arbitrary",)),
        name="swiglu_dense",
    )(x, w_gate.astype(F32), w_up.astype(F32), w_down.astype(F32), g[None, :].astype(F32), b[None, :].astype(F32))


FFN_TM = 1024
FFN_TF = 512
FFN_SUB = 256


def _ffn_kernel(te_ref, nt_ref, rows_ref, x_ref, wg_ref, wu_ref, wd_ref, o_ref, xb_ref, acc_ref, *, n_chunks):
    i, j = pl.program_id(0), pl.program_id(1)
    tm = xb_ref.shape[0]

    def swiglu_rows(rs):
        xb = xb_ref[rs, :]
        gate = jnp.dot(xb, wg_ref[...].astype(BF16), preferred_element_type=F32)
        up = jnp.dot(xb, wu_ref[...].astype(BF16), preferred_element_type=F32)
        h = (gate * jax.nn.sigmoid(gate) * up).astype(BF16)
        acc_ref[rs, :] += jnp.dot(h, wd_ref[...].astype(BF16), preferred_element_type=F32)

    @pl.when(i < nt_ref[0])
    def _():
        @pl.when(j == 0)
        def _():
            xb_ref[...] = _load_token_tiles(x_ref, tm).astype(BF16)
            acc_ref[...] = jnp.zeros_like(acc_ref)

        sparse_tile = rows_ref[i] <= tm - FFN_SUB

        @pl.when(jnp.logical_not(sparse_tile))
        def _():
            swiglu_rows(slice(None))

        for sb in range(tm // FFN_SUB - 1):
            @pl.when(sparse_tile & (sb * FFN_SUB < rows_ref[i]))
            def _(sb=sb):
                swiglu_rows(slice(sb * FFN_SUB, (sb + 1) * FFN_SUB))

        @pl.when(j == n_chunks - 1)
        def _():
            _store_token_tiles(o_ref, acc_ref[...])

    @pl.when((i >= nt_ref[0]) & (j == n_chunks - 1))
    def _():
        o_ref[...] = jnp.zeros_like(o_ref)


def _grouped_ffn(x, w_gate, w_up, w_down, tile_expert, n_tiles, tile_rows):
    tm, tf = FFN_TM, FFN_TF
    n_chunks = w_gate.shape[-1] // tf
    max_tiles = tile_expert.shape[0]

    def tile_ix(i, nt):
        return jnp.minimum(i, nt[0] - 1)

    def chunk_ix(i, j, nt):
        return jnp.where(i < nt[0], j, n_chunks - 1)

    blk = (tm * SUBLANES, LANES)
    in_specs = [
        pl.BlockSpec(blk, lambda i, j, te, nt, tr: (tile_ix(i, nt), 0)),
        pl.BlockSpec((None, D_MODEL, tf), lambda i, j, te, nt, tr: (te[tile_ix(i, nt)], 0, chunk_ix(i, j, nt))),
        pl.BlockSpec((None, D_MODEL, tf), lambda i, j, te, nt, tr: (te[tile_ix(i, nt)], 0, chunk_ix(i, j, nt))),
        pl.BlockSpec((None, tf, D_MODEL), lambda i, j, te, nt, tr: (te[tile_ix(i, nt)], chunk_ix(i, j, nt), 0)),
    ]
    return pl.pallas_call(
        functools.partial(_ffn_kernel, n_chunks=n_chunks),
        grid_spec=pltpu.PrefetchScalarGridSpec(
            num_scalar_prefetch=3,
            grid=(max_tiles, n_chunks),
            in_specs=in_specs,
            out_specs=pl.BlockSpec(blk, lambda i, j, te, nt, tr: (i, 0)),
            scratch_shapes=[pltpu.VMEM((tm, D_MODEL), BF16), pltpu.VMEM((tm, D_MODEL), F32)],
        ),
        out_shape=jax.ShapeDtypeStruct(x.shape, F32),
        compiler_params=_cparams(("arbitrary", "arbitrary")),
        name="swiglu_grouped",
    )(tile_expert, n_tiles, tile_rows, x, w_gate, w_up, w_down)


META_E1, META_E2, META_R1, META_R2, META_G1, META_G2 = range(6)


ROUTE_ROWS = 16


def _route_rows(x, wr_ref, carry_ref):
    tb = x.shape[0]
    xh, xl = _split_bf16(x)
    hi_lo = jnp.dot(xh, wr_ref[...], preferred_element_type=F32)
    logits = hi_lo[:, :LANES] + hi_lo[:, LANES:] + jnp.dot(xl, wr_ref[:, :LANES], preferred_element_type=F32)
    logits = logits.T[:ROUTE_ROWS]
    row = lax.broadcasted_iota(jnp.int32, (ROUTE_ROWS, tb), 0).astype(F32)
    logits = jnp.where(row < N_EXPERTS, logits, -jnp.inf)
    m1 = jnp.max(logits, axis=0, keepdims=True)
    i1 = jnp.min(jnp.where(logits == m1, row, float(ROUTE_ROWS)), axis=0, keepdims=True)
    rest = jnp.where(row == i1, -jnp.inf, logits)
    m2 = jnp.max(rest, axis=0, keepdims=True)
    i2 = jnp.min(jnp.where(rest == m2, row, float(ROUTE_ROWS)), axis=0, keepdims=True)
    e21 = jnp.exp(m2 - m1)
    g1 = 1.0 / (1.0 + e21)
    g2 = e21 * g1

    member = ((row == i1) | (row == i2)).astype(F32)
    r_i = lax.broadcasted_iota(jnp.int32, (tb, tb), 0)
    c_i = lax.broadcasted_iota(jnp.int32, (tb, tb), 1)
    earlier = (r_i < c_i).astype(BF16)
    before = jnp.dot(member.astype(BF16), earlier, preferred_element_type=F32)
    rank = carry_ref[:, 0:1] + before
    r1 = jnp.sum(jnp.where(row == i1, rank, 0.0), axis=0, keepdims=True)
    r2 = jnp.sum(jnp.where(row == i2, rank, 0.0), axis=0, keepdims=True)
    carry_ref[...] += jnp.sum(member, axis=1, keepdims=True)

    fields = [None] * SUBLANES
    for slot, val in ((META_E1, i1), (META_E2, i2), (META_R1, r1), (META_R2, r2), (META_G1, g1), (META_G2, g2)):
        fields[slot] = val
    return jnp.concatenate([f if f is not None else jnp.zeros((1, tb), F32) for f in fields], axis=0)


MOVE_TB = 1024
PAD_CHUNK = 64
DMA_ISSUE_UNROLL = 16


def _tile_rows(ref, idx):
    return ref.at[pl.ds(pl.multiple_of(idx * SUBLANES, SUBLANES), SUBLANES), :]


def _zero_fill(pad_ref, xs_hbm, zbuf, zsem):
    zbuf[...] = jnp.zeros_like(zbuf)
    one = _tile_rows(zbuf, 0)
    for wait in (False, True):
        for seg in range(pad_ref.shape[1]):
            start, n = pad_ref[0, seg], pad_ref[1, seg]
            n_single = n % PAD_CHUNK

            def single(t, c):
                cp = pltpu.make_async_copy(one, _tile_rows(xs_hbm, start + t), zsem)
                cp.wait() if wait else cp.start()
                return c

            def bulk(t, c):
                r0 = pl.multiple_of((start + n_single + t * PAD_CHUNK) * SUBLANES, SUBLANES)
                cp = pltpu.make_async_copy(zbuf, xs_hbm.at[pl.ds(r0, PAD_CHUNK * SUBLANES), :], zsem)
                cp.wait() if wait else cp.start()
                return c

            lax.fori_loop(0, n_single, single, 0)
            lax.fori_loop(0, n // PAD_CHUNK, bulk, 0)


def _dispatch_kernel(dest_ref, pad_ref, x_ref, xs_hbm, zbuf, sem, zsem):
    tb = MOVE_TB

    @pl.when(pl.program_id(0) == 0)
    def _():
        _zero_fill(pad_ref, xs_hbm, zbuf, zsem)

    def body(t, c):
        src = _tile_rows(x_ref, t)
        for k in range(TOP_K):
            pltpu.make_async_copy(src, _tile_rows(xs_hbm, dest_ref[0, 0, TOP_K * t + k]), sem).start(priority=k)
        return c

    lax.fori_loop(0, tb, body, 0, unroll=DMA_ISSUE_UNROLL)
    n_rows = TOP_K * tb * SUBLANES
    pltpu.make_async_copy(xs_hbm.at[pl.ds(0, n_rows), :], xs_hbm.at[pl.ds(0, n_rows), :], sem).wait()


def _dispatch(xt, dest, pads, n_slots):
    n_tok = xt.shape[0] // SUBLANES
    tb = MOVE_TB
    dest3 = dest.reshape(n_tok // tb, 1, TOP_K * tb)
    return pl.pallas_call(
        _dispatch_kernel,
        grid=(n_tok // tb,),
        in_specs=[pl.BlockSpec((1, 1, TOP_K * tb), lambda i: (i, 0, 0), memory_space=pltpu.SMEM),
                  pl.BlockSpec(memory_space=pltpu.SMEM),
                  pl.BlockSpec((tb * SUBLANES, LANES), lambda i: (i, 0))],
        out_specs=pl.BlockSpec(memory_space=pl.ANY),
        out_shape=jax.ShapeDtypeStruct((n_slots * SUBLANES, LANES), F32),
        scratch_shapes=[pltpu.VMEM((PAD_CHUNK * SUBLANES, LANES), F32),
                        pltpu.SemaphoreType.DMA(()), pltpu.SemaphoreType.DMA(())],
        compiler_params=_cparams(("arbitrary",)),
        name="moe_dispatch",
    )(dest3, pads, xt)


COMB_TB = 256


def _combine_kernel(dest_ref, next_ref, y_hbm, x_ref, meta_ref, g_ref, b_ref, o_ref, ybuf, sem, *, alpha, n_steps):
    tb = COMB_TB
    i = pl.program_id(0)
    slot = i % 2

    def gather(d_ref, step, s):
        base = (step % (MOVE_TB // tb)) * (TOP_K * tb)

        def body(t, c):
            for k in range(TOP_K):
                pltpu.make_async_copy(_tile_rows(y_hbm, d_ref[0, 0, base + TOP_K * t + k]),
                                      _tile_rows(ybuf.at[s], k * tb + t), sem.at[s]).start(priority=k)
            return c
        lax.fori_loop(0, tb, body, 0, unroll=DMA_ISSUE_UNROLL)

    @pl.when(i == 0)
    def _():
        gather(dest_ref, i, 0)

    @pl.when(i + 1 < n_steps)
    def _():
        gather(next_ref, i + 1, 1 - slot)

    cur = ybuf.at[slot]
    pltpu.make_async_copy(cur, cur, sem.at[slot]).wait()

    meta = meta_ref[...]
    lane = lax.broadcasted_iota(jnp.int32, meta.shape, 1)
    g1 = jnp.sum(jnp.where(lane == META_G1, meta, 0.0), axis=-1, keepdims=True)
    g2 = jnp.sum(jnp.where(lane == META_G2, meta, 0.0), axis=-1, keepdims=True)
    x = _load_token_tiles(x_ref, tb)
    f = g1 * _load_token_tiles(cur, tb) + g2 * _load_token_tiles(cur, tb, row0=tb)
    o_ref[...] = _layer_norm(alpha * x + f, g_ref[...], b_ref[...])


def _combine(yt, xt, dest, meta, g, b, alpha):
    n_tok = xt.shape[0] // SUBLANES
    tb = COMB_TB
    n_steps = n_tok // tb
    per_blk = MOVE_TB // tb
    dest3 = dest.reshape(n_tok // MOVE_TB, 1, TOP_K * MOVE_TB)
    return pl.pallas_call(
        functools.partial(_combine_kernel, alpha=alpha, n_steps=n_steps),
        grid=(n_steps,),
        in_specs=[pl.BlockSpec((1, 1, TOP_K * MOVE_TB), lambda i: (i // per_blk, 0, 0), memory_space=pltpu.SMEM),
                  pl.BlockSpec((1, 1, TOP_K * MOVE_TB),
                               lambda i: (jnp.minimum(i + 1, n_steps - 1) // per_blk, 0, 0),
                               memory_space=pltpu.SMEM),
                  pl.BlockSpec(memory_space=pl.ANY),
                  pl.BlockSpec((tb * SUBLANES, LANES), lambda i: (i, 0)),
                  pl.BlockSpec((tb, LANES), lambda i: (i, 0)),
                  pl.BlockSpec((1, D_MODEL), lambda i: (0, 0)),
                  pl.BlockSpec((1, D_MODEL), lambda i: (0, 0))],
        out_specs=pl.BlockSpec((tb, D_MODEL), lambda i: (i, 0)),
        out_shape=jax.ShapeDtypeStruct((n_tok, D_MODEL), F32),
        scratch_shapes=[pltpu.VMEM((2, TOP_K * tb * SUBLANES, LANES), F32), pltpu.SemaphoreType.DMA((2,))],
        compiler_params=_cparams(("arbitrary",)),
        name="moe_combine_ln",
    )(dest3, dest3, yt, xt, meta, g[None, :].astype(F32), b[None, :].astype(F32))


class MoePlan(NamedTuple):
    dest: jax.Array
    tile_expert: jax.Array
    tile_rows: jax.Array
    n_tiles: jax.Array
    pads: jax.Array
    n_slots: int


def _moe_plan(fields, counts, n_tok):
    tm = FFN_TM
    max_tiles = n_tok * TOP_K // tm + N_EXPERTS
    cnt = counts[:N_EXPERTS, 0].astype(jnp.int32)
    tiles = (cnt + tm - 1) // tm
    tile_end = jnp.cumsum(tiles)
    start = (tile_end - tiles) * tm
    e = fields[META_E1:META_E2 + 1].astype(jnp.int32)
    r = fields[META_R1:META_R2 + 1].astype(jnp.int32)
    experts = jnp.arange(N_EXPERTS, dtype=jnp.int32)[:, None, None]
    group_start = jnp.sum(jnp.where(e[None] == experts, start[:, None, None], 0), axis=0)
    dest = (group_start + r).T.reshape(-1)
    tile_ids = jnp.arange(max_tiles, dtype=jnp.int32)
    tile_expert = jnp.minimum(jnp.sum((tile_ids[:, None] >= tile_end[None, :]).astype(jnp.int32), axis=1),
                              N_EXPERTS - 1)
    first_tile = (tile_end - tiles)[tile_expert]
    tile_rows = jnp.clip(cnt[tile_expert] - (tile_ids - first_tile) * tm, 0, tm).astype(jnp.int32)
    n_tiles = tile_end[-1:]
    pads = jnp.stack([jnp.concatenate([start + cnt, n_tiles * tm]),
                      jnp.concatenate([tiles * tm - cnt, (max_tiles - n_tiles) * tm])]).astype(jnp.int32)
    return MoePlan(dest, tile_expert, tile_rows, n_tiles.astype(jnp.int32), pads, max_tiles * tm)


def kernel(x, w_in, w_out, beta_mix, diff_lambda, diff_norm_g, na_rpb, ln1_g, ln1_b, ln2_g, ln2_b,
           ffn_w_gate, ffn_w_up, ffn_w_down, moe_w_router, moe_w_gate, moe_w_up, moe_w_down):
    B, S, D = x.shape
    assert D == D_MODEL and S % (max(DILATIONS) * BAND_TQ) == 0 and S % PROJ_TM == 0
    depth = w_in.shape[0]
    alpha = (2 * depth) ** 0.25
    n_tok = B * S
    tables = _rope_tables(S)
    x = x.astype(F32)
    for l in range(depth):
        lam_init = 0.8 - 0.6 * math.exp(-0.3 * l)
        moe = l % 2 == 1
        qkv, *res_major = _project(x, w_in, l, tables)
        oa = _attn_a(qkv, diff_lambda[l], diff_norm_g[l], lam_init).reshape(n_tok, WIDTH_A)
        band = [_attn_b(qkv[:, None], DILATIONS[0], BAND_LO // WIDTH_B)]
        band += [_attn_b(src, d, 0) for d, src in zip(DILATIONS[1:], res_major)]
        oc = _attn_c(qkv, na_rpb[l]).reshape(n_tok, WIDTH_C)
        j = l // 2
        mixed = _outproj(x.reshape(n_tok, D), oa, [o for o, _ in band], [s for _, s in band], oc,
                         beta_mix[l], w_out, l, ln1_g[l], ln1_b[l], alpha,
                         w_router=moe_w_router[j] if moe else None)
        if not moe:
            x = _dense_ffn(mixed, ffn_w_gate[j], ffn_w_up[j], ffn_w_down[j], ln2_g[l], ln2_b[l], alpha)
        else:
            x1, meta, fields, counts = mixed
            plan = _moe_plan(fields, counts, n_tok)
            xs = _dispatch(x1, plan.dest, plan.pads, plan.n_slots)
            ys = _grouped_ffn(xs, moe_w_gate[j], moe_w_up[j], moe_w_down[j],
                              plan.tile_expert, plan.n_tiles, plan.tile_rows)
            x = _combine(ys, x1, plan.dest, meta, ln2_g[l], ln2_b[l], alpha)
        x = x.reshape(B, S, D)
    return x
```
